```python
import math
import jax, jax.numpy as jnp
from jax import lax
import numpy as np

D_MODEL = 1024
BATCH = 8
SEQ = 2048
DEPTH = 4
DEC_BATCH = 128
DEC_SEQ = 1
PAST_LEN = 2048
PAGE_SIZE = 128

CONV_WIDTH = 4
RMS_EPS = 1e-6
GDN_HEADS = 4
GDN_DK = 128
GDN_DV = 128
GDN_CHUNK = 64
GDN_QK = GDN_HEADS * GDN_DK
GDN_VW = GDN_HEADS * GDN_DV
GDN_CONV_DIM = 2 * GDN_QK + GDN_VW
SSD_HEADS = 8
SSD_HEADDIM = 64
SSD_GROUPS = 2
SSD_STATE = 128
SSD_CHUNK = 128
SSD_INNER = SSD_HEADS * SSD_HEADDIM
SSD_CONV_DIM = SSD_INNER + 2 * SSD_GROUPS * SSD_STATE
ATT_GROUPS = ((128, 1), (512, 4), (2048, 16))
N_ATT_GROUPS = len(ATT_GROUPS)
ATT_HEADS_PER_GROUP = 4
ATT_HEAD_DIM = 64
ATT_HEADS = N_ATT_GROUPS * ATT_HEADS_PER_GROUP
ATT_OUT = ATT_HEADS_PER_GROUP * ATT_HEAD_DIM
D_FF = ((8 * D_MODEL // 3 + 127) // 128) * 128
N_BRANCH = 3
IN_SPLITS = (GDN_CONV_DIM, GDN_VW, GDN_HEADS, GDN_HEADS, SSD_INNER, SSD_CONV_DIM, SSD_HEADS,
             3 * ATT_HEADS * ATT_HEAD_DIM, N_BRANCH * D_MODEL)
D_IN = sum(IN_SPLITS)

kernel_name = 'hybrid_gdn_ssd_dilated_attn_step'

F32 = jnp.float32


def rmsnorm(x, w):
    xf = x.astype(F32)
    y = xf * lax.rsqrt(jnp.mean(xf * xf, axis=-1, keepdims=True) + RMS_EPS)
    return (y * w.astype(F32)).astype(x.dtype)


def l2norm(x):
    xf = x.astype(F32)
    return xf * lax.rsqrt(jnp.sum(xf * xf, axis=-1, keepdims=True) + RMS_EPS)


def swiglu(x, w_in, w_out):
    g, up = jnp.split(x @ w_in, 2, axis=-1)
    return (jax.nn.silu(g) * up) @ w_out


def causal_dwconv(x, buf, w, b=None):
    K = w.shape[0]
    L = x.shape[1]
    xp = jnp.concatenate([buf.astype(x.dtype), x], axis=1)
    y = xp[:, 0:L] * w[0]
    for i in range(1, K):
        y = y + xp[:, i:i + L] * w[i]
    if b is not None:
        y = y + b
    return y, xp[:, -(K - 1):]


def alibi_slopes():
    idx = np.arange(1, ATT_HEADS + 1, dtype=np.float32)
    slopes = np.exp2(-8.0 * idx / ATT_HEADS).astype(np.float32)
    return jnp.asarray(slopes).reshape(N_ATT_GROUPS, ATT_HEADS_PER_GROUP)


def gated_delta_chunked(q, k, v, g, beta, s0):
    bsz, L, H, dk = q.shape
    dv = v.shape[-1]
    C = min(GDN_CHUNK, L)
    n = -(-L // C)
    pad = n * C - L

    def prep(t):
        t = t.astype(F32)
        t = jnp.pad(t, [(0, 0), (0, pad)] + [(0, 0)] * (t.ndim - 2))
        t = t.reshape((bsz, n, C) + t.shape[2:])
        return jnp.moveaxis(t, 3, 2)

    q, k, v, g, beta = prep(q), prep(k), prep(v), prep(g), prep(beta)
    gam = jnp.cumsum(g, axis=-1)
    incl = jnp.tril(jnp.ones((C, C), bool))
    strict = jnp.tril(jnp.ones((C, C), bool), -1)
    dec = jnp.exp(jnp.where(incl, gam[..., :, None] - gam[..., None, :], -jnp.inf))
    kk = jnp.einsum('bnhcd,bnhsd->bnhcs', k, k)
    a_mat = jnp.where(strict, beta[..., :, None] * kk * dec, 0.0) + jnp.eye(C, dtype=F32)
    rhs = jnp.concatenate([v * beta[..., None], k * (beta * jnp.exp(gam))[..., None]], axis=-1)
    sol = lax.linalg.triangular_solve(a_mat, rhs, left_side=True, lower=True, unit_diagonal=True)
    u0, w = sol[..., :dv], sol[..., dv:]
    qk = jnp.einsum('bnhcd,bnhsd->bnhcs', q, k) * dec
    q_dec = q * jnp.exp(gam)[..., None]
    k_dec = k * jnp.exp(gam[..., -1:] - gam)[..., None]
    g_last = jnp.exp(gam[..., -1])

    def step(S, xs):
        u0_c, w_c, q_c, qk_c, k_c, gl = xs
        u = u0_c - jnp.einsum('bhcd,bhde->bhce', w_c, S)
        o = jnp.einsum('bhcd,bhde->bhce', q_c, S) + jnp.einsum('bhcs,bhse->bhce', qk_c, u)
        S = S * gl[..., None, None] + jnp.einsum('bhcd,bhce->bhde', k_c, u)
        return S, o

    xs = tuple(jnp.moveaxis(t, 1, 0) for t in (u0, w, q_dec, qk, k_dec, g_last))
    S, o = lax.scan(step, s0.astype(F32), xs)
    o = jnp.moveaxis(jnp.moveaxis(o, 0, 1), 2, 3).reshape(bsz, n * C, H, dv)[:, :L]
    return o, S


def ssd_chunked(x, dt, a, bm, cm, h0):
    bsz, L, H, P = x.shape
    C = min(SSD_CHUNK, L)
    n = -(-L // C)
    pad = n * C - L

    def prep(t):
        t = t.astype(F32)
        t = jnp.pad(t, [(0, 0), (0, pad)] + [(0, 0)] * (t.ndim - 2))
        return t.reshape((bsz, n, C) + t.shape[2:])

    xd = prep(x.astype(F32) * dt[..., None])
    da = prep(dt * a)
    bm, cm = prep(bm), prep(cm)
    acum = jnp.cumsum(da, axis=2)
    incl = jnp.tril(jnp.ones((C, C), bool))
    seg = acum[:, :, :, None, :] - acum[:, :, None, :, :]
    lmat = jnp.exp(jnp.where(incl[:, :, None], seg, -jnp.inf))
    scores = jnp.einsum('bnlhk,bnshk->bnlsh', cm, bm) * lmat
    y_diag = jnp.einsum('bnlsh,bnshp->bnlhp', scores, xd)
    decay_states = jnp.exp(acum[:, :, -1:, :] - acum)
    states = jnp.einsum('bnshk,bnshp->bnhpk', bm * decay_states[..., None], xd)
    chunk_decay = jnp.exp(acum[:, :, -1, :])

    def step(h, xs):
        st, cd = xs
        return h * cd[..., None, None] + st, h

    h_fin, h_in = lax.scan(step, h0.astype(F32), (jnp.moveaxis(states, 1, 0), jnp.moveaxis(chunk_decay, 1, 0)))
    h_in = jnp.moveaxis(h_in, 0, 1)
    y_off = jnp.einsum('bnlhk,bnhpk->bnlhp', cm * jnp.exp(acum)[..., None], h_in)
    y = (y_diag + y_off).reshape(bsz, n * C, H, P)[:, :L]
    return y, h_fin


def dilated_band_attention(q, k, v, slopes, window, dil):
    bsz, L, H, hd = q.shape
    band = window // dil
    unit = band * dil
    nb = -(-L // unit)
    Lp = nb * unit
    pad = Lp - L

    def blocks(t):
        t = jnp.pad(t.astype(F32), ((0, 0), (0, pad), (0, 0), (0, 0)))
        return t.reshape(bsz, nb, band, dil, H, hd)

    def with_prev(t):
        prev = jnp.pad(t, ((0, 0), (1, 0), (0, 0), (0, 0), (0, 0), (0, 0)))[:, :-1]
        return jnp.concatenate([prev, t], axis=2)

    qb = blocks(q)
    kk, vv = with_prev(blocks(k)), with_prev(blocks(v))
    s = jnp.einsum('bnirhd,bnjrhd->bnrhij', qb, kk)
    i = jnp.arange(band)[:, None]
    j = jnp.arange(2 * band)[None, :]
    du = i + band - j
    blk = jnp.arange(nb)[:, None, None]
    valid = (du >= 0) & (du <= band) & ((blk > 0) | (j >= band))
    bias = -(slopes.astype(F32) * dil)[:, None, None] * du
    s = jnp.where(valid[:, None, None], s + bias, -jnp.inf)
    lse = jax.nn.logsumexp(s, axis=-1)
    o = jnp.einsum('bnrhij,bnjrhd->bnirhd', jnp.exp(s - lse[..., None]), vv)
    o = o.reshape(bsz, Lp, H, hd)[:, :L]
    lse = jnp.moveaxis(lse, -1, 2).reshape(bsz, Lp, H)[:, :L]
    return o, lse


def dilated_gather_attention(q, rows_new, kv_buf, slopes, window, dil):
    bsz, Lq, H, hd = q.shape
    Lw = kv_buf.shape[1]
    n_keys = window // dil + 1
    i = jnp.arange(Lq)[:, None]
    j = jnp.arange(n_keys)[None, :]
    idx = Lw + i - j * dil
    valid = idx >= 0
    from_buf = (idx < Lw)[:, :, None, None, None]
    rows = jnp.where(from_buf,
                     kv_buf[:, jnp.clip(idx, 0, Lw - 1)].astype(F32),
                     rows_new[:, jnp.clip(idx - Lw, 0, Lq - 1)].astype(F32))
    kg, vg = rows[:, :, :, 0], rows[:, :, :, 1]
    s = jnp.einsum('bqhd,bqjhd->bhqj', q.astype(F32), kg) - (slopes.astype(F32) * dil)[:, None, None] * j
    s = jnp.where(valid, s, -jnp.inf)
    lse = jax.nn.logsumexp(s, axis=-1)
    o = jnp.einsum('bhqj,bqjhd->bqhd', jnp.exp(s - lse[..., None]), vg)
    return o, jnp.moveaxis(lse, 1, 2)


def token_mixing(u, p, l, s_gdn, c_gdn, s_ssd, c_ssd, kv_bufs):
    bsz, L, _ = u.shape
    offs = np.cumsum(np.array(IN_SPLITS))[:-1].tolist()
    (a_qkv, a_gate, a_beta, a_alpha, b_z, b_xbc, b_dt, c_qkv, gate_logits) = jnp.split(u @ p['w_in'][l], offs, axis=-1)

    qkv, c_gdn_new = causal_dwconv(a_qkv, c_gdn, p['gdn_conv_w'][l])
    qkv = jax.nn.silu(qkv)
    q, k, v = jnp.split(qkv, [GDN_QK, 2 * GDN_QK], axis=-1)
    q = l2norm(q.reshape(bsz, L, GDN_HEADS, GDN_DK)) * GDN_DK ** -0.5
    k = l2norm(k.reshape(bsz, L, GDN_HEADS, GDN_DK))
    v = v.reshape(bsz, L, GDN_HEADS, GDN_DV)
    beta = jax.nn.sigmoid(a_beta.astype(F32))
    g = -jnp.exp(p['gdn_a_log'][l].astype(F32)) * jax.nn.softplus((a_alpha + p['gdn_dt_bias'][l]).astype(F32))
    o_a, s_gdn_new = gated_delta_chunked(q, k, v, g, beta, s_gdn)
    o_a = rmsnorm(o_a, p['gdn_norm_w'][l]) * jax.nn.silu(a_gate.reshape(bsz, L, GDN_HEADS, GDN_DV).astype(F32))
    o_a = o_a.reshape(bsz, L, GDN_VW).astype(u.dtype)

    xbc, c_ssd_new = causal_dwconv(b_xbc, c_ssd, p['ssd_conv_w'][l], p['ssd_conv_b'][l])
    xbc = jax.nn.silu(xbc)
    xs, bm, cm = jnp.split(xbc, [SSD_INNER, SSD_INNER + SSD_GROUPS * SSD_STATE], axis=-1)
    xs = xs.reshape(bsz, L, SSD_HEADS, SSD_HEADDIM)
    rep = SSD_HEADS // SSD_GROUPS
    bm = jnp.repeat(bm.reshape(bsz, L, SSD_GROUPS, SSD_STATE), rep, axis=2)
    cm = jnp.repeat(cm.reshape(bsz, L, SSD_GROUPS, SSD_STATE), rep, axis=2)
    dt = jax.nn.softplus((b_dt + p['ssd_dt_bias'][l]).astype(F32))
    a = -jnp.exp(p['ssd_a_log'][l].astype(F32))
    y, s_ssd_new = ssd_chunked(xs, dt, a, bm, cm, s_ssd)
    y = y + p['ssd_d'][l].astype(F32)[:, None] * xs.astype(F32)
    y = y.reshape(bsz, L, SSD_INNER) * jax.nn.silu(b_z.astype(F32))
    y = rmsnorm(y.reshape(bsz, L, SSD_GROUPS, SSD_INNER // SSD_GROUPS),
                p['ssd_norm_w'][l].reshape(SSD_GROUPS, SSD_INNER // SSD_GROUPS))
    o_b = y.reshape(bsz, L, SSD_INNER).astype(u.dtype)

    cqkv = c_qkv.reshape(bsz, L, 3, N_ATT_GROUPS, ATT_HEADS_PER_GROUP, ATT_HEAD_DIM)
    cq = rmsnorm(cqkv[:, :, 0], p['q_norm'][l]) * ATT_HEAD_DIM ** -0.5
    ck = rmsnorm(cqkv[:, :, 1], p['k_norm'][l])
    cv = cqkv[:, :, 2]
    slopes = alibi_slopes()
    outs, lses, kv_new = [], [], []
    for gi, (win, dil) in enumerate(ATT_GROUPS):
        qg, kg, vg = cq[:, :, gi], ck[:, :, gi], cv[:, :, gi]
        rows = jnp.stack([kg, vg], axis=2)
        if kv_bufs is None:
            o, lse = dilated_band_attention(qg, kg, vg, slopes[gi], win, dil)
            kv_new.append(rows[:, -min(win, L):])
        else:
            o, lse = dilated_gather_attention(qg, rows, kv_bufs[gi], slopes[gi], win, dil)
            kv_new.append(rows)
        outs.append(o)
        lses.append(lse)
    wts = jax.nn.softmax(jnp.stack(lses, axis=2), axis=2)
    o_c = jnp.sum(wts[..., None] * jnp.stack(outs, axis=2), axis=2).reshape(bsz, L, ATT_OUT).astype(u.dtype)

    gates = jax.nn.sigmoid(gate_logits + p['gate_b'][l]).reshape(bsz, L, N_BRANCH, D_MODEL)
    merged = (gates[:, :, 0] * (o_a @ p['w_br_a'][l])
              + gates[:, :, 1] * (o_b @ p['w_br_b'][l])
              + gates[:, :, 2] * (o_c @ p['w_br_c'][l]))
    out = merged @ p['w_out'][l]
    return out, (s_gdn_new.astype(s_gdn.dtype), c_gdn_new, s_ssd_new.astype(s_ssd.dtype), c_ssd_new, *kv_new)


def run_trunk(x, p, s_gdn, c_gdn, s_ssd, c_ssd, kv_bufs):
    h = x
    new = [[] for _ in range(4 + N_ATT_GROUPS)]
    for l in range(DEPTH):
        h = h + 0.5 * swiglu(rmsnorm(h, p['ffn1_norm'][l]), p['ffn1_w_in'][l], p['ffn1_w_out'][l])
        bufs = None if kv_bufs is None else [b[l] for b in kv_bufs]
        mix, states = token_mixing(rmsnorm(h, p['mix_norm'][l]), p, l, s_gdn[l], c_gdn[l], s_ssd[l], c_ssd[l], bufs)
        h = h + mix
        h = h + 0.5 * swiglu(rmsnorm(h, p['ffn2_norm'][l]), p['ffn2_w_in'][l], p['ffn2_w_out'][l])
        for lst, s in zip(new, states):
            lst.append(s)
    return h, [jnp.stack(lst, axis=0) for lst in new]


def setup_inputs(seed: int = 0) -> dict:
    key = jax.random.key(seed)
    ks = iter(jax.random.split(key, 48))

    def nrm(shape, scale):
        return jax.random.normal(next(ks), shape, F32) * scale

    def gain(shape):
        return 1.0 + nrm(shape, 0.02)

    def dt_bias(shape):
        dt = jnp.exp(jax.random.uniform(next(ks), shape, F32, math.log(1e-3), math.log(1e-1)))
        return dt + jnp.log(-jnp.expm1(-dt))

    def a_log(shape):
        return jnp.log(jax.random.uniform(next(ks), shape, F32, 1.0, 16.0))

    def win_len(gi):
        return min(ATT_GROUPS[gi][0], PAST_LEN)

    kv_tail = (2, ATT_HEADS_PER_GROUP, ATT_HEAD_DIM)
    return {
        'x_prompt': nrm((BATCH, SEQ, D_MODEL), 1.0),
        'x_sample': nrm((DEC_BATCH, DEC_SEQ, D_MODEL), 1.0),
        'state_gdn': nrm((DEPTH, DEC_BATCH, GDN_HEADS, GDN_DK, GDN_DV), 0.5),
        'state_gdn_conv': nrm((DEPTH, DEC_BATCH, CONV_WIDTH - 1, GDN_CONV_DIM), 1.0),
        'state_ssd': nrm((DEPTH, DEC_BATCH, SSD_HEADS, SSD_HEADDIM, SSD_STATE), 0.5),
        'state_ssd_conv': nrm((DEPTH, DEC_BATCH, CONV_WIDTH - 1, SSD_CONV_DIM), 1.0),
        'cache_kv_w128': nrm((DEPTH, DEC_BATCH, win_len(0)) + kv_tail, 1.0),
        'cache_kv_w512': nrm((DEPTH, DEC_BATCH, win_len(1)) + kv_tail, 1.0),
        'cache_kv_w2048': nrm((DEPTH, DEC_BATCH, win_len(2)) + kv_tail, 1.0),
        'ffn1_norm': gain((DEPTH, D_MODEL)),
        'ffn1_w_in': nrm((DEPTH, D_MODEL, 2 * D_FF), D_MODEL ** -0.5),
        'ffn1_w_out': nrm((DEPTH, D_FF, D_MODEL), D_FF ** -0.5),
        'mix_norm': gain((DEPTH, D_MODEL)),
        'w_in': nrm((DEPTH, D_MODEL, D_IN), D_MODEL ** -0.5),
        'gate_b': nrm((DEPTH, N_BRANCH * D_MODEL), 0.1),
        'gdn_conv_w': nrm((DEPTH, CONV_WIDTH, GDN_CONV_DIM), CONV_WIDTH ** -0.5),
        'gdn_a_log': a_log((DEPTH, GDN_HEADS)),
        'gdn_dt_bias': dt_bias((DEPTH, GDN_HEADS)),
        'gdn_norm_w': gain((DEPTH, GDN_DV)),
        'ssd_conv_w': nrm((DEPTH, CONV_WIDTH, SSD_CONV_DIM), CONV_WIDTH ** -0.5),
        'ssd_conv_b': nrm((DEPTH, SSD_CONV_DIM), 0.02),
        'ssd_a_log': a_log((DEPTH, SSD_HEADS)),
        'ssd_dt_bias': dt_bias((DEPTH, SSD_HEADS)),
        'ssd_d': 1.0 + nrm((DEPTH, SSD_HEADS), 0.1),
        'ssd_norm_w': gain((DEPTH, SSD_INNER)),
        'q_norm': gain((DEPTH, ATT_HEAD_DIM)),
        'k_norm': gain((DEPTH, ATT_HEAD_DIM)),
        'w_br_a': nrm((DEPTH, GDN_VW, D_MODEL), GDN_VW ** -0.5),
        'w_br_b': nrm((DEPTH, SSD_INNER, D_MODEL), SSD_INNER ** -0.5),
        'w_br_c': nrm((DEPTH, ATT_OUT, D_MODEL), ATT_OUT ** -0.5),
        'w_out': nrm((DEPTH, D_MODEL, D_MODEL), D_MODEL ** -0.5),
        'ffn2_norm': gain((DEPTH, D_MODEL)),
        'ffn2_w_in': nrm((DEPTH, D_MODEL, 2 * D_FF), D_MODEL ** -0.5),
        'ffn2_w_out': nrm((DEPTH, D_FF, D_MODEL), D_FF ** -0.5),
    }


def reference(x_prompt, x_sample, state_gdn, state_gdn_conv, state_ssd, state_ssd_conv,
              cache_kv_w128, cache_kv_w512, cache_kv_w2048,
              ffn1_norm, ffn1_w_in, ffn1_w_out, mix_norm, w_in, gate_b,
              gdn_conv_w, gdn_a_log, gdn_dt_bias, gdn_norm_w,
              ssd_conv_w, ssd_conv_b, ssd_a_log, ssd_dt_bias, ssd_d, ssd_norm_w,
              q_norm, k_norm, w_br_a, w_br_b, w_br_c, w_out,
              ffn2_norm, ffn2_w_in, ffn2_w_out):
    p = {'ffn1_norm': ffn1_norm, 'ffn1_w_in': ffn1_w_in, 'ffn1_w_out': ffn1_w_out,
         'mix_norm': mix_norm, 'w_in': w_in, 'gate_b': gate_b,
         'gdn_conv_w': gdn_conv_w, 'gdn_a_log': gdn_a_log, 'gdn_dt_bias': gdn_dt_bias, 'gdn_norm_w': gdn_norm_w,
         'ssd_conv_w': ssd_conv_w, 'ssd_conv_b': ssd_conv_b, 'ssd_a_log': ssd_a_log,
         'ssd_dt_bias': ssd_dt_bias, 'ssd_d': ssd_d, 'ssd_norm_w': ssd_norm_w,
         'q_norm': q_norm, 'k_norm': k_norm, 'w_br_a': w_br_a, 'w_br_b': w_br_b, 'w_br_c': w_br_c,
         'w_out': w_out, 'ffn2_norm': ffn2_norm, 'ffn2_w_in': ffn2_w_in, 'ffn2_w_out': ffn2_w_out}
    bp, dtype = x_prompt.shape[0], x_prompt.dtype
    y_prompt, st_p = run_trunk(
        x_prompt, p,
        jnp.zeros((DEPTH, bp, GDN_HEADS, GDN_DK, GDN_DV), dtype),
        jnp.zeros((DEPTH, bp, CONV_WIDTH - 1, GDN_CONV_DIM), dtype),
        jnp.zeros((DEPTH, bp, SSD_HEADS, SSD_HEADDIM, SSD_STATE), dtype),
        jnp.zeros((DEPTH, bp, CONV_WIDTH - 1, SSD_CONV_DIM), dtype),
        None)
    y_sample, st_s = run_trunk(x_sample, p, state_gdn, state_gdn_conv, state_ssd, state_ssd_conv,
                               [cache_kv_w128, cache_kv_w512, cache_kv_w2048])
    gdn_p, gdn_conv_p, ssd_p, ssd_conv_p, kv128_p, kv512_p, kv2048_p = st_p
    gdn_s, gdn_conv_s, ssd_s, ssd_conv_s, kv128_s, kv512_s, kv2048_s = st_s
    return (y_prompt, y_sample, gdn_p, gdn_s, gdn_conv_p, gdn_conv_s, ssd_p, ssd_s, ssd_conv_p, ssd_conv_s,
            kv128_p, kv128_s, kv512_p, kv512_s, kv2048_p, kv2048_s)
```

```python
import functools
import math
import jax, jax.numpy as jnp
from jax import lax
import numpy as np
from jax.experimental import pallas as pl
from jax.experimental.pallas import tpu as pltpu

D_MODEL = 1024
BATCH = 8
SEQ = 2048
DEPTH = 4
DEC_BATCH = 128
DEC_SEQ = 1
PAST_LEN = 2048

CONV_WIDTH = 4
RMS_EPS = 1e-6
GDN_HEADS = 4
GDN_DK = 128
GDN_DV = 128
GDN_CHUNK = 64
GDN_QK = GDN_HEADS * GDN_DK
GDN_VW = GDN_HEADS * GDN_DV
GDN_CONV_DIM = 2 * GDN_QK + GDN_VW
SSD_HEADS = 8
SSD_HEADDIM = 64
SSD_GROUPS = 2
SSD_STATE = 128
SSD_CHUNK = 128
SSD_INNER = SSD_HEADS * SSD_HEADDIM
SSD_CONV_DIM = SSD_INNER + 2 * SSD_GROUPS * SSD_STATE
ATT_GROUPS = ((128, 1), (512, 4), (2048, 16))
N_ATT_GROUPS = len(ATT_GROUPS)
ATT_HEADS_PER_GROUP = 4
ATT_HEAD_DIM = 64
ATT_HEADS = N_ATT_GROUPS * ATT_HEADS_PER_GROUP
ATT_OUT = ATT_HEADS_PER_GROUP * ATT_HEAD_DIM
D_FF = ((8 * D_MODEL // 3 + 127) // 128) * 128
N_BRANCH = 3
IN_SPLITS = (GDN_CONV_DIM, GDN_VW, GDN_HEADS, GDN_HEADS, SSD_INNER, SSD_CONV_DIM, SSD_HEADS,
             3 * ATT_HEADS * ATT_HEAD_DIM, N_BRANCH * D_MODEL)
D_IN = sum(IN_SPLITS)

F32 = jnp.float32
BF16 = jnp.bfloat16
VMEM_LIMIT_BYTES = 56 * 1024 * 1024


def rmsnorm(x, w):
    xf = x.astype(F32)
    y = xf * lax.rsqrt(jnp.mean(xf * xf, axis=-1, keepdims=True) + RMS_EPS)
    return (y * w.astype(F32)).astype(x.dtype)


def l2norm(x):
    xf = x.astype(F32)
    return xf * lax.rsqrt(jnp.sum(xf * xf, axis=-1, keepdims=True) + RMS_EPS)


def causal_dwconv(x, buf, w, b=None):
    K = w.shape[0]
    L = x.shape[1]
    xp = jnp.concatenate([buf.astype(x.dtype), x], axis=1)
    y = xp[:, 0:L] * w[0]
    for i in range(1, K):
        y = y + xp[:, i:i + L] * w[i]
    if b is not None:
        y = y + b
    return y, xp[:, -(K - 1):]


def alibi_slopes():
    idx = np.arange(1, ATT_HEADS + 1, dtype=np.float32)
    slopes = np.exp2(-8.0 * idx / ATT_HEADS).astype(np.float32)
    return jnp.asarray(slopes).reshape(N_ATT_GROUPS, ATT_HEADS_PER_GROUP)


def gated_delta_chunked(q, k, v, g, beta, s0):
    bsz, L, H, dk = q.shape
    dv = v.shape[-1]
    C = min(GDN_CHUNK, L)
    n = -(-L // C)
    pad = n * C - L

    def prep(t):
        t = t.astype(F32)
        t = jnp.pad(t, [(0, 0), (0, pad)] + [(0, 0)] * (t.ndim - 2))
        t = t.reshape((bsz, n, C) + t.shape[2:])
        return jnp.moveaxis(t, 3, 2)

    q, k, v, g, beta = prep(q), prep(k), prep(v), prep(g), prep(beta)
    gam = jnp.cumsum(g, axis=-1)
    incl = jnp.tril(jnp.ones((C, C), bool))
    strict = jnp.tril(jnp.ones((C, C), bool), -1)
    dec = jnp.exp(jnp.where(incl, gam[..., :, None] - gam[..., None, :], -jnp.inf))
    kk = jnp.einsum('bnhcd,bnhsd->bnhcs', k, k)
    a_mat = jnp.where(strict, beta[..., :, None] * kk * dec, 0.0) + jnp.eye(C, dtype=F32)
    rhs = jnp.concatenate([v * beta[..., None], k * (beta * jnp.exp(gam))[..., None]], axis=-1)
    sol = lax.linalg.triangular_solve(a_mat, rhs, left_side=True, lower=True, unit_diagonal=True)
    u0, w = sol[..., :dv], sol[..., dv:]
    qk = jnp.einsum('bnhcd,bnhsd->bnhcs', q, k) * dec
    q_dec = q * jnp.exp(gam)[..., None]
    k_dec = k * jnp.exp(gam[..., -1:] - gam)[..., None]
    g_last = jnp.exp(gam[..., -1])

    def step(S, xs):
        u0_c, w_c, q_c, qk_c, k_c, gl = xs
        u = u0_c - jnp.einsum('bhcd,bhde->bhce', w_c, S)
        o = jnp.einsum('bhcd,bhde->bhce', q_c, S) + jnp.einsum('bhcs,bhse->bhce', qk_c, u)
        S = S * gl[..., None, None] + jnp.einsum('bhcd,bhce->bhde', k_c, u)
        return S, o

    xs = tuple(jnp.moveaxis(t, 1, 0) for t in (u0, w, q_dec, qk, k_dec, g_last))
    S, o = lax.scan(step, s0.astype(F32), xs)
    o = jnp.moveaxis(jnp.moveaxis(o, 0, 1), 2, 3).reshape(bsz, n * C, H, dv)[:, :L]
    return o, S


def ssd_chunked(x, dt, a, bm, cm, h0):
    bsz, L, H, P = x.shape
    C = min(SSD_CHUNK, L)
    n = -(-L // C)
    pad = n * C - L

    def prep(t):
        t = t.astype(F32)
        t = jnp.pad(t, [(0, 0), (0, pad)] + [(0, 0)] * (t.ndim - 2))
        return t.reshape((bsz, n, C) + t.shape[2:])

    xd = prep(x.astype(F32) * dt[..., None])
    da = prep(dt * a)
    bm, cm = prep(bm), prep(cm)
    acum = jnp.cumsum(da, axis=2)
    incl = jnp.tril(jnp.ones((C, C), bool))
    seg = acum[:, :, :, None, :] - acum[:, :, None, :, :]
    lmat = jnp.exp(jnp.where(incl[:, :, None], seg, -jnp.inf))
    scores = jnp.einsum('bnlhk,bnshk->bnlsh', cm, bm) * lmat
    y_diag = jnp.einsum('bnlsh,bnshp->bnlhp', scores, xd)
    decay_states = jnp.exp(acum[:, :, -1:, :] - acum)
    states = jnp.einsum('bnshk,bnshp->bnhpk', bm * decay_states[..., None], xd)
    chunk_decay = jnp.exp(acum[:, :, -1, :])

    def step(h, xs):
        st, cd = xs
        return h * cd[..., None, None] + st, h

    h_fin, h_in = lax.scan(step, h0.astype(F32), (jnp.moveaxis(states, 1, 0), jnp.moveaxis(chunk_decay, 1, 0)))
    h_in = jnp.moveaxis(h_in, 0, 1)
    y_off = jnp.einsum('bnlhk,bnhpk->bnlhp', cm * jnp.exp(acum)[..., None], h_in)
    y = (y_diag + y_off).reshape(bsz, n * C, H, P)[:, :L]
    return y, h_fin


def dilated_band_attention(q, k, v, slopes, window, dil):
    bsz, L, H, hd = q.shape
    band = window // dil
    unit = band * dil
    nb = -(-L // unit)
    Lp = nb * unit
    pad = Lp - L

    def blocks(t):
        t = jnp.pad(t.astype(F32), ((0, 0), (0, pad), (0, 0), (0, 0)))
        return t.reshape(bsz, nb, band, dil, H, hd)

    def with_prev(t):
        prev = jnp.pad(t, ((0, 0), (1, 0), (0, 0), (0, 0), (0, 0), (0, 0)))[:, :-1]
        return jnp.concatenate([prev, t], axis=2)

    qb = blocks(q)
    kk, vv = with_prev(blocks(k)), with_prev(blocks(v))
    s = jnp.einsum('bnirhd,bnjrhd->bnrhij', qb, kk)
    i = jnp.arange(band)[:, None]
    j = jnp.arange(2 * band)[None, :]
    du = i + band - j
    blk = jnp.arange(nb)[:, None, None]
    valid = (du >= 0) & (du <= band) & ((blk > 0) | (j >= band))
    bias = -(slopes.astype(F32) * dil)[:, None, None] * du
    s = jnp.where(valid[:, None, None], s + bias, -jnp.inf)
    lse = jax.nn.logsumexp(s, axis=-1)
    o = jnp.einsum('bnrhij,bnjrhd->bnirhd', jnp.exp(s - lse[..., None]), vv)
    o = o.reshape(bsz, Lp, H, hd)[:, :L]
    lse = jnp.moveaxis(lse, -1, 2).reshape(bsz, Lp, H)[:, :L]
    return o, lse


def dilated_gather_attention(q, rows_new, kv_buf, slopes, window, dil):
    bsz, Lq, H, hd = q.shape
    Lw = kv_buf.shape[1]
    n_keys = window // dil + 1
    i = jnp.arange(Lq)[:, None]
    j = jnp.arange(n_keys)[None, :]
    idx = Lw + i - j * dil
    valid = idx >= 0
    from_buf = (idx < Lw)[:, :, None, None, None]
    rows = jnp.where(from_buf,
                     kv_buf[:, jnp.clip(idx, 0, Lw - 1)].astype(F32),
                     rows_new[:, jnp.clip(idx - Lw, 0, Lq - 1)].astype(F32))
    kg, vg = rows[:, :, :, 0], rows[:, :, :, 1]
    s = jnp.einsum('bqhd,bqjhd->bhqj', q.astype(F32), kg) - (slopes.astype(F32) * dil)[:, None, None] * j
    s = jnp.where(valid, s, -jnp.inf)
    lse = jax.nn.logsumexp(s, axis=-1)
    o = jnp.einsum('bhqj,bqjhd->bqhd', jnp.exp(s - lse[..., None]), vg)
    return o, jnp.moveaxis(lse, 1, 2)


def token_mixing(u, p, l, s_gdn, c_gdn, s_ssd, c_ssd, kv_bufs):
    bsz, L, _ = u.shape
    offs = np.cumsum(np.array(IN_SPLITS))[:-1].tolist()
    (a_qkv, a_gate, a_beta, a_alpha, b_z, b_xbc, b_dt, c_qkv, gate_logits) = jnp.split(u @ p['w_in'][l], offs, axis=-1)

    qkv, c_gdn_new = causal_dwconv(a_qkv, c_gdn, p['gdn_conv_w'][l])
    qkv = jax.nn.silu(qkv)
    q, k, v = jnp.split(qkv, [GDN_QK, 2 * GDN_QK], axis=-1)
    q = l2norm(q.reshape(bsz, L, GDN_HEADS, GDN_DK)) * GDN_DK ** -0.5
    k = l2norm(k.reshape(bsz, L, GDN_HEADS, GDN_DK))
    v = v.reshape(bsz, L, GDN_HEADS, GDN_DV)
    beta = jax.nn.sigmoid(a_beta.astype(F32))
    g = -jnp.exp(p['gdn_a_log'][l].astype(F32)) * jax.nn.softplus((a_alpha + p['gdn_dt_bias'][l]).astype(F32))
    o_a, s_gdn_new = gated_delta_chunked(q, k, v, g, beta, s_gdn)
    o_a = rmsnorm(o_a, p['gdn_norm_w'][l]) * jax.nn.silu(a_gate.reshape(bsz, L, GDN_HEADS, GDN_DV).astype(F32))
    o_a = o_a.reshape(bsz, L, GDN_VW).astype(u.dtype)

    xbc, c_ssd_new = causal_dwconv(b_xbc, c_ssd, p['ssd_conv_w'][l], p['ssd_conv_b'][l])
    xbc = jax.nn.silu(xbc)
    xs, bm, cm = jnp.split(xbc, [SSD_INNER, SSD_INNER + SSD_GROUPS * SSD_STATE], axis=-1)
    xs = xs.reshape(bsz, L, SSD_HEADS, SSD_HEADDIM)
    rep = SSD_HEADS // SSD_GROUPS
    bm = jnp.repeat(bm.reshape(bsz, L, SSD_GROUPS, SSD_STATE), rep, axis=2)
    cm = jnp.repeat(cm.reshape(bsz, L, SSD_GROUPS, SSD_STATE), rep, axis=2)
    dt = jax.nn.softplus((b_dt + p['ssd_dt_bias'][l]).astype(F32))
    a = -jnp.exp(p['ssd_a_log'][l].astype(F32))
    y, s_ssd_new = ssd_chunked(xs, dt, a, bm, cm, s_ssd)
    y = y + p['ssd_d'][l].astype(F32)[:, None] * xs.astype(F32)
    y = y.reshape(bsz, L, SSD_INNER) * jax.nn.silu(b_z.astype(F32))
    y = rmsnorm(y.reshape(bsz, L, SSD_GROUPS, SSD_INNER // SSD_GROUPS),
                p['ssd_norm_w'][l].reshape(SSD_GROUPS, SSD_INNER // SSD_GROUPS))
    o_b = y.reshape(bsz, L, SSD_INNER).astype(u.dtype)

    cqkv = c_qkv.reshape(bsz, L, 3, N_ATT_GROUPS, ATT_HEADS_PER_GROUP, ATT_HEAD_DIM)
    cq = rmsnorm(cqkv[:, :, 0], p['q_norm'][l]) * ATT_HEAD_DIM ** -0.5
    ck = rmsnorm(cqkv[:, :, 1], p['k_norm'][l])
    cv = cqkv[:, :, 2]
    slopes = alibi_slopes()
    outs, lses, kv_new = [], [], []
    for gi, (win, dil) in enumerate(ATT_GROUPS):
        qg, kg, vg = cq[:, :, gi], ck[:, :, gi], cv[:, :, gi]
        rows = jnp.stack([kg, vg], axis=2)
        if kv_bufs is None:
            o, lse = dilated_band_attention(qg, kg, vg, slopes[gi], win, dil)
            kv_new.append(rows[:, -min(win, L):])
        else:
            o, lse = dilated_gather_attention(qg, rows, kv_bufs[gi], slopes[gi], win, dil)
            kv_new.append(rows)
        outs.append(o)
        lses.append(lse)
    wts = jax.nn.softmax(jnp.stack(lses, axis=2), axis=2)
    o_c = jnp.sum(wts[..., None] * jnp.stack(outs, axis=2), axis=2).reshape(bsz, L, ATT_OUT).astype(u.dtype)

    gates = jax.nn.sigmoid(gate_logits + p['gate_b'][l]).reshape(bsz, L, N_BRANCH, D_MODEL)
    merged = (gates[:, :, 0] * (o_a @ p['w_br_a'][l])
              + gates[:, :, 1] * (o_b @ p['w_br_b'][l])
              + gates[:, :, 2] * (o_c @ p['w_br_c'][l]))
    out = merged @ p['w_out'][l]
    return out, (s_gdn_new.astype(s_gdn.dtype), c_gdn_new, s_ssd_new.astype(s_ssd.dtype), c_ssd_new, *kv_new)


def _ffn_kernel(h_ref, nw_ref, wi_ref, wo_ref, o_ref):
    h = h_ref[...]
    xn = h * lax.rsqrt(jnp.mean(h * h, axis=-1, keepdims=True) + RMS_EPS) * nw_ref[...]
    gu = jnp.dot(xn.astype(BF16), wi_ref[...], preferred_element_type=F32)
    g, up = gu[:, :D_FF], gu[:, D_FF:]
    act = (g * jax.nn.sigmoid(g) * up).astype(BF16)
    y = jnp.dot(act, wo_ref[...], preferred_element_type=F32)
    o_ref[...] = h + 0.5 * y


def ffn_block(h, norm_w, w_in_bf16, w_out_bf16, tm):
    t = h.shape[0]
    assert t % tm == 0
    const = dict(pipeline_mode=pl.Buffered(1))
    return pl.pallas_call(
        _ffn_kernel,
        out_shape=jax.ShapeDtypeStruct(h.shape, h.dtype),
        grid=(t // tm,),
        in_specs=[
            pl.BlockSpec((tm, D_MODEL), lambda i: (i, 0)),
            pl.BlockSpec((1, D_MODEL), lambda i: (0, 0), **const),
            pl.BlockSpec((D_MODEL, 2 * D_FF), lambda i: (0, 0), **const),
            pl.BlockSpec((D_FF, D_MODEL), lambda i: (0, 0), **const),
        ],
        out_specs=pl.BlockSpec((tm, D_MODEL), lambda i: (i, 0)),
        compiler_params=pltpu.CompilerParams(
            dimension_semantics=("arbitrary",), vmem_limit_bytes=VMEM_LIMIT_BYTES),
        name="ffn",
    )(h, norm_w.reshape(1, D_MODEL), w_in_bf16, w_out_bf16)


def run_trunk(x, p, s_gdn, c_gdn, s_ssd, c_ssd, kv_bufs, tm):
    bsz, L, _ = x.shape
    h = x
    new = [[] for _ in range(4 + N_ATT_GROUPS)]
    for l in range(DEPTH):
        h = ffn_block(h.reshape(bsz * L, D_MODEL), p['ffn1_norm'][l], p['ffn1_w_in_bf16'][l],
                      p['ffn1_w_out_bf16'][l], tm).reshape(bsz, L, D_MODEL)
        bufs = None if kv_bufs is None else [b[l] for b in kv_bufs]
        mix, states = token_mixing(rmsnorm(h, p['mix_norm'][l]), p, l, s_gdn[l], c_gdn[l], s_ssd[l], c_ssd[l], bufs)
        h = h + mix
        h = ffn_block(h.reshape(bsz * L, D_MODEL), p['ffn2_norm'][l], p['ffn2_w_in_bf16'][l],
                      p['ffn2_w_out_bf16'][l], tm).reshape(bsz, L, D_MODEL)
        for lst, s in zip(new, states):
            lst.append(s)
    return h, [jnp.stack(lst, axis=0) for lst in new]


def kernel(x_prompt, x_sample, state_gdn, state_gdn_conv, state_ssd, state_ssd_conv,
           cache_kv_w128, cache_kv_w512, cache_kv_w2048,
           ffn1_norm, ffn1_w_in, ffn1_w_out, mix_norm, w_in, gate_b,
           gdn_conv_w, gdn_a_log, gdn_dt_bias, gdn_norm_w,
           ssd_conv_w, ssd_conv_b, ssd_a_log, ssd_dt_bias, ssd_d, ssd_norm_w,
           q_norm, k_norm, w_br_a, w_br_b, w_br_c, w_out,
           ffn2_norm, ffn2_w_in, ffn2_w_out):
    p = {'ffn1_norm': ffn1_norm, 'ffn1_w_in_bf16': ffn1_w_in.astype(BF16),
         'ffn1_w_out_bf16': ffn1_w_out.astype(BF16),
         'mix_norm': mix_norm, 'w_in': w_in, 'gate_b': gate_b,
         'gdn_conv_w': gdn_conv_w, 'gdn_a_log': gdn_a_log, 'gdn_dt_bias': gdn_dt_bias, 'gdn_norm_w': gdn_norm_w,
         'ssd_conv_w': ssd_conv_w, 'ssd_conv_b': ssd_conv_b, 'ssd_a_log': ssd_a_log,
         'ssd_dt_bias': ssd_dt_bias, 'ssd_d': ssd_d, 'ssd_norm_w': ssd_norm_w,
         'q_norm': q_norm, 'k_norm': k_norm, 'w_br_a': w_br_a, 'w_br_b': w_br_b, 'w_br_c': w_br_c,
         'w_out': w_out, 'ffn2_norm': ffn2_norm, 'ffn2_w_in_bf16': ffn2_w_in.astype(BF16),
         'ffn2_w_out_bf16': ffn2_w_out.astype(BF16)}
    bp, dtype = x_prompt.shape[0], x_prompt.dtype
    y_prompt, st_p = run_trunk(
        x_prompt, p,
        jnp.zeros((DEPTH, bp, GDN_HEADS, GDN_DK, GDN_DV), dtype),
        jnp.zeros((DEPTH, bp, CONV_WIDTH - 1, GDN_CONV_DIM), dtype),
        jnp.zeros((DEPTH, bp, SSD_HEADS, SSD_HEADDIM, SSD_STATE), dtype),
        jnp.zeros((DEPTH, bp, CONV_WIDTH - 1, SSD_CONV_DIM), dtype),
        None, 256)
    y_sample, st_s = run_trunk(x_sample, p, state_gdn, state_gdn_conv, state_ssd, state_ssd_conv,
                               [cache_kv_w128, cache_kv_w512, cache_kv_w2048], 128)
    gdn_p, gdn_conv_p, ssd_p, ssd_conv_p, kv128_p, kv512_p, kv2048_p = st_p
    gdn_s, gdn_conv_s, ssd_s, ssd_conv_s, kv128_s, kv512_s, kv2048_s = st_s
    return (y_prompt, y_sample, gdn_p, gdn_s, gdn_conv_p, gdn_conv_s, ssd_p, ssd_s, ssd_conv_p, ssd_conv_s,
            kv128_p, kv128_s, kv512_p, kv512_s, kv2048_p, kv2048_s)
```

```python
import functools
import math
import jax, jax.numpy as jnp
from jax import lax
import numpy as np
from jax.experimental import pallas as pl
from jax.experimental.pallas import tpu as pltpu

D_MODEL = 1024
BATCH = 8
SEQ = 2048
DEPTH = 4
DEC_BATCH = 128
DEC_SEQ = 1
PAST_LEN = 2048

CONV_WIDTH = 4
RMS_EPS = 1e-6
GDN_HEADS = 4
GDN_DK = 128
GDN_DV = 128
GDN_CHUNK = 64
GDN_QK = GDN_HEADS * GDN_DK
GDN_VW = GDN_HEADS * GDN_DV
GDN_CONV_DIM = 2 * GDN_QK + GDN_VW
SSD_HEADS = 8
SSD_HEADDIM = 64
SSD_GROUPS = 2
SSD_STATE = 128
SSD_CHUNK = 128
SSD_INNER = SSD_HEADS * SSD_HEADDIM
SSD_CONV_DIM = SSD_INNER + 2 * SSD_GROUPS * SSD_STATE
ATT_GROUPS = ((128, 1), (512, 4), (2048, 16))
N_ATT_GROUPS = len(ATT_GROUPS)
ATT_HEADS_PER_GROUP = 4
ATT_HEAD_DIM = 64
ATT_HEADS = N_ATT_GROUPS * ATT_HEADS_PER_GROUP
ATT_OUT = ATT_HEADS_PER_GROUP * ATT_HEAD_DIM
D_FF = ((8 * D_MODEL // 3 + 127) // 128) * 128
N_BRANCH = 3
IN_SPLITS = (GDN_CONV_DIM, GDN_VW, GDN_HEADS, GDN_HEADS, SSD_INNER, SSD_CONV_DIM, SSD_HEADS,
             3 * ATT_HEADS * ATT_HEAD_DIM, N_BRANCH * D_MODEL)
D_IN = sum(IN_SPLITS)

F32 = jnp.float32
BF16 = jnp.bfloat16
VMEM_LIMIT_BYTES = 56 * 1024 * 1024


def rmsnorm(x, w):
    xf = x.astype(F32)
    y = xf * lax.rsqrt(jnp.mean(xf * xf, axis=-1, keepdims=True) + RMS_EPS)
    return (y * w.astype(F32)).astype(x.dtype)


def l2norm(x):
    xf = x.astype(F32)
    return xf * lax.rsqrt(jnp.sum(xf * xf, axis=-1, keepdims=True) + RMS_EPS)


def causal_dwconv(x, buf, w, b=None):
    K = w.shape[0]
    L = x.shape[1]
    xp = jnp.concatenate([buf.astype(x.dtype), x], axis=1)
    y = xp[:, 0:L] * w[0]
    for i in range(1, K):
        y = y + xp[:, i:i + L] * w[i]
    if b is not None:
        y = y + b
    return y, xp[:, -(K - 1):]


def alibi_slopes():
    idx = np.arange(1, ATT_HEADS + 1, dtype=np.float32)
    slopes = np.exp2(-8.0 * idx / ATT_HEADS).astype(np.float32)
    return jnp.asarray(slopes).reshape(N_ATT_GROUPS, ATT_HEADS_PER_GROUP)


def gated_delta_chunked(q, k, v, g, beta, s0):
    bsz, L, H, dk = q.shape
    dv = v.shape[-1]
    C = min(GDN_CHUNK, L)
    n = -(-L // C)
    pad = n * C - L

    def prep(t):
        t = t.astype(F32)
        t = jnp.pad(t, [(0, 0), (0, pad)] + [(0, 0)] * (t.ndim - 2))
        t = t.reshape((bsz, n, C) + t.shape[2:])
        return jnp.moveaxis(t, 3, 2)

    q, k, v, g, beta = prep(q), prep(k), prep(v), prep(g), prep(beta)
    gam = jnp.cumsum(g, axis=-1)
    incl = jnp.tril(jnp.ones((C, C), bool))
    strict = jnp.tril(jnp.ones((C, C), bool), -1)
    dec = jnp.exp(jnp.where(incl, gam[..., :, None] - gam[..., None, :], -jnp.inf))
    kk = jnp.einsum('bnhcd,bnhsd->bnhcs', k, k)
    a_mat = jnp.where(strict, beta[..., :, None] * kk * dec, 0.0) + jnp.eye(C, dtype=F32)
    rhs = jnp.concatenate([v * beta[..., None], k * (beta * jnp.exp(gam))[..., None]], axis=-1)
    sol = lax.linalg.triangular_solve(a_mat, rhs, left_side=True, lower=True, unit_diagonal=True)
    u0, w = sol[..., :dv], sol[..., dv:]
    qk = jnp.einsum('bnhcd,bnhsd->bnhcs', q, k) * dec
    q_dec = q * jnp.exp(gam)[..., None]
    k_dec = k * jnp.exp(gam[..., -1:] - gam)[..., None]
    g_last = jnp.exp(gam[..., -1])

    def step(S, xs):
        u0_c, w_c, q_c, qk_c, k_c, gl = xs
        u = u0_c - jnp.einsum('bhcd,bhde->bhce', w_c, S)
        o = jnp.einsum('bhcd,bhde->bhce', q_c, S) + jnp.einsum('bhcs,bhse->bhce', qk_c, u)
        S = S * gl[..., None, None] + jnp.einsum('bhcd,bhce->bhde', k_c, u)
        return S, o

    xs = tuple(jnp.moveaxis(t, 1, 0) for t in (u0, w, q_dec, qk, k_dec, g_last))
    S, o = lax.scan(step, s0.astype(F32), xs)
    o = jnp.moveaxis(jnp.moveaxis(o, 0, 1), 2, 3).reshape(bsz, n * C, H, dv)[:, :L]
    return o, S


def ssd_chunked(x, dt, a, bm, cm, h0):
    bsz, L, H, P = x.shape
    C = min(SSD_CHUNK, L)
    n = -(-L // C)
    pad = n * C - L

    def prep(t):
        t = t.astype(F32)
        t = jnp.pad(t, [(0, 0), (0, pad)] + [(0, 0)] * (t.ndim - 2))
        return t.reshape((bsz, n, C) + t.shape[2:])

    xd = prep(x.astype(F32) * dt[..., None])
    da = prep(dt * a)
    bm, cm = prep(bm), prep(cm)
    acum = jnp.cumsum(da, axis=2)
    incl = jnp.tril(jnp.ones((C, C), bool))
    seg = acum[:, :, :, None, :] - acum[:, :, None, :, :]
    lmat = jnp.exp(jnp.where(incl[:, :, None], seg, -jnp.inf))
    scores = jnp.einsum('bnlhk,bnshk->bnlsh', cm, bm) * lmat
    y_diag = jnp.einsum('bnlsh,bnshp->bnlhp', scores, xd)
    decay_states = jnp.exp(acum[:, :, -1:, :] - acum)
    states = jnp.einsum('bnshk,bnshp->bnhpk', bm * decay_states[..., None], xd)
    chunk_decay = jnp.exp(acum[:, :, -1, :])

    def step(h, xs):
        st, cd = xs
        return h * cd[..., None, None] + st, h

    h_fin, h_in = lax.scan(step, h0.astype(F32), (jnp.moveaxis(states, 1, 0), jnp.moveaxis(chunk_decay, 1, 0)))
    h_in = jnp.moveaxis(h_in, 0, 1)
    y_off = jnp.einsum('bnlhk,bnhpk->bnlhp', cm * jnp.exp(acum)[..., None], h_in)
    y = (y_diag + y_off).reshape(bsz, n * C, H, P)[:, :L]
    return y, h_fin


def dilated_band_attention(q, k, v, slopes, window, dil):
    bsz, L, H, hd = q.shape
    band = window // dil
    unit = band * dil
    nb = -(-L // unit)
    Lp = nb * unit
    pad = Lp - L

    def blocks(t):
        t = jnp.pad(t.astype(F32), ((0, 0), (0, pad), (0, 0), (0, 0)))
        return t.reshape(bsz, nb, band, dil, H, hd)

    def with_prev(t):
        prev = jnp.pad(t, ((0, 0), (1, 0), (0, 0), (0, 0), (0, 0), (0, 0)))[:, :-1]
        return jnp.concatenate([prev, t], axis=2)

    qb = blocks(q)
    kk, vv = with_prev(blocks(k)), with_prev(blocks(v))
    s = jnp.einsum('bnirhd,bnjrhd->bnrhij', qb, kk)
    i = jnp.arange(band)[:, None]
    j = jnp.arange(2 * band)[None, :]
    du = i + band - j
    blk = jnp.arange(nb)[:, None, None]
    valid = (du >= 0) & (du <= band) & ((blk > 0) | (j >= band))
    bias = -(slopes.astype(F32) * dil)[:, None, None] * du
    s = jnp.where(valid[:, None, None], s + bias, -jnp.inf)
    lse = jax.nn.logsumexp(s, axis=-1)
    o = jnp.einsum('bnrhij,bnjrhd->bnirhd', jnp.exp(s - lse[..., None]), vv)
    o = o.reshape(bsz, Lp, H, hd)[:, :L]
    lse = jnp.moveaxis(lse, -1, 2).reshape(bsz, Lp, H)[:, :L]
    return o, lse


def dilated_gather_attention(q, rows_new, kv_buf, slopes, window, dil):
    bsz, Lq, H, hd = q.shape
    Lw = kv_buf.shape[1]
    n_keys = window // dil + 1
    i = jnp.arange(Lq)[:, None]
    j = jnp.arange(n_keys)[None, :]
    idx = Lw + i - j * dil
    valid = idx >= 0
    from_buf = (idx < Lw)[:, :, None, None, None]
    rows = jnp.where(from_buf,
                     kv_buf[:, jnp.clip(idx, 0, Lw - 1)].astype(F32),
                     rows_new[:, jnp.clip(idx - Lw, 0, Lq - 1)].astype(F32))
    kg, vg = rows[:, :, :, 0], rows[:, :, :, 1]
    s = jnp.einsum('bqhd,bqjhd->bhqj', q.astype(F32), kg) - (slopes.astype(F32) * dil)[:, None, None] * j
    s = jnp.where(valid, s, -jnp.inf)
    lse = jax.nn.logsumexp(s, axis=-1)
    o = jnp.einsum('bhqj,bqjhd->bqhd', jnp.exp(s - lse[..., None]), vg)
    return o, jnp.moveaxis(lse, 1, 2)


def token_mixing(u, p, l, s_gdn, c_gdn, s_ssd, c_ssd, kv_bufs):
    bsz, L, _ = u.shape
    offs = np.cumsum(np.array(IN_SPLITS))[:-1].tolist()
    (a_qkv, a_gate, a_beta, a_alpha, b_z, b_xbc, b_dt, c_qkv, gate_logits) = jnp.split(u @ p['w_in'][l], offs, axis=-1)

    qkv, c_gdn_new = causal_dwconv(a_qkv, c_gdn, p['gdn_conv_w'][l])
    qkv = jax.nn.silu(qkv)
    q, k, v = jnp.split(qkv, [GDN_QK, 2 * GDN_QK], axis=-1)
    q = l2norm(q.reshape(bsz, L, GDN_HEADS, GDN_DK)) * GDN_DK ** -0.5
    k = l2norm(k.reshape(bsz, L, GDN_HEADS, GDN_DK))
    v = v.reshape(bsz, L, GDN_HEADS, GDN_DV)
    beta = jax.nn.sigmoid(a_beta.astype(F32))
    g = -jnp.exp(p['gdn_a_log'][l].astype(F32)) * jax.nn.softplus((a_alpha + p['gdn_dt_bias'][l]).astype(F32))
    o_a, s_gdn_new = gated_delta_chunked(q, k, v, g, beta, s_gdn)
    o_a = rmsnorm(o_a, p['gdn_norm_w'][l]) * jax.nn.silu(a_gate.reshape(bsz, L, GDN_HEADS, GDN_DV).astype(F32))
    o_a = o_a.reshape(bsz, L, GDN_VW).astype(u.dtype)

    xbc, c_ssd_new = causal_dwconv(b_xbc, c_ssd, p['ssd_conv_w'][l], p['ssd_conv_b'][l])
    xbc = jax.nn.silu(xbc)
    xs, bm, cm = jnp.split(xbc, [SSD_INNER, SSD_INNER + SSD_GROUPS * SSD_STATE], axis=-1)
    xs = xs.reshape(bsz, L, SSD_HEADS, SSD_HEADDIM)
    rep = SSD_HEADS // SSD_GROUPS
    bm = jnp.repeat(bm.reshape(bsz, L, SSD_GROUPS, SSD_STATE), rep, axis=2)
    cm = jnp.repeat(cm.reshape(bsz, L, SSD_GROUPS, SSD_STATE), rep, axis=2)
    dt = jax.nn.softplus((b_dt + p['ssd_dt_bias'][l]).astype(F32))
    a = -jnp.exp(p['ssd_a_log'][l].astype(F32))
    y, s_ssd_new = ssd_chunked(xs, dt, a, bm, cm, s_ssd)
    y = y + p['ssd_d'][l].astype(F32)[:, None] * xs.astype(F32)
    y = y.reshape(bsz, L, SSD_INNER) * jax.nn.silu(b_z.astype(F32))
    y = rmsnorm(y.reshape(bsz, L, SSD_GROUPS, SSD_INNER // SSD_GROUPS),
                p['ssd_norm_w'][l].reshape(SSD_GROUPS, SSD_INNER // SSD_GROUPS))
    o_b = y.reshape(bsz, L, SSD_INNER).astype(u.dtype)

    cqkv = c_qkv.reshape(bsz, L, 3, N_ATT_GROUPS, ATT_HEADS_PER_GROUP, ATT_HEAD_DIM)
    cq = rmsnorm(cqkv[:, :, 0], p['q_norm'][l]) * ATT_HEAD_DIM ** -0.5
    ck = rmsnorm(cqkv[:, :, 1], p['k_norm'][l])
    cv = cqkv[:, :, 2]
    slopes = alibi_slopes()
    outs, lses, kv_new = [], [], []
    for gi, (win, dil) in enumerate(ATT_GROUPS):
        qg, kg, vg = cq[:, :, gi], ck[:, :, gi], cv[:, :, gi]
        rows = jnp.stack([kg, vg], axis=2)
        if kv_bufs is None:
            o, lse = dilated_band_attention(qg, kg, vg, slopes[gi], win, dil)
            kv_new.append(rows[:, -min(win, L):])
        else:
            o, lse = dilated_gather_attention(qg, rows, kv_bufs[gi], slopes[gi], win, dil)
            kv_new.append(rows)
        outs.append(o)
        lses.append(lse)
    wts = jax.nn.softmax(jnp.stack(lses, axis=2), axis=2)
    o_c = jnp.sum(wts[..., None] * jnp.stack(outs, axis=2), axis=2).reshape(bsz, L, ATT_OUT).astype(u.dtype)

    gates = jax.nn.sigmoid(gate_logits + p['gate_b'][l]).reshape(bsz, L, N_BRANCH, D_MODEL)
    merged = (gates[:, :, 0] * (o_a @ p['w_br_a'][l])
              + gates[:, :, 1] * (o_b @ p['w_br_b'][l])
              + gates[:, :, 2] * (o_c @ p['w_br_c'][l]))
    out = merged @ p['w_out'][l]
    return out, (s_gdn_new.astype(s_gdn.dtype), c_gdn_new, s_ssd_new.astype(s_ssd.dtype), c_ssd_new, *kv_new)


def _ffn_kernel(h_ref, nw_ref, wi_ref, wo_ref, o_ref):
    h = h_ref[...]
    xn = h * lax.rsqrt(jnp.mean(h * h, axis=-1, keepdims=True) + RMS_EPS) * nw_ref[...]
    gu = jnp.dot(xn.astype(BF16), wi_ref[...], preferred_element_type=F32)
    g, up = gu[:, :D_FF], gu[:, D_FF:]
    act = (g * jax.nn.sigmoid(g) * up).astype(BF16)
    y = jnp.dot(act, wo_ref[...], preferred_element_type=F32)
    o_ref[...] = h + 0.5 * y


def ffn_block(h, norm_w, w_in_bf16, w_out_bf16, tm):
    t = h.shape[0]
    assert t % tm == 0
    const = dict(pipeline_mode=pl.Buffered(1))
    return pl.pallas_call(
        _ffn_kernel,
        out_shape=jax.ShapeDtypeStruct(h.shape, h.dtype),
        grid=(t // tm,),
        in_specs=[
            pl.BlockSpec((tm, D_MODEL), lambda i: (i, 0)),
            pl.BlockSpec((1, D_MODEL), lambda i: (0, 0), **const),
            pl.BlockSpec((D_MODEL, 2 * D_FF), lambda i: (0, 0), **const),
            pl.BlockSpec((D_FF, D_MODEL), lambda i: (0, 0), **const),
        ],
        out_specs=pl.BlockSpec((tm, D_MODEL), lambda i: (i, 0)),
        compiler_params=pltpu.CompilerParams(
            dimension_semantics=("arbitrary",), vmem_limit_bytes=VMEM_LIMIT_BYTES),
        name="ffn",
    )(h, norm_w.reshape(1, D_MODEL), w_in_bf16, w_out_bf16)


_DN_BMM = (((2,), (1,)), ((0,), (0,)))
_DN_BMM_NT = (((2,), (2,)), ((0,), (0,)))


def _bmm(a, b):
    return lax.dot_general(a.astype(BF16), b.astype(BF16), _DN_BMM, preferred_element_type=F32)


def _bmm_nt(a, b):
    return lax.dot_general(a.astype(BF16), b.astype(BF16), _DN_BMM_NT, preferred_element_type=F32)


def _split2(x):
    hi = x.astype(BF16)
    return hi, (x - hi.astype(F32)).astype(BF16)


def _split3(x):
    x1 = x.astype(BF16)
    r1 = x - x1.astype(F32)
    x2 = r1.astype(BF16)
    x3 = (r1 - x2.astype(F32)).astype(BF16)
    return x1, x2, x3


def _bmm_hi(a, b):
    a1, a2 = _split2(a)
    b1, b2 = _split2(b)
    dot = functools.partial(lax.dot_general, dimension_numbers=_DN_BMM, preferred_element_type=F32)
    return dot(a1, b1) + (dot(a1, b2) + dot(a2, b1))


def _cumsum_rows(tri, x):
    dot = functools.partial(lax.dot_general, dimension_numbers=_DN_BMM, preferred_element_type=F32)
    x1, x2, x3 = _split3(x)
    return dot(tri, x1) + (dot(tri, x2) + dot(tri, x3))


def _softplus(x):
    return jnp.maximum(x, 0.0) + jnp.log(1.0 + jnp.exp(-jnp.abs(x)))


def _silu(x):
    return x * jax.nn.sigmoid(x)


GDN_INV_BLOCK = 16


def _unit_lower_solve(a, rhs, row, col, mm):
    c = a.shape[-1]
    eye = (row == col).astype(F32)
    same_blk = (row // GDN_INV_BLOCK) == (col // GDN_INV_BLOCK)
    d = jnp.where(same_blk, a, 0.0)
    n = a - d
    x = eye - d
    pw = d
    k = 2
    while k < GDN_INV_BLOCK:
        pw = mm(pw, pw)
        x = x + mm(x, pw)
        k *= 2
    m = mm(x, n)
    y = mm(x, rhs)
    nblk = c // GDN_INV_BLOCK
    q = eye - m
    pw = m
    k = 2
    while k < nblk:
        pw = mm(pw, pw)
        q = q + mm(q, pw)
        k *= 2
    return mm(q, y)


def _gdn_prompt_kernel(qkv_ref, gate_ref, small_ref, convw_ref, alog_ref, dtb_ref, normw_ref,
                       o_ref, s_ref, cst_ref, xp_ref):
    c = GDN_CHUNK
    nb = qkv_ref.shape[0]

    @pl.when(pl.program_id(0) == 0)
    def _():
        s_ref[...] = jnp.zeros_like(s_ref)
        xp_ref[:, 0:8, :] = jnp.zeros((nb, 8, GDN_CONV_DIM), F32)

    x = qkv_ref[...]
    xp_ref[:, 8:8 + c, :] = x
    w = convw_ref[...]
    y = (xp_ref[:, 5:5 + c, :] * w[0:1, :] + xp_ref[:, 6:6 + c, :] * w[1:2, :]
         + xp_ref[:, 7:7 + c, :] * w[2:3, :] + x * w[3:4, :])
    xp_ref[:, 0:8, :] = xp_ref[:, c:c + 8, :]
    cst_ref[...] = x[:, c - (CONV_WIDTH - 1):, :]
    y = _silu(y)

    small = small_ref[...]
    beta_all = jax.nn.sigmoid(small)
    g_all = -jnp.exp(alog_ref[...]) * _softplus(small + dtb_ref[...])
    row = lax.broadcasted_iota(jnp.int32, (c, c), 0)
    col = lax.broadcasted_iota(jnp.int32, (c, c), 1)
    incl = row >= col
    strict = row > col
    tri = jnp.broadcast_to(incl.astype(BF16), (nb, c, c))
    gam_all = _cumsum_rows(tri, g_all)
    gam_all_t = jnp.swapaxes(gam_all, 1, 2)
    egam_all = jnp.exp(gam_all)

    for h in range(GDN_HEADS):
        lo = h * GDN_DK
        q = y[:, :, lo:lo + GDN_DK]
        k = y[:, :, GDN_QK + lo:GDN_QK + lo + GDN_DK]
        v = y[:, :, 2 * GDN_QK + h * GDN_DV:2 * GDN_QK + (h + 1) * GDN_DV]
        q = q * (lax.rsqrt(jnp.sum(q * q, axis=-1, keepdims=True) + RMS_EPS) * GDN_DK ** -0.5)
        k = k * lax.rsqrt(jnp.sum(k * k, axis=-1, keepdims=True) + RMS_EPS)
        beta = beta_all[:, :, h:h + 1]
        gam = gam_all[:, :, GDN_HEADS + h:GDN_HEADS + h + 1]
        gam_row = gam_all_t[:, GDN_HEADS + h:GDN_HEADS + h + 1, :]
        egam = egam_all[:, :, GDN_HEADS + h:GDN_HEADS + h + 1]
        gam_last = gam_row[:, :, c - 1:c]
        dec = jnp.where(incl, jnp.exp(jnp.minimum(gam - gam_row, 0.0)), 0.0)
        kk = _bmm_nt(k, k)
        a_mat = jnp.where(strict, beta * kk * dec, 0.0)
        rhs = jnp.concatenate([v * beta, k * (beta * egam)], axis=-1)
        sol = _unit_lower_solve(a_mat, rhs, row, col, _bmm_hi)
        u0, wk = sol[:, :, :GDN_DV], sol[:, :, GDN_DV:]
        qk = _bmm_nt(q, k) * dec
        q_dec = q * egam
        k_dec = k * jnp.exp(gam_last - gam)
        s = s_ref[:, h]
        u = u0 - _bmm(wk, s)
        o = _bmm(q_dec, s) + _bmm(qk, u)
        s_ref[:, h] = s * jnp.exp(gam_last) + _bmm(jnp.swapaxes(k_dec, 1, 2), u)
        o = o * lax.rsqrt(jnp.mean(o * o, axis=-1, keepdims=True) + RMS_EPS) * normw_ref[...]
        o_ref[:, :, h * GDN_DV:(h + 1) * GDN_DV] = o * _silu(gate_ref[:, :, h * GDN_DV:(h + 1) * GDN_DV])


def gdn_prompt(a_qkv, a_gate, small, conv_w, a_log, dt_bias, norm_w):
    bsz, L, _ = a_qkv.shape
    c = GDN_CHUNK
    assert L % c == 0
    lanes = small.shape[-1]
    alog_row = jnp.zeros((1, lanes), F32).at[0, GDN_HEADS:2 * GDN_HEADS].set(a_log)
    dtb_row = jnp.zeros((1, lanes), F32).at[0, GDN_HEADS:2 * GDN_HEADS].set(dt_bias)
    const2 = lambda n: (0, 0)
    return pl.pallas_call(
        _gdn_prompt_kernel,
        out_shape=(jax.ShapeDtypeStruct((bsz, L, GDN_VW), F32),
                   jax.ShapeDtypeStruct((bsz, GDN_HEADS, GDN_DK, GDN_DV), F32),
                   jax.ShapeDtypeStruct((bsz, CONV_WIDTH - 1, GDN_CONV_DIM), F32)),
        grid=(L // c,),
        in_specs=[
            pl.BlockSpec((bsz, c, GDN_CONV_DIM), lambda n: (0, n, 0)),
            pl.BlockSpec((bsz, c, GDN_VW), lambda n: (0, n, 0)),
            pl.BlockSpec((bsz, c, lanes), lambda n: (0, n, 0)),
            pl.BlockSpec((CONV_WIDTH, GDN_CONV_DIM), const2),
            pl.BlockSpec((1, lanes), const2),
            pl.BlockSpec((1, lanes), const2),
            pl.BlockSpec((1, GDN_DV), const2),
        ],
        out_specs=(
            pl.BlockSpec((bsz, c, GDN_VW), lambda n: (0, n, 0)),
            pl.BlockSpec((bsz, GDN_HEADS, GDN_DK, GDN_DV), lambda n: (0, 0, 0, 0)),
            pl.BlockSpec((bsz, CONV_WIDTH - 1, GDN_CONV_DIM), lambda n: (0, 0, 0)),
        ),
        scratch_shapes=[pltpu.VMEM((bsz, c + 8, GDN_CONV_DIM), F32)],
        compiler_params=pltpu.CompilerParams(
            dimension_semantics=("arbitrary",), vmem_limit_bytes=VMEM_LIMIT_BYTES),
        name="gdn_prompt",
    )(a_qkv, a_gate, small, conv_w, alog_row, dtb_row, norm_w.reshape(1, GDN_DV))


SSD_DT_LANE = 2 * GDN_HEADS
SSD_BC = SSD_GROUPS * SSD_STATE
SSD_HPG = SSD_HEADS // SSD_GROUPS


def _ssd_prompt_kernel(xbc_ref, z_ref, small_ref, convw_ref, convb_ref, alog_ref, dtb_ref, d_ref, normw_ref,
                       o_ref, h_ref, cst_ref, xp_ref):
    c = SSD_CHUNK
    nb = xbc_ref.shape[0]

    @pl.when(pl.program_id(0) == 0)
    def _():
        h_ref[...] = jnp.zeros_like(h_ref)
        xp_ref[:, 0:8, :] = jnp.zeros((nb, 8, SSD_CONV_DIM), F32)

    x = xbc_ref[...]
    xp_ref[:, 8:8 + c, :] = x
    w = convw_ref[...]
    y = (xp_ref[:, 5:5 + c, :] * w[0:1, :] + xp_ref[:, 6:6 + c, :] * w[1:2, :]
         + xp_ref[:, 7:7 + c, :] * w[2:3, :] + x * w[3:4, :]) + convb_ref[...]
    xp_ref[:, 0:8, :] = xp_ref[:, c:c + 8, :]
    cst_ref[...] = x[:, c - (CONV_WIDTH - 1):, :]
    y = _silu(y)
    xs = y[:, :, :SSD_INNER]
    bm = y[:, :, SSD_INNER:SSD_INNER + SSD_BC]
    cm = y[:, :, SSD_INNER + SSD_BC:]

    dt_all = _softplus(small_ref[...] + dtb_ref[...])
    da_all = dt_all * (-jnp.exp(alog_ref[...]))
    row = lax.broadcasted_iota(jnp.int32, (c, c), 0)
    col = lax.broadcasted_iota(jnp.int32, (c, c), 1)
    incl = row >= col
    tri = jnp.broadcast_to(incl.astype(BF16), (nb, c, c))
    acum_all = _cumsum_rows(tri, da_all)
    acum_all_t = jnp.swapaxes(acum_all, 1, 2)
    eacum_all = jnp.exp(acum_all)

    ys = []
    for g in range(SSD_GROUPS):
        bg = bm[:, :, g * SSD_STATE:(g + 1) * SSD_STATE]
        cg = cm[:, :, g * SSD_STATE:(g + 1) * SSD_STATE]
        scores = _bmm_nt(cg, bg)
        for hh in range(SSD_HPG):
            h = g * SSD_HPG + hh
            ln = SSD_DT_LANE + h
            dtc = dt_all[:, :, ln:ln + 1]
            ac = acum_all[:, :, ln:ln + 1]
            ar = acum_all_t[:, ln:ln + 1, :]
            alast = ar[:, :, c - 1:c]
            lmat = jnp.where(incl, jnp.exp(jnp.minimum(ac - ar, 0.0)), 0.0)
            xd = xs[:, :, h * SSD_HEADDIM:(h + 1) * SSD_HEADDIM] * dtc
            y_diag = _bmm(scores * lmat, xd)
            hin = h_ref[:, h]
            y_off = _bmm_nt(cg * eacum_all[:, :, ln:ln + 1], hin)
            states = _bmm(jnp.swapaxes(xd, 1, 2), bg * jnp.exp(alast - ac))
            h_ref[:, h] = hin * jnp.exp(alast) + states
            ys.append(y_diag + y_off)
    yf = jnp.concatenate(ys, axis=-1) + d_ref[...] * xs
    yf = yf * _silu(z_ref[...])
    gw = SSD_INNER // SSD_GROUPS
    for g in range(SSD_GROUPS):
        seg = yf[:, :, g * gw:(g + 1) * gw]
        o_ref[:, :, g * gw:(g + 1) * gw] = (seg * lax.rsqrt(jnp.mean(seg * seg, axis=-1, keepdims=True) + RMS_EPS)
                                            * normw_ref[:, g * gw:(g + 1) * gw])


def ssd_prompt(b_xbc, b_z, small, conv_w, conv_b, a_log, dt_bias, d_skip, norm_w):
    bsz, L, _ = b_xbc.shape
    c = SSD_CHUNK
    assert L % c == 0
    lanes = small.shape[-1]
    sl = slice(SSD_DT_LANE, SSD_DT_LANE + SSD_HEADS)
    alog_row = jnp.zeros((1, lanes), F32).at[0, sl].set(a_log)
    dtb_row = jnp.zeros((1, lanes), F32).at[0, sl].set(dt_bias)
    d_row = jnp.repeat(d_skip, SSD_HEADDIM).reshape(1, SSD_INNER)
    const2 = lambda n: (0, 0)
    return pl.pallas_call(
        _ssd_prompt_kernel,
        out_shape=(jax.ShapeDtypeStruct((bsz, L, SSD_INNER), F32),
                   jax.ShapeDtypeStruct((bsz, SSD_HEADS, SSD_HEADDIM, SSD_STATE), F32),
                   jax.ShapeDtypeStruct((bsz, CONV_WIDTH - 1, SSD_CONV_DIM), F32)),
        grid=(L // c,),
        in_specs=[
            pl.BlockSpec((bsz, c, SSD_CONV_DIM), lambda n: (0, n, 0)),
            pl.BlockSpec((bsz, c, SSD_INNER), lambda n: (0, n, 0)),
            pl.BlockSpec((bsz, c, lanes), lambda n: (0, n, 0)),
            pl.BlockSpec((CONV_WIDTH, SSD_CONV_DIM), const2),
            pl.BlockSpec((1, SSD_CONV_DIM), const2),
            pl.BlockSpec((1, lanes), const2),
            pl.BlockSpec((1, lanes), const2),
            pl.BlockSpec((1, SSD_INNER), const2),
            pl.BlockSpec((1, SSD_INNER), const2),
        ],
        out_specs=(
            pl.BlockSpec((bsz, c, SSD_INNER), lambda n: (0, n, 0)),
            pl.BlockSpec((bsz, SSD_HEADS, SSD_HEADDIM, SSD_STATE), lambda n: (0, 0, 0, 0)),
            pl.BlockSpec((bsz, CONV_WIDTH - 1, SSD_CONV_DIM), lambda n: (0, 0, 0)),
        ),
        scratch_shapes=[pltpu.VMEM((bsz, c + 8, SSD_CONV_DIM), F32)],
        compiler_params=pltpu.CompilerParams(
            dimension_semantics=("arbitrary",), vmem_limit_bytes=VMEM_LIMIT_BYTES),
        name="ssd_prompt",
    )(b_xbc, b_z, small, conv_w, conv_b.reshape(1, SSD_CONV_DIM), alog_row, dtb_row, d_row,
      norm_w.reshape(1, SSD_INNER))


ATT_BAND = 128
ATT_MASKED = -1e30
assert all(w // d == ATT_BAND for w, d in ATT_GROUPS)


def _alibi_slopes_np():
    idx = np.arange(1, ATT_HEADS + 1, dtype=np.float32)
    return np.exp2(-8.0 * idx / ATT_HEADS).astype(np.float32).reshape(N_ATT_GROUPS, ATT_HEADS_PER_GROUP)


def _attn_bias(gi, with_prev):
    nk = 2 * ATT_BAND if with_prev else ATT_BAND
    i = lax.broadcasted_iota(jnp.int32, (ATT_BAND, nk), 0)
    j = lax.broadcasted_iota(jnp.int32, (ATT_BAND, nk), 1)
    du = i + (nk - ATT_BAND) - j
    duf = du.astype(F32)
    dil = ATT_GROUPS[gi][1]
    slopes = _alibi_slopes_np()[gi]
    inside = jnp.abs(2 * du - ATT_BAND) <= ATT_BAND
    return [jnp.where(inside, duf * float(-slopes[hh] * dil), ATT_MASKED) for hh in range(ATT_HEADS_PER_GROUP)]


ATT_LANE_CHUNKS = ATT_OUT // 128
ATT_HEADS_PER_CHUNK = 128 // ATT_HEAD_DIM


def _attn_block(q, keys, vals, bias):
    hd = ATT_HEAD_DIM
    outs, lses = [], []
    for hh in range(ATT_HEADS_PER_CHUNK):
        sl = slice(hh * hd, (hh + 1) * hd)
        s = lax.dot_general(q[:, sl].astype(BF16), keys[:, sl].astype(BF16), (((1,), (1,)), ((), ())),
                            preferred_element_type=F32) + bias[hh]
        m = jnp.max(s, axis=-1, keepdims=True)
        p = jnp.exp(s - m)
        l = jnp.sum(p, axis=-1, keepdims=True)
        o = jnp.dot(p.astype(BF16), vals[:, sl].astype(BF16), preferred_element_type=F32) * (1.0 / l)
        outs.append(o)
        lses.append(jnp.broadcast_to(m + jnp.log(l), (ATT_BAND, hd)))
    return jnp.concatenate(outs, axis=-1), jnp.concatenate(lses, axis=-1)


def _attn_prompt_kernel(q_ref, kv_ref, o_ref, lse_ref):
    g = pl.program_id(0)
    L = q_ref.shape[2]
    nc = ATT_LANE_CHUNKS

    for gi, (win, dil) in enumerate(ATT_GROUPS):
        @pl.when(g == gi)
        def _(gi=gi, win=win, dil=dil):
            nblk = L // win

            def rows(start):
                return pl.ds(start, ATT_BAND, stride=dil) if dil > 1 else pl.ds(start, ATT_BAND)

            def unit(cur, prev, bias):
                for c in range(nc):
                    hs = slice(c * ATT_HEADS_PER_CHUNK, (c + 1) * ATT_HEADS_PER_CHUNK)
                    keys, vals = kv_ref[0, c, cur, :], kv_ref[0, nc + c, cur, :]
                    if prev is not None:
                        keys = jnp.concatenate([kv_ref[0, c, prev, :], keys], axis=0)
                        vals = jnp.concatenate([kv_ref[0, nc + c, prev, :], vals], axis=0)
                    o, lse = _attn_block(q_ref[0, c, cur, :], keys, vals, bias[hs])
                    o_ref[0, c, cur, :] = o
                    lse_ref[0, c, cur, :] = lse

            bias0 = _attn_bias(gi, False)
            if nblk > 1:
                for r in range(dil):
                    unit(rows(r), None, bias0)
                bias1 = _attn_bias(gi, True)

                def body(u, carry):
                    start = (u // dil + 1) * win + u % dil
                    unit(rows(start), rows(start - win), bias1)
                    return carry

                lax.fori_loop(0, (nblk - 1) * dil, body, 0)
            else:
                def body0(r, carry):
                    unit(rows(r), None, bias0)
                    return carry

                lax.fori_loop(0, dil, body0, 0)


def attn_prompt(q5, kv5, bsz):
    ng, nc, _, lanes = q5.shape
    L = SEQ
    assert all(L % w == 0 for w, _ in ATT_GROUPS)
    blk = lambda n: pl.BlockSpec((1, n, L, lanes), lambda g, b: (g, 0, b, 0))
    out = jax.ShapeDtypeStruct((ng, nc, bsz * L, lanes), F32)
    return pl.pallas_call(
        _attn_prompt_kernel,
        out_shape=(out, out),
        grid=(ng, bsz),
        in_specs=[blk(nc), blk(2 * nc)],
        out_specs=(blk(nc), blk(nc)),
        compiler_params=pltpu.CompilerParams(
            dimension_semantics=("arbitrary", "arbitrary"), vmem_limit_bytes=VMEM_LIMIT_BYTES),
        name="attn_prompt",
    )(q5, kv5)


SMALL_LANES = 128
_OFF = np.cumsum((0,) + IN_SPLITS).tolist()
PROJ_AB_WIDTHS = (GDN_CONV_DIM, GDN_VW, SSD_INNER, SSD_CONV_DIM, SMALL_LANES)
PROJ_AB_N = sum(PROJ_AB_WIDTHS)
PROJ_C_N = 3 * ATT_HEADS * ATT_HEAD_DIM
ATT_Q_N = ATT_HEADS * ATT_HEAD_DIM


def prep_weights(w_in):
    o = _OFF
    pad = jnp.zeros(w_in.shape[:-1] + (SMALL_LANES - 2 * GDN_HEADS - SSD_HEADS,), w_in.dtype)
    w_ab = jnp.concatenate([w_in[..., o[0]:o[1]], w_in[..., o[1]:o[2]], w_in[..., o[4]:o[5]], w_in[..., o[5]:o[6]],
                            w_in[..., o[2]:o[4]], w_in[..., o[6]:o[7]], pad], axis=-1).astype(BF16)
    w_c = w_in[..., o[7]:o[8]].astype(BF16)
    w_g = w_in[..., o[8]:o[9]].astype(BF16)
    return w_ab, w_c, w_g


def _rmsnorm_rows(h, w_row):
    return h * lax.rsqrt(jnp.mean(h * h, axis=-1, keepdims=True) + RMS_EPS) * w_row


def _proj_ab_kernel(h_ref, nw_ref, w_ref, qkv_ref, gate_ref, z_ref, xbc_ref, small_ref):
    u = _rmsnorm_rows(h_ref[...], nw_ref[...]).astype(BF16)
    y = jnp.dot(u, w_ref[...], preferred_element_type=F32)
    lo = 0
    for ref, wd in zip((qkv_ref, gate_ref, z_ref, xbc_ref, small_ref), PROJ_AB_WIDTHS):
        ref[...] = y[:, lo:lo + wd]
        lo += wd


def proj_ab(h, norm_w, w_ab, tm):
    t = h.shape[0]
    assert t % tm == 0
    const = dict(pipeline_mode=pl.Buffered(1))
    return pl.pallas_call(
        _proj_ab_kernel,
        out_shape=tuple(jax.ShapeDtypeStruct((t, wd), F32) for wd in PROJ_AB_WIDTHS),
        grid=(t // tm,),
        in_specs=[
            pl.BlockSpec((tm, D_MODEL), lambda i: (i, 0)),
            pl.BlockSpec((1, D_MODEL), lambda i: (0, 0), **const),
            pl.BlockSpec((D_MODEL, PROJ_AB_N), lambda i: (0, 0), **const),
        ],
        out_specs=tuple(pl.BlockSpec((tm, wd), lambda i: (i, 0)) for wd in PROJ_AB_WIDTHS),
        compiler_params=pltpu.CompilerParams(
            dimension_semantics=("arbitrary",), vmem_limit_bytes=VMEM_LIMIT_BYTES),
        name="proj_ab",
    )(h, norm_w.reshape(1, D_MODEL), w_ab)


def _head_rmsnorm(x, w_row, seg):
    outs = []
    for c in range(x.shape[-1] // 128):
        xc = x[:, c * 128:(c + 1) * 128]
        hi, lo = _split2(xc * xc)
        ss = jnp.dot(hi, seg, preferred_element_type=F32) + jnp.dot(lo, seg, preferred_element_type=F32)
        outs.append(xc * lax.rsqrt(ss * (1.0 / ATT_HEAD_DIM) + RMS_EPS) * w_row)
    return outs


def _proj_c_kernel(h_ref, nw_ref, w_ref, qw_ref, kw_ref, *out_refs, chunked):
    u = _rmsnorm_rows(h_ref[...], nw_ref[...]).astype(BF16)
    y = jnp.dot(u, w_ref[...], preferred_element_type=F32)
    r = lax.broadcasted_iota(jnp.int32, (128, 128), 0) // ATT_HEAD_DIM
    c = lax.broadcasted_iota(jnp.int32, (128, 128), 1) // ATT_HEAD_DIM
    seg = (r == c).astype(BF16)
    qn = [t * ATT_HEAD_DIM ** -0.5 for t in _head_rmsnorm(y[:, :ATT_Q_N], qw_ref[...], seg)]
    kn = _head_rmsnorm(y[:, ATT_Q_N:2 * ATT_Q_N], kw_ref[...], seg)
    v = y[:, 2 * ATT_Q_N:]
    nc = ATT_LANE_CHUNKS
    if chunked:
        q5_ref, kv5_ref, kvrows_ref = out_refs
    else:
        q_ref, kvrows_ref = out_refs
    for g in range(N_ATT_GROUPS):
        for cc in range(nc):
            i = g * nc + cc
            vc = v[:, i * 128:(i + 1) * 128]
            kvrows_ref[g, :, cc * 128:(cc + 1) * 128] = kn[i]
            kvrows_ref[g, :, ATT_OUT + cc * 128:ATT_OUT + (cc + 1) * 128] = vc
            if chunked:
                q5_ref[g, cc] = qn[i]
                kv5_ref[g, cc] = kn[i]
                kv5_ref[g, nc + cc] = vc
            else:
                q_ref[:, i * 128:(i + 1) * 128] = qn[i]


def proj_c(h, norm_w, w_c, q_norm, k_norm, tm, chunked):
    t = h.shape[0]
    assert t % tm == 0
    const = dict(pipeline_mode=pl.Buffered(1))
    ng, nc = N_ATT_GROUPS, ATT_LANE_CHUNKS
    rows_shape = jax.ShapeDtypeStruct((ng, t, 2 * ATT_OUT), F32)
    rows_spec = pl.BlockSpec((ng, tm, 2 * ATT_OUT), lambda i: (0, i, 0))
    if chunked:
        out_shape = (jax.ShapeDtypeStruct((ng, nc, t, 128), F32), jax.ShapeDtypeStruct((ng, 2 * nc, t, 128), F32),
                     rows_shape)
        out_specs = (pl.BlockSpec((ng, nc, tm, 128), lambda i: (0, 0, i, 0)),
                     pl.BlockSpec((ng, 2 * nc, tm, 128), lambda i: (0, 0, i, 0)), rows_spec)
    else:
        out_shape = (jax.ShapeDtypeStruct((t, ATT_Q_N), F32), rows_shape)
        out_specs = (pl.BlockSpec((tm, ATT_Q_N), lambda i: (i, 0)), rows_spec)
    tile2 = lambda w: jnp.tile(w, 128 // ATT_HEAD_DIM).reshape(1, 128)
    return pl.pallas_call(
        functools.partial(_proj_c_kernel, chunked=chunked),
        out_shape=out_shape,
        grid=(t // tm,),
        in_specs=[
            pl.BlockSpec((tm, D_MODEL), lambda i: (i, 0)),
            pl.BlockSpec((1, D_MODEL), lambda i: (0, 0), **const),
            pl.BlockSpec((D_MODEL, PROJ_C_N), lambda i: (0, 0), **const),
            pl.BlockSpec((1, 128), lambda i: (0, 0), **const),
            pl.BlockSpec((1, 128), lambda i: (0, 0), **const),
        ],
        out_specs=out_specs,
        compiler_params=pltpu.CompilerParams(
            dimension_semantics=("arbitrary",), vmem_limit_bytes=VMEM_LIMIT_BYTES),
        name="proj_c",
    )(h, norm_w.reshape(1, D_MODEL), w_c, tile2(q_norm), tile2(k_norm))


def _merge_kernel(h_ref, oa_ref, ob_ref, *refs, chunked):
    if chunked:
        o5_ref, lse5_ref = refs[:2]
        refs = refs[2:]
        ocs = []
        for cc in range(ATT_LANE_CHUNKS):
            ls = [lse5_ref[g, cc] for g in range(N_ATT_GROUPS)]
            m = functools.reduce(jnp.maximum, ls)
            es = [jnp.exp(l - m) for l in ls]
            num = sum(e * o5_ref[g, cc] for g, e in enumerate(es))
            ocs.append(num * (1.0 / sum(es)))
        o_c = jnp.concatenate(ocs, axis=-1)
    else:
        o_c = refs[0][...]
        refs = refs[1:]
    nw_ref, wg_ref, gb_ref, wa_ref, wb_ref, wc_ref, wo_ref, out_ref = refs
    h = h_ref[...]
    u = _rmsnorm_rows(h, nw_ref[...]).astype(BF16)
    merged = None
    for i, (o, w_ref) in enumerate(((oa_ref[...], wa_ref), (ob_ref[...], wb_ref), (o_c, wc_ref))):
        sl = slice(i * D_MODEL, (i + 1) * D_MODEL)
        gate = jax.nn.sigmoid(jnp.dot(u, wg_ref[:, sl], preferred_element_type=F32) + gb_ref[:, sl])
        term = gate * jnp.dot(o.astype(BF16), w_ref[...], preferred_element_type=F32)
        merged = term if merged is None else merged + term
    out_ref[...] = h + jnp.dot(merged.astype(BF16), wo_ref[...], preferred_element_type=F32)


def merge_block(h, o_a, o_b, o_att, norm_w, w_g, gate_b, w_br_a, w_br_b, w_br_c, w_out, tm):
    t = h.shape[0]
    assert t % tm == 0
    chunked = isinstance(o_att, tuple)
    const = dict(pipeline_mode=pl.Buffered(1))
    full = lambda a: pl.BlockSpec(a.shape, lambda i: (0,) * a.ndim, **const)
    rows = lambda wd: pl.BlockSpec((tm, wd), lambda i: (i, 0))
    if chunked:
        att_specs = [pl.BlockSpec((N_ATT_GROUPS, ATT_LANE_CHUNKS, tm, 128), lambda i: (0, 0, i, 0))] * 2
        att_args = list(o_att)
    else:
        att_specs = [rows(ATT_OUT)]
        att_args = [o_att]
    params = [norm_w.reshape(1, D_MODEL), w_g, gate_b.reshape(1, N_BRANCH * D_MODEL), w_br_a, w_br_b, w_br_c, w_out]
    return pl.pallas_call(
        functools.partial(_merge_kernel, chunked=chunked),
        out_shape=jax.ShapeDtypeStruct(h.shape, F32),
        grid=(t // tm,),
        in_specs=[rows(D_MODEL), rows(GDN_VW), rows(SSD_INNER)] + att_specs + [full(a) for a in params],
        out_specs=rows(D_MODEL),
        compiler_params=pltpu.CompilerParams(
            dimension_semantics=("arbitrary",), vmem_limit_bytes=VMEM_LIMIT_BYTES),
        name="merge",
    )(h, o_a, o_b, *att_args, *params)


DEC_TB = 8


def _conv_step(cst_ref, x, w_ref):
    prev = [cst_ref[0, i] for i in range(CONV_WIDTH - 1)]
    y = x * w_ref[CONV_WIDTH - 1:CONV_WIDTH, :]
    for i, pr in enumerate(prev):
        y = y + pr * w_ref[i:i + 1, :]
    return y, prev[1:] + [x]


def _gdn_decode_kernel(qkv_ref, gate_ref, small_ref, cst_ref, s_ref, convw_ref, alog_ref, dtb_ref, normw_ref,
                       o_ref, so_ref, csto_ref):
    tb = qkv_ref.shape[0]
    x = qkv_ref[...]
    y, new_cst = _conv_step(cst_ref, x, convw_ref)
    for i, r in enumerate(new_cst):
        csto_ref[i] = r
    y = _silu(y)
    small = small_ref[...]
    beta_all = jax.nn.sigmoid(small)
    eg_all = jnp.exp(-jnp.exp(alog_ref[...]) * _softplus(small + dtb_ref[...]))
    for h in range(GDN_HEADS):
        lo = h * GDN_DK
        q = y[:, lo:lo + GDN_DK]
        k = y[:, GDN_QK + lo:GDN_QK + lo + GDN_DK]
        v = y[:, 2 * GDN_QK + h * GDN_DV:2 * GDN_QK + (h + 1) * GDN_DV]
        q = q * (lax.rsqrt(jnp.sum(q * q, axis=-1, keepdims=True) + RMS_EPS) * GDN_DK ** -0.5)
        k = k * lax.rsqrt(jnp.sum(k * k, axis=-1, keepdims=True) + RMS_EPS)
        k_t, q_t = k.T, q.T
        beta = beta_all[:, h:h + 1]
        eg = eg_all[:, GDN_HEADS + h:GDN_HEADS + h + 1]
        qk = jnp.sum(q * k, axis=-1, keepdims=True)
        o_rows = []
        for t in range(tb):
            s = s_ref[0, t, h]
            kc, qc = k_t[:, t:t + 1], q_t[:, t:t + 1]
            eg_t = eg[t:t + 1, :]
            ks = jnp.sum(s * kc, axis=0, keepdims=True)
            qs = jnp.sum(s * qc, axis=0, keepdims=True)
            u = beta[t:t + 1, :] * (v[t:t + 1, :] - eg_t * ks)
            o_rows.append(eg_t * qs + qk[t:t + 1, :] * u)
            so_ref[t, h] = s * eg_t + kc * u
        o = jnp.concatenate(o_rows, axis=0)
        o = o * lax.rsqrt(jnp.mean(o * o, axis=-1, keepdims=True) + RMS_EPS) * normw_ref[...]
        o_ref[:, h * GDN_DV:(h + 1) * GDN_DV] = o * _silu(gate_ref[:, h * GDN_DV:(h + 1) * GDN_DV])


def gdn_decode(a_qkv, a_gate, small, conv_state_t, state, layer, conv_w, a_log, dt_bias, norm_w):
    bsz = a_qkv.shape[0]
    tb = DEC_TB
    lanes = small.shape[-1]
    alog_row = jnp.zeros((1, lanes), F32).at[0, GDN_HEADS:2 * GDN_HEADS].set(a_log)
    dtb_row = jnp.zeros((1, lanes), F32).at[0, GDN_HEADS:2 * GDN_HEADS].set(dt_bias)
    nprev = CONV_WIDTH - 1
    rows = lambda wd: pl.BlockSpec((tb, wd), lambda i: (i, 0))
    const2 = lambda i: (0, 0)
    return pl.pallas_call(
        _gdn_decode_kernel,
        out_shape=(jax.ShapeDtypeStruct((bsz, GDN_VW), F32),
                   jax.ShapeDtypeStruct((bsz, GDN_HEADS, GDN_DK, GDN_DV), F32),
                   jax.ShapeDtypeStruct((nprev, bsz, GDN_CONV_DIM), F32)),
        grid=(bsz // tb,),
        in_specs=[rows(GDN_CONV_DIM), rows(GDN_VW), rows(lanes),
                  pl.BlockSpec((1, nprev, tb, GDN_CONV_DIM), lambda i: (layer, 0, i, 0)),
                  pl.BlockSpec((1, tb, GDN_HEADS, GDN_DK, GDN_DV), lambda i: (layer, i, 0, 0, 0)),
                  pl.BlockSpec((CONV_WIDTH, GDN_CONV_DIM), const2),
                  pl.BlockSpec((1, lanes), const2), pl.BlockSpec((1, lanes), const2),
                  pl.BlockSpec((1, GDN_DV), const2)],
        out_specs=(rows(GDN_VW),
                   pl.BlockSpec((tb, GDN_HEADS, GDN_DK, GDN_DV), lambda i: (i, 0, 0, 0)),
                   pl.BlockSpec((nprev, tb, GDN_CONV_DIM), lambda i: (0, i, 0))),
        compiler_params=pltpu.CompilerParams(
            dimension_semantics=("arbitrary",), vmem_limit_bytes=VMEM_LIMIT_BYTES),
        name="gdn_decode",
    )(a_qkv, a_gate, small, conv_state_t, state, conv_w, alog_row, dtb_row, norm_w.reshape(1, GDN_DV))


def _ssd_decode_kernel(xbc_ref, z_ref, small_ref, cst_ref, h_ref, convw_ref, convb_ref, alog_ref, dtb_ref, d_ref,
                       normw_ref, o_ref, ho_ref, csto_ref):
    tb = xbc_ref.shape[0]
    x = xbc_ref[...]
    y, new_cst = _conv_step(cst_ref, x, convw_ref)
    for i, r in enumerate(new_cst):
        csto_ref[i] = r
    y = _silu(y + convb_ref[...])
    xs = y[:, :SSD_INNER]
    bm = y[:, SSD_INNER:SSD_INNER + SSD_BC]
    cm = y[:, SSD_INNER + SSD_BC:]
    dt_all = _softplus(small_ref[...] + dtb_ref[...])
    eda_all = jnp.exp(dt_all * (-jnp.exp(alog_ref[...])))
    lane = lax.broadcasted_iota(jnp.int32, (SSD_HEADDIM, tb), 1)
    ys = []
    for h in range(SSD_HEADS):
        g = h // SSD_HPG
        ln = SSD_DT_LANE + h
        xd_t = (xs[:, h * SSD_HEADDIM:(h + 1) * SSD_HEADDIM] * dt_all[:, ln:ln + 1]).T
        ycols = jnp.zeros((SSD_HEADDIM, tb), F32)
        for t in range(tb):
            hn = (h_ref[0, t, h] * eda_all[t:t + 1, ln:ln + 1]
                  + xd_t[:, t:t + 1] * bm[t:t + 1, g * SSD_STATE:(g + 1) * SSD_STATE])
            ho_ref[t, h] = hn
            ycol = jnp.sum(hn * cm[t:t + 1, g * SSD_STATE:(g + 1) * SSD_STATE], axis=1, keepdims=True)
            ycols = jnp.where(lane == t, ycol, ycols)
        ys.append(ycols.T)
    yf = jnp.concatenate(ys, axis=-1) + d_ref[...] * xs
    yf = yf * _silu(z_ref[...])
    gw = SSD_INNER // SSD_GROUPS
    for g in range(SSD_GROUPS):
        seg = yf[:, g * gw:(g + 1) * gw]
        o_ref[:, g * gw:(g + 1) * gw] = (seg * lax.rsqrt(jnp.mean(seg * seg, axis=-1, keepdims=True) + RMS_EPS)
                                         * normw_ref[:, g * gw:(g + 1) * gw])


def ssd_decode(b_xbc, b_z, small, conv_state_t, state, layer, conv_w, conv_b, a_log, dt_bias, d_skip, norm_w):
    bsz = b_xbc.shape[0]
    tb = DEC_TB
    lanes = small.shape[-1]
    sl = slice(SSD_DT_LANE, SSD_DT_LANE + SSD_HEADS)
    alog_row = jnp.zeros((1, lanes), F32).at[0, sl].set(a_log)
    dtb_row = jnp.zeros((1, lanes), F32).at[0, sl].set(dt_bias)
    d_row = jnp.repeat(d_skip, SSD_HEADDIM).reshape(1, SSD_INNER)
    nprev = CONV_WIDTH - 1
    rows = lambda wd: pl.BlockSpec((tb, wd), lambda i: (i, 0))
    const2 = lambda i: (0, 0)
    return pl.pallas_call(
        _ssd_decode_kernel,
        out_shape=(jax.ShapeDtypeStruct((bsz, SSD_INNER), F32),
                   jax.ShapeDtypeStruct((bsz, SSD_HEADS, SSD_HEADDIM, SSD_STATE), F32),
                   jax.ShapeDtypeStruct((nprev, bsz, SSD_CONV_DIM), F32)),
        grid=(bsz // tb,),
        in_specs=[rows(SSD_CONV_DIM), rows(SSD_INNER), rows(lanes),
                  pl.BlockSpec((1, nprev, tb, SSD_CONV_DIM), lambda i: (layer, 0, i, 0)),
                  pl.BlockSpec((1, tb, SSD_HEADS, SSD_HEADDIM, SSD_STATE), lambda i: (layer, i, 0, 0, 0)),
                  pl.BlockSpec((CONV_WIDTH, SSD_CONV_DIM), const2), pl.BlockSpec((1, SSD_CONV_DIM), const2),
                  pl.BlockSpec((1, lanes), const2), pl.BlockSpec((1, lanes), const2),
                  pl.BlockSpec((1, SSD_INNER), const2), pl.BlockSpec((1, SSD_INNER), const2)],
        out_specs=(rows(SSD_INNER),
                   pl.BlockSpec((tb, SSD_HEADS, SSD_HEADDIM, SSD_STATE), lambda i: (i, 0, 0, 0)),
                   pl.BlockSpec((nprev, tb, SSD_CONV_DIM), lambda i: (0, i, 0))),
        compiler_params=pltpu.CompilerParams(
            dimension_semantics=("arbitrary",), vmem_limit_bytes=VMEM_LIMIT_BYTES),
        name="ssd_decode",
    )(b_xbc, b_z, small, conv_state_t, state, conv_w, conv_b.reshape(1, SSD_CONV_DIM),
      alog_row, dtb_row, d_row, norm_w.reshape(1, SSD_INNER))


ATT_DEC_TB = 8


def _attn_decode_kernel(q_ref, kvn_ref, c0_ref, c1_ref, c2_ref, o_ref, acc_ref):
    tb = q_ref.shape[0]
    hd = ATT_HEAD_DIM
    k = pl.program_id(1)
    slopes = _alibi_slopes_np()

    def column(rows):
        x_t = rows.T
        sel = lax.broadcasted_iota(jnp.int32, x_t.shape, 1) == k
        return jnp.sum(jnp.where(sel, x_t, 0.0), axis=1, keepdims=True)

    q_col = column(q_ref[...])
    cols = [[], [], [], []]
    og = [[None] * N_ATT_GROUPS for _ in range(ATT_HEADS_PER_GROUP)]
    lg = [[None] * N_ATT_GROUPS for _ in range(ATT_HEADS_PER_GROUP)]
    for gi, (c_ref, (win, dil)) in enumerate(zip((c0_ref, c1_ref, c2_ref), ATT_GROUPS)):
        kvn_col = column(kvn_ref[gi])
        j = lax.broadcasted_iota(jnp.int32, (1, win), 1)
        dist = win - j
        is_key = (dist % dil) == 0
        for hh in range(ATT_HEADS_PER_GROUP):
            lo = gi * ATT_OUT + hh * hd
            qc = q_col[lo:lo + hd]
            kn = kvn_col[hh * hd:(hh + 1) * hd]
            vn = kvn_col[ATT_OUT + hh * hd:ATT_OUT + (hh + 1) * hd]
            bias = jnp.where(is_key, dist.astype(F32) * float(-slopes[gi, hh]), ATT_MASKED)
            s = jnp.sum(c_ref[0, 0, 0, hh] * qc, axis=0, keepdims=True) + bias
            s_new = jnp.sum(qc * kn, axis=0, keepdims=True)
            m = jnp.maximum(jnp.max(s, axis=-1, keepdims=True), s_new)
            p = jnp.exp(s - m)
            p_new = jnp.exp(s_new - m)
            l = jnp.sum(p, axis=-1, keepdims=True) + p_new
            og[hh][gi] = (jnp.sum(c_ref[0, 0, 1, hh] * p, axis=-1, keepdims=True) + p_new * vn) * (1.0 / l)
            lg[hh][gi] = m + jnp.log(l)
    heads = []
    for hh in range(ATT_HEADS_PER_GROUP):
        m = functools.reduce(jnp.maximum, lg[hh])
        es = [jnp.exp(l - m) for l in lg[hh]]
        heads.append(sum(e * o for e, o in zip(es, og[hh])) * (1.0 / sum(es)))
    o_col = jnp.concatenate(heads, axis=0)

    @pl.when(k == 0)
    def _():
        acc_ref[...] = jnp.zeros_like(acc_ref)

    lane = lax.broadcasted_iota(jnp.int32, acc_ref.shape, 1)
    acc_ref[...] = jnp.where(lane == k, o_col, acc_ref[...])

    @pl.when(k == tb - 1)
    def _():
        o_ref[...] = acc_ref[...].T[:tb, :]


def attn_decode(q_rows, kv_new, caches, layer):
    bsz = q_rows.shape[0]
    tb = ATT_DEC_TB
    views, specs = [], []
    for cache, (win, dil) in zip(caches, ATT_GROUPS):
        assert cache.shape[2] == win and win % dil == 0
        views.append(jnp.transpose(cache, (0, 1, 3, 4, 5, 2)))
        specs.append(pl.BlockSpec((1, 1) + KV_TAIL + (win,), lambda i, k: (layer, i * tb + k, 0, 0, 0, 0)))
    return pl.pallas_call(
        _attn_decode_kernel,
        out_shape=jax.ShapeDtypeStruct((bsz, ATT_OUT), F32),
        grid=(bsz // tb, tb),
        in_specs=[pl.BlockSpec((tb, ATT_Q_N), lambda i, k: (i, 0)),
                  pl.BlockSpec((N_ATT_GROUPS, tb, 2 * ATT_OUT), lambda i, k: (0, i, 0))] + specs,
        out_specs=pl.BlockSpec((tb, ATT_OUT), lambda i, k: (i, 0)),
        scratch_shapes=[pltpu.VMEM((ATT_OUT, 128), F32)],
        compiler_params=pltpu.CompilerParams(
            dimension_semantics=("arbitrary", "arbitrary"), vmem_limit_bytes=VMEM_LIMIT_BYTES),
        name="attn_decode",
    )(q_rows, kv_new, *views)


PROMPT_TM = 512
PROMPT_FFN_TM = 256
KV_TAIL = (2, ATT_HEADS_PER_GROUP, ATT_HEAD_DIM)


def prompt_trunk(x, p):
    bsz, L, _ = x.shape
    t = bsz * L
    h = x.reshape(t, D_MODEL)
    new = [[] for _ in range(4 + N_ATT_GROUPS)]
    seq = lambda a: a.reshape(bsz, L, a.shape[-1])
    for l in range(DEPTH):
        h = ffn_block(h, p['ffn1_norm'][l], p['ffn1_w_in'][l], p['ffn1_w_out'][l], PROMPT_FFN_TM)
        qkv, gate, z, xbc, small = proj_ab(h, p['mix_norm'][l], p['w_ab'][l], PROMPT_TM)
        q5, kv5, kvrows = proj_c(h, p['mix_norm'][l], p['w_c'][l], p['q_norm'][l], p['k_norm'][l], PROMPT_TM, True)
        o_a, s_gdn, c_gdn = gdn_prompt(seq(qkv), seq(gate), seq(small), p['gdn_conv_w'][l], p['gdn_a_log'][l],
                                       p['gdn_dt_bias'][l], p['gdn_norm_w'][l])
        o_b, s_ssd, c_ssd = ssd_prompt(seq(xbc), seq(z), seq(small), p['ssd_conv_w'][l], p['ssd_conv_b'][l],
                                       p['ssd_a_log'][l], p['ssd_dt_bias'][l], p['ssd_d'][l], p['ssd_norm_w'][l])
        o5, lse5 = attn_prompt(q5, kv5, bsz)
        h = merge_block(h, o_a.reshape(t, GDN_VW), o_b.reshape(t, SSD_INNER), (o5, lse5), p['mix_norm'][l],
                        p['w_g'][l], p['gate_b'][l], p['w_br_a'][l], p['w_br_b'][l], p['w_br_c'][l], p['w_out'][l],
                        PROMPT_TM)
        h = ffn_block(h, p['ffn2_norm'][l], p['ffn2_w_in'][l], p['ffn2_w_out'][l], PROMPT_FFN_TM)
        kv_new = [kvrows[gi].reshape(bsz, L, 2 * ATT_OUT)[:, L - min(win, L):].reshape((bsz, min(win, L)) + KV_TAIL)
                  for gi, (win, _) in enumerate(ATT_GROUPS)]
        for lst, s in zip(new, (s_gdn, c_gdn, s_ssd, c_ssd, *kv_new)):
            lst.append(s)
    return h.reshape(bsz, L, D_MODEL), [jnp.stack(lst, axis=0) for lst in new]


def sample_trunk(x, p, state_gdn, state_gdn_conv, state_ssd, state_ssd_conv, caches):
    bsz = x.shape[0]
    assert x.shape[1] == 1
    h = x.reshape(bsz, D_MODEL)
    tm = bsz
    new = [[] for _ in range(4 + N_ATT_GROUPS)]
    gdn_conv_t = jnp.transpose(state_gdn_conv, (0, 2, 1, 3))
    ssd_conv_t = jnp.transpose(state_ssd_conv, (0, 2, 1, 3))
    for l in range(DEPTH):
        h = ffn_block(h, p['ffn1_norm'][l], p['ffn1_w_in'][l], p['ffn1_w_out'][l], tm)
        qkv, gate, z, xbc, small = proj_ab(h, p['mix_norm'][l], p['w_ab'][l], tm)
        q_rows, kvrows = proj_c(h, p['mix_norm'][l], p['w_c'][l], p['q_norm'][l], p['k_norm'][l], tm, False)
        o_a, s_gdn, c_gdn = gdn_decode(qkv, gate, small, gdn_conv_t, state_gdn, l, p['gdn_conv_w'][l],
                                       p['gdn_a_log'][l], p['gdn_dt_bias'][l], p['gdn_norm_w'][l])
        o_b, s_ssd, c_ssd = ssd_decode(xbc, z, small, ssd_conv_t, state_ssd, l, p['ssd_conv_w'][l],
                                       p['ssd_conv_b'][l], p['ssd_a_log'][l], p['ssd_dt_bias'][l], p['ssd_d'][l],
                                       p['ssd_norm_w'][l])
        o_c = attn_decode(q_rows, kvrows, caches, l)
        h = merge_block(h, o_a, o_b, o_c, p['mix_norm'][l], p['w_g'][l], p['gate_b'][l], p['w_br_a'][l],
                        p['w_br_b'][l], p['w_br_c'][l], p['w_out'][l], tm)
        h = ffn_block(h, p['ffn2_norm'][l], p['ffn2_w_in'][l], p['ffn2_w_out'][l], tm)
        kv_new = [kvrows[gi].reshape((bsz, 1) + KV_TAIL) for gi in range(N_ATT_GROUPS)]
        for lst, s in zip(new, (s_gdn, c_gdn, s_ssd, c_ssd, *kv_new)):
            lst.append(s)
    out = [jnp.stack(lst, axis=0) for lst in new]
    out[1] = jnp.transpose(out[1], (0, 2, 1, 3))
    out[3] = jnp.transpose(out[3], (0, 2, 1, 3))
    return h.reshape(bsz, 1, D_MODEL), out


def kernel(x_prompt, x_sample, state_gdn, state_gdn_conv, state_ssd, state_ssd_conv,
           cache_kv_w128, cache_kv_w512, cache_kv_w2048,
           ffn1_norm, ffn1_w_in, ffn1_w_out, mix_norm, w_in, gate_b,
           gdn_conv_w, gdn_a_log, gdn_dt_bias, gdn_norm_w,
           ssd_conv_w, ssd_conv_b, ssd_a_log, ssd_dt_bias, ssd_d, ssd_norm_w,
           q_norm, k_norm, w_br_a, w_br_b, w_br_c, w_out,
           ffn2_norm, ffn2_w_in, ffn2_w_out):
    w_ab, w_c, w_g = prep_weights(w_in)
    p = {'ffn1_norm': ffn1_norm, 'ffn1_w_in': ffn1_w_in.astype(BF16), 'ffn1_w_out': ffn1_w_out.astype(BF16),
         'mix_norm': mix_norm, 'w_ab': w_ab, 'w_c': w_c, 'w_g': w_g, 'gate_b': gate_b,
         'gdn_conv_w': gdn_conv_w, 'gdn_a_log': gdn_a_log, 'gdn_dt_bias': gdn_dt_bias, 'gdn_norm_w': gdn_norm_w,
         'ssd_conv_w': ssd_conv_w, 'ssd_conv_b': ssd_conv_b, 'ssd_a_log': ssd_a_log,
         'ssd_dt_bias': ssd_dt_bias, 'ssd_d': ssd_d, 'ssd_norm_w': ssd_norm_w,
         'q_norm': q_norm, 'k_norm': k_norm, 'w_br_a': w_br_a.astype(BF16), 'w_br_b': w_br_b.astype(BF16),
         'w_br_c': w_br_c.astype(BF16), 'w_out': w_out.astype(BF16),
         'ffn2_norm': ffn2_norm, 'ffn2_w_in': ffn2_w_in.astype(BF16), 'ffn2_w_out': ffn2_w_out.astype(BF16)}
    y_prompt, st_p = prompt_trunk(x_prompt, p)
    y_sample, st_s = sample_trunk(x_sample, p, state_gdn, state_gdn_conv, state_ssd, state_ssd_conv,
                                  [cache_kv_w128, cache_kv_w512, cache_kv_w2048])
    gdn_p, gdn_conv_p, ssd_p, ssd_conv_p, kv128_p, kv512_p, kv2048_p = st_p
    gdn_s, gdn_conv_s, ssd_s, ssd_conv_s, kv128_s, kv512_s, kv2048_s = st_s
    return (y_prompt, y_sample, gdn_p, gdn_s, gdn_conv_p, gdn_conv_s, ssd_p, ssd_s, ssd_conv_p, ssd_conv_s,
            kv128_p, kv128_s, kv512_p, kv512_s, kv2048_p, kv2048_s)
```

```python
import functools
import math
import jax, jax.numpy as jnp
from jax import lax
import numpy as np
from jax.experimental import pallas as pl
from jax.experimental.pallas import tpu as pltpu

D_MODEL = 1024
BATCH = 8
SEQ = 2048
DEPTH = 4
DEC_BATCH = 128
DEC_SEQ = 1
PAST_LEN = 2048

CONV_WIDTH = 4
RMS_EPS = 1e-6
GDN_HEADS = 4
GDN_DK = 128
GDN_DV = 128
GDN_CHUNK = 64
GDN_QK = GDN_HEADS * GDN_DK
GDN_VW = GDN_HEADS * GDN_DV
GDN_CONV_DIM = 2 * GDN_QK + GDN_VW
SSD_HEADS = 8
SSD_HEADDIM = 64
SSD_GROUPS = 2
SSD_STATE = 128
SSD_CHUNK = 128
SSD_INNER = SSD_HEADS * SSD_HEADDIM
SSD_CONV_DIM = SSD_INNER + 2 * SSD_GROUPS * SSD_STATE
ATT_GROUPS = ((128, 1), (512, 4), (2048, 16))
N_ATT_GROUPS = len(ATT_GROUPS)
ATT_HEADS_PER_GROUP = 4
ATT_HEAD_DIM = 64
ATT_HEADS = N_ATT_GROUPS * ATT_HEADS_PER_GROUP
ATT_OUT = ATT_HEADS_PER_GROUP * ATT_HEAD_DIM
D_FF = ((8 * D_MODEL // 3 + 127) // 128) * 128
N_BRANCH = 3
IN_SPLITS = (GDN_CONV_DIM, GDN_VW, GDN_HEADS, GDN_HEADS, SSD_INNER, SSD_CONV_DIM, SSD_HEADS,
             3 * ATT_HEADS * ATT_HEAD_DIM, N_BRANCH * D_MODEL)
D_IN = sum(IN_SPLITS)

F32 = jnp.float32
BF16 = jnp.bfloat16
VMEM_LIMIT_BYTES = 56 * 1024 * 1024


def rmsnorm(x, w):
    xf = x.astype(F32)
    y = xf * lax.rsqrt(jnp.mean(xf * xf, axis=-1, keepdims=True) + RMS_EPS)
    return (y * w.astype(F32)).astype(x.dtype)


def l2norm(x):
    xf = x.astype(F32)
    return xf * lax.rsqrt(jnp.sum(xf * xf, axis=-1, keepdims=True) + RMS_EPS)


def causal_dwconv(x, buf, w, b=None):
    K = w.shape[0]
    L = x.shape[1]
    xp = jnp.concatenate([buf.astype(x.dtype), x], axis=1)
    y = xp[:, 0:L] * w[0]
    for i in range(1, K):
        y = y + xp[:, i:i + L] * w[i]
    if b is not None:
        y = y + b
    return y, xp[:, -(K - 1):]


def alibi_slopes():
    idx = np.arange(1, ATT_HEADS + 1, dtype=np.float32)
    slopes = np.exp2(-8.0 * idx / ATT_HEADS).astype(np.float32)
    return jnp.asarray(slopes).reshape(N_ATT_GROUPS, ATT_HEADS_PER_GROUP)


def gated_delta_chunked(q, k, v, g, beta, s0):
    bsz, L, H, dk = q.shape
    dv = v.shape[-1]
    C = min(GDN_CHUNK, L)
    n = -(-L // C)
    pad = n * C - L

    def prep(t):
        t = t.astype(F32)
        t = jnp.pad(t, [(0, 0), (0, pad)] + [(0, 0)] * (t.ndim - 2))
        t = t.reshape((bsz, n, C) + t.shape[2:])
        return jnp.moveaxis(t, 3, 2)

    q, k, v, g, beta = prep(q), prep(k), prep(v), prep(g), prep(beta)
    gam = jnp.cumsum(g, axis=-1)
    incl = jnp.tril(jnp.ones((C, C), bool))
    strict = jnp.tril(jnp.ones((C, C), bool), -1)
    dec = jnp.exp(jnp.where(incl, gam[..., :, None] - gam[..., None, :], -jnp.inf))
    kk = jnp.einsum('bnhcd,bnhsd->bnhcs', k, k)
    a_mat = jnp.where(strict, beta[..., :, None] * kk * dec, 0.0) + jnp.eye(C, dtype=F32)
    rhs = jnp.concatenate([v * beta[..., None], k * (beta * jnp.exp(gam))[..., None]], axis=-1)
    sol = lax.linalg.triangular_solve(a_mat, rhs, left_side=True, lower=True, unit_diagonal=True)
    u0, w = sol[..., :dv], sol[..., dv:]
    qk = jnp.einsum('bnhcd,bnhsd->bnhcs', q, k) * dec
    q_dec = q * jnp.exp(gam)[..., None]
    k_dec = k * jnp.exp(gam[..., -1:] - gam)[..., None]
    g_last = jnp.exp(gam[..., -1])

    def step(S, xs):
        u0_c, w_c, q_c, qk_c, k_c, gl = xs
        u = u0_c - jnp.einsum('bhcd,bhde->bhce', w_c, S)
        o = jnp.einsum('bhcd,bhde->bhce', q_c, S) + jnp.einsum('bhcs,bhse->bhce', qk_c, u)
        S = S * gl[..., None, None] + jnp.einsum('bhcd,bhce->bhde', k_c, u)
        return S, o

    xs = tuple(jnp.moveaxis(t, 1, 0) for t in (u0, w, q_dec, qk, k_dec, g_last))
    S, o = lax.scan(step, s0.astype(F32), xs)
    o = jnp.moveaxis(jnp.moveaxis(o, 0, 1), 2, 3).reshape(bsz, n * C, H, dv)[:, :L]
    return o, S


def ssd_chunked(x, dt, a, bm, cm, h0):
    bsz, L, H, P = x.shape
    C = min(SSD_CHUNK, L)
    n = -(-L // C)
    pad = n * C - L

    def prep(t):
        t = t.astype(F32)
        t = jnp.pad(t, [(0, 0), (0, pad)] + [(0, 0)] * (t.ndim - 2))
        return t.reshape((bsz, n, C) + t.shape[2:])

    xd = prep(x.astype(F32) * dt[..., None])
    da = prep(dt * a)
    bm, cm = prep(bm), prep(cm)
    acum = jnp.cumsum(da, axis=2)
    incl = jnp.tril(jnp.ones((C, C), bool))
    seg = acum[:, :, :, None, :] - acum[:, :, None, :, :]
    lmat = jnp.exp(jnp.where(incl[:, :, None], seg, -jnp.inf))
    scores = jnp.einsum('bnlhk,bnshk->bnlsh', cm, bm) * lmat
    y_diag = jnp.einsum('bnlsh,bnshp->bnlhp', scores, xd)
    decay_states = jnp.exp(acum[:, :, -1:, :] - acum)
    states = jnp.einsum('bnshk,bnshp->bnhpk', bm * decay_states[..., None], xd)
    chunk_decay = jnp.exp(acum[:, :, -1, :])

    def step(h, xs):
        st, cd = xs
        return h * cd[..., None, None] + st, h

    h_fin, h_in = lax.scan(step, h0.astype(F32), (jnp.moveaxis(states, 1, 0), jnp.moveaxis(chunk_decay, 1, 0)))
    h_in = jnp.moveaxis(h_in, 0, 1)
    y_off = jnp.einsum('bnlhk,bnhpk->bnlhp', cm * jnp.exp(acum)[..., None], h_in)
    y = (y_diag + y_off).reshape(bsz, n * C, H, P)[:, :L]
    return y, h_fin


def dilated_band_attention(q, k, v, slopes, window, dil):
    bsz, L, H, hd = q.shape
    band = window // dil
    unit = band * dil
    nb = -(-L // unit)
    Lp = nb * unit
    pad = Lp - L

    def blocks(t):
        t = jnp.pad(t.astype(F32), ((0, 0), (0, pad), (0, 0), (0, 0)))
        return t.reshape(bsz, nb, band, dil, H, hd)

    def with_prev(t):
        prev = jnp.pad(t, ((0, 0), (1, 0), (0, 0), (0, 0), (0, 0), (0, 0)))[:, :-1]
        return jnp.concatenate([prev, t], axis=2)

    qb = blocks(q)
    kk, vv = with_prev(blocks(k)), with_prev(blocks(v))
    s = jnp.einsum('bnirhd,bnjrhd->bnrhij', qb, kk)
    i = jnp.arange(band)[:, None]
    j = jnp.arange(2 * band)[None, :]
    du = i + band - j
    blk = jnp.arange(nb)[:, None, None]
    valid = (du >= 0) & (du <= band) & ((blk > 0) | (j >= band))
    bias = -(slopes.astype(F32) * dil)[:, None, None] * du
    s = jnp.where(valid[:, None, None], s + bias, -jnp.inf)
    lse = jax.nn.logsumexp(s, axis=-1)
    o = jnp.einsum('bnrhij,bnjrhd->bnirhd', jnp.exp(s - lse[..., None]), vv)
    o = o.reshape(bsz, Lp, H, hd)[:, :L]
    lse = jnp.moveaxis(lse, -1, 2).reshape(bsz, Lp, H)[:, :L]
    return o, lse


def dilated_gather_attention(q, rows_new, kv_buf, slopes, window, dil):
    bsz, Lq, H, hd = q.shape
    Lw = kv_buf.shape[1]
    n_keys = window // dil + 1
    i = jnp.arange(Lq)[:, None]
    j = jnp.arange(n_keys)[None, :]
    idx = Lw + i - j * dil
    valid = idx >= 0
    from_buf = (idx < Lw)[:, :, None, None, None]
    rows = jnp.where(from_buf,
                     kv_buf[:, jnp.clip(idx, 0, Lw - 1)].astype(F32),
                     rows_new[:, jnp.clip(idx - Lw, 0, Lq - 1)].astype(F32))
    kg, vg = rows[:, :, :, 0], rows[:, :, :, 1]
    s = jnp.einsum('bqhd,bqjhd->bhqj', q.astype(F32), kg) - (slopes.astype(F32) * dil)[:, None, None] * j
    s = jnp.where(valid, s, -jnp.inf)
    lse = jax.nn.logsumexp(s, axis=-1)
    o = jnp.einsum('bhqj,bqjhd->bqhd', jnp.exp(s - lse[..., None]), vg)
    return o, jnp.moveaxis(lse, 1, 2)


def token_mixing(u, p, l, s_gdn, c_gdn, s_ssd, c_ssd, kv_bufs):
    bsz, L, _ = u.shape
    offs = np.cumsum(np.array(IN_SPLITS))[:-1].tolist()
    (a_qkv, a_gate, a_beta, a_alpha, b_z, b_xbc, b_dt, c_qkv, gate_logits) = jnp.split(u @ p['w_in'][l], offs, axis=-1)

    qkv, c_gdn_new = causal_dwconv(a_qkv, c_gdn, p['gdn_conv_w'][l])
    qkv = jax.nn.silu(qkv)
    q, k, v = jnp.split(qkv, [GDN_QK, 2 * GDN_QK], axis=-1)
    q = l2norm(q.reshape(bsz, L, GDN_HEADS, GDN_DK)) * GDN_DK ** -0.5
    k = l2norm(k.reshape(bsz, L, GDN_HEADS, GDN_DK))
    v = v.reshape(bsz, L, GDN_HEADS, GDN_DV)
    beta = jax.nn.sigmoid(a_beta.astype(F32))
    g = -jnp.exp(p['gdn_a_log'][l].astype(F32)) * jax.nn.softplus((a_alpha + p['gdn_dt_bias'][l]).astype(F32))
    o_a, s_gdn_new = gated_delta_chunked(q, k, v, g, beta, s_gdn)
    o_a = rmsnorm(o_a, p['gdn_norm_w'][l]) * jax.nn.silu(a_gate.reshape(bsz, L, GDN_HEADS, GDN_DV).astype(F32))
    o_a = o_a.reshape(bsz, L, GDN_VW).astype(u.dtype)

    xbc, c_ssd_new = causal_dwconv(b_xbc, c_ssd, p['ssd_conv_w'][l], p['ssd_conv_b'][l])
    xbc = jax.nn.silu(xbc)
    xs, bm, cm = jnp.split(xbc, [SSD_INNER, SSD_INNER + SSD_GROUPS * SSD_STATE], axis=-1)
    xs = xs.reshape(bsz, L, SSD_HEADS, SSD_HEADDIM)
    rep = SSD_HEADS // SSD_GROUPS
    bm = jnp.repeat(bm.reshape(bsz, L, SSD_GROUPS, SSD_STATE), rep, axis=2)
    cm = jnp.repeat(cm.reshape(bsz, L, SSD_GROUPS, SSD_STATE), rep, axis=2)
    dt = jax.nn.softplus((b_dt + p['ssd_dt_bias'][l]).astype(F32))
    a = -jnp.exp(p['ssd_a_log'][l].astype(F32))
    y, s_ssd_new = ssd_chunked(xs, dt, a, bm, cm, s_ssd)
    y = y + p['ssd_d'][l].astype(F32)[:, None] * xs.astype(F32)
    y = y.reshape(bsz, L, SSD_INNER) * jax.nn.silu(b_z.astype(F32))
    y = rmsnorm(y.reshape(bsz, L, SSD_GROUPS, SSD_INNER // SSD_GROUPS),
                p['ssd_norm_w'][l].reshape(SSD_GROUPS, SSD_INNER // SSD_GROUPS))
    o_b = y.reshape(bsz, L, SSD_INNER).astype(u.dtype)

    cqkv = c_qkv.reshape(bsz, L, 3, N_ATT_GROUPS, ATT_HEADS_PER_GROUP, ATT_HEAD_DIM)
    cq = rmsnorm(cqkv[:, :, 0], p['q_norm'][l]) * ATT_HEAD_DIM ** -0.5
    ck = rmsnorm(cqkv[:, :, 1], p['k_norm'][l])
    cv = cqkv[:, :, 2]
    slopes = alibi_slopes()
    outs, lses, kv_new = [], [], []
    for gi, (win, dil) in enumerate(ATT_GROUPS):
        qg, kg, vg = cq[:, :, gi], ck[:, :, gi], cv[:, :, gi]
        rows = jnp.stack([kg, vg], axis=2)
        if kv_bufs is None:
            o, lse = dilated_band_attention(qg, kg, vg, slopes[gi], win, dil)
            kv_new.append(rows[:, -min(win, L):])
        else:
            o, lse = dilated_gather_attention(qg, rows, kv_bufs[gi], slopes[gi], win, dil)
            kv_new.append(rows)
        outs.append(o)
        lses.append(lse)
    wts = jax.nn.softmax(jnp.stack(lses, axis=2), axis=2)
    o_c = jnp.sum(wts[..., None] * jnp.stack(outs, axis=2), axis=2).reshape(bsz, L, ATT_OUT).astype(u.dtype)

    gates = jax.nn.sigmoid(gate_logits + p['gate_b'][l]).reshape(bsz, L, N_BRANCH, D_MODEL)
    merged = (gates[:, :, 0] * (o_a @ p['w_br_a'][l])
              + gates[:, :, 1] * (o_b @ p['w_br_b'][l])
              + gates[:, :, 2] * (o_c @ p['w_br_c'][l]))
    out = merged @ p['w_out'][l]
    return out, (s_gdn_new.astype(s_gdn.dtype), c_gdn_new, s_ssd_new.astype(s_ssd.dtype), c_ssd_new, *kv_new)


def _ffn_kernel(h_ref, nw_ref, wi_ref, wo_ref, o_ref):
    h = h_ref[...]
    xn = h * lax.rsqrt(jnp.mean(h * h, axis=-1, keepdims=True) + RMS_EPS) * nw_ref[...]
    gu = jnp.dot(xn.astype(BF16), wi_ref[...], preferred_element_type=F32)
    g, up = gu[:, :D_FF], gu[:, D_FF:]
    act = (g * jax.nn.sigmoid(g) * up).astype(BF16)
    y = jnp.dot(act, wo_ref[...], preferred_element_type=F32)
    o_ref[...] = h + 0.5 * y


def ffn_block(h, norm_w, w_in_bf16, w_out_bf16, tm):
    t = h.shape[0]
    assert t % tm == 0
    const = dict(pipeline_mode=pl.Buffered(1))
    return pl.pallas_call(
        _ffn_kernel,
        out_shape=jax.ShapeDtypeStruct(h.shape, h.dtype),
        grid=(t // tm,),
        in_specs=[
            pl.BlockSpec((tm, D_MODEL), lambda i: (i, 0)),
            pl.BlockSpec((1, D_MODEL), lambda i: (0, 0), **const),
            pl.BlockSpec((D_MODEL, 2 * D_FF), lambda i: (0, 0), **const),
            pl.BlockSpec((D_FF, D_MODEL), lambda i: (0, 0), **const),
        ],
        out_specs=pl.BlockSpec((tm, D_MODEL), lambda i: (i, 0)),
        compiler_params=pltpu.CompilerParams(
            dimension_semantics=("arbitrary",), vmem_limit_bytes=VMEM_LIMIT_BYTES),
        name="ffn",
    )(h, norm_w.reshape(1, D_MODEL), w_in_bf16, w_out_bf16)


_DN_BMM = (((2,), (1,)), ((0,), (0,)))
_DN_BMM_NT = (((2,), (2,)), ((0,), (0,)))


def _bmm(a, b):
    return lax.dot_general(a.astype(BF16), b.astype(BF16), _DN_BMM, preferred_element_type=F32)


def _bmm_nt(a, b):
    return lax.dot_general(a.astype(BF16), b.astype(BF16), _DN_BMM_NT, preferred_element_type=F32)


def _split2(x):
    hi = x.astype(BF16)
    return hi, (x - hi.astype(F32)).astype(BF16)


def _split3(x):
    x1 = x.astype(BF16)
    r1 = x - x1.astype(F32)
    x2 = r1.astype(BF16)
    x3 = (r1 - x2.astype(F32)).astype(BF16)
    return x1, x2, x3


def _bmm_hi(a, b):
    a1, a2 = _split2(a)
    b1, b2 = _split2(b)
    dot = functools.partial(lax.dot_general, dimension_numbers=_DN_BMM, preferred_element_type=F32)
    return dot(a1, b1) + (dot(a1, b2) + dot(a2, b1))


def _cumsum_rows(tri, x):
    dot = functools.partial(lax.dot_general, dimension_numbers=_DN_BMM, preferred_element_type=F32)
    x1, x2, x3 = _split3(x)
    return dot(tri, x1) + (dot(tri, x2) + dot(tri, x3))


def _softplus(x):
    return jnp.maximum(x, 0.0) + jnp.log(1.0 + jnp.exp(-jnp.abs(x)))


def _silu(x):
    return x * jax.nn.sigmoid(x)


GDN_INV_BLOCK = 16


def _unit_lower_solve(a, rhs, row, col, mm):
    c = a.shape[-1]
    eye = (row == col).astype(F32)
    same_blk = (row // GDN_INV_BLOCK) == (col // GDN_INV_BLOCK)
    d = jnp.where(same_blk, a, 0.0)
    n = a - d
    x = eye - d
    pw = d
    k = 2
    while k < GDN_INV_BLOCK:
        pw = mm(pw, pw)
        x = x + mm(x, pw)
        k *= 2
    m = mm(x, n)
    y = mm(x, rhs)
    nblk = c // GDN_INV_BLOCK
    q = eye - m
    pw = m
    k = 2
    while k < nblk:
        pw = mm(pw, pw)
        q = q + mm(q, pw)
        k *= 2
    return mm(q, y)


def _gdn_prompt_kernel(qkv_ref, gate_ref, small_ref, convw_ref, alog_ref, dtb_ref, normw_ref,
                       o_ref, s_ref, cst_ref, xp_ref):
    c = GDN_CHUNK
    nb = qkv_ref.shape[0]

    @pl.when(pl.program_id(0) == 0)
    def _():
        s_ref[...] = jnp.zeros_like(s_ref)
        xp_ref[:, 0:8, :] = jnp.zeros((nb, 8, GDN_CONV_DIM), F32)

    x = qkv_ref[...]
    xp_ref[:, 8:8 + c, :] = x
    w = convw_ref[...]
    y = (xp_ref[:, 5:5 + c, :] * w[0:1, :] + xp_ref[:, 6:6 + c, :] * w[1:2, :]
         + xp_ref[:, 7:7 + c, :] * w[2:3, :] + x * w[3:4, :])
    xp_ref[:, 0:8, :] = xp_ref[:, c:c + 8, :]
    cst_ref[...] = x[:, c - (CONV_WIDTH - 1):, :]
    y = _silu(y)

    small = small_ref[...]
    beta_all = jax.nn.sigmoid(small)
    g_all = -jnp.exp(alog_ref[...]) * _softplus(small + dtb_ref[...])
    row = lax.broadcasted_iota(jnp.int32, (c, c), 0)
    col = lax.broadcasted_iota(jnp.int32, (c, c), 1)
    incl = row >= col
    strict = row > col
    tri = jnp.broadcast_to(incl.astype(BF16), (nb, c, c))
    gam_all = _cumsum_rows(tri, g_all)
    gam_all_t = jnp.swapaxes(gam_all, 1, 2)
    egam_all = jnp.exp(gam_all)

    for h in range(GDN_HEADS):
        lo = h * GDN_DK
        q = y[:, :, lo:lo + GDN_DK]
        k = y[:, :, GDN_QK + lo:GDN_QK + lo + GDN_DK]
        v = y[:, :, 2 * GDN_QK + h * GDN_DV:2 * GDN_QK + (h + 1) * GDN_DV]
        q = q * (lax.rsqrt(jnp.sum(q * q, axis=-1, keepdims=True) + RMS_EPS) * GDN_DK ** -0.5)
        k = k * lax.rsqrt(jnp.sum(k * k, axis=-1, keepdims=True) + RMS_EPS)
        beta = beta_all[:, :, h:h + 1]
        gam = gam_all[:, :, GDN_HEADS + h:GDN_HEADS + h + 1]
        gam_row = gam_all_t[:, GDN_HEADS + h:GDN_HEADS + h + 1, :]
        egam = egam_all[:, :, GDN_HEADS + h:GDN_HEADS + h + 1]
        gam_last = gam_row[:, :, c - 1:c]
        dec = jnp.where(incl, jnp.exp(jnp.minimum(gam - gam_row, 0.0)), 0.0)
        kk = _bmm_nt(k, k)
        a_mat = jnp.where(strict, beta * kk * dec, 0.0)
        rhs = jnp.concatenate([v * beta, k * (beta * egam)], axis=-1)
        sol = _unit_lower_solve(a_mat, rhs, row, col, _bmm)
        u0, wk = sol[:, :, :GDN_DV], sol[:, :, GDN_DV:]
        qk = _bmm_nt(q, k) * dec
        q_dec = q * egam
        k_dec = k * jnp.exp(gam_last - gam)
        s = s_ref[:, h]
        u = u0 - _bmm(wk, s)
        o = _bmm(q_dec, s) + _bmm(qk, u)
        s_ref[:, h] = s * jnp.exp(gam_last) + _bmm(jnp.swapaxes(k_dec, 1, 2), u)
        o = o * lax.rsqrt(jnp.mean(o * o, axis=-1, keepdims=True) + RMS_EPS) * normw_ref[...]
        o_ref[:, :, h * GDN_DV:(h + 1) * GDN_DV] = o * _silu(gate_ref[:, :, h * GDN_DV:(h + 1) * GDN_DV])


def gdn_prompt(a_qkv, a_gate, small, conv_w, a_log, dt_bias, norm_w):
    bsz, L, _ = a_qkv.shape
    c = GDN_CHUNK
    assert L % c == 0
    lanes = small.shape[-1]
    alog_row = jnp.zeros((1, lanes), F32).at[0, GDN_HEADS:2 * GDN_HEADS].set(a_log)
    dtb_row = jnp.zeros((1, lanes), F32).at[0, GDN_HEADS:2 * GDN_HEADS].set(dt_bias)
    const2 = lambda n: (0, 0)
    return pl.pallas_call(
        _gdn_prompt_kernel,
        out_shape=(jax.ShapeDtypeStruct((bsz, L, GDN_VW), F32),
                   jax.ShapeDtypeStruct((bsz, GDN_HEADS, GDN_DK, GDN_DV), F32),
                   jax.ShapeDtypeStruct((bsz, CONV_WIDTH - 1, GDN_CONV_DIM), F32)),
        grid=(L // c,),
        in_specs=[
            pl.BlockSpec((bsz, c, GDN_CONV_DIM), lambda n: (0, n, 0)),
            pl.BlockSpec((bsz, c, GDN_VW), lambda n: (0, n, 0)),
            pl.BlockSpec((bsz, c, lanes), lambda n: (0, n, 0)),
            pl.BlockSpec((CONV_WIDTH, GDN_CONV_DIM), const2),
            pl.BlockSpec((1, lanes), const2),
            pl.BlockSpec((1, lanes), const2),
            pl.BlockSpec((1, GDN_DV), const2),
        ],
        out_specs=(
            pl.BlockSpec((bsz, c, GDN_VW), lambda n: (0, n, 0)),
            pl.BlockSpec((bsz, GDN_HEADS, GDN_DK, GDN_DV), lambda n: (0, 0, 0, 0)),
            pl.BlockSpec((bsz, CONV_WIDTH - 1, GDN_CONV_DIM), lambda n: (0, 0, 0)),
        ),
        scratch_shapes=[pltpu.VMEM((bsz, c + 8, GDN_CONV_DIM), F32)],
        compiler_params=pltpu.CompilerParams(
            dimension_semantics=("arbitrary",), vmem_limit_bytes=VMEM_LIMIT_BYTES),
        name="gdn_prompt",
    )(a_qkv, a_gate, small, conv_w, alog_row, dtb_row, norm_w.reshape(1, GDN_DV))


SSD_DT_LANE = 2 * GDN_HEADS
SSD_BC = SSD_GROUPS * SSD_STATE
SSD_HPG = SSD_HEADS // SSD_GROUPS


def _ssd_prompt_kernel(xbc_ref, z_ref, small_ref, convw_ref, convb_ref, alog_ref, dtb_ref, d_ref, normw_ref,
                       o_ref, h_ref, cst_ref, xp_ref):
    c = SSD_CHUNK
    nb = xbc_ref.shape[0]

    @pl.when(pl.program_id(0) == 0)
    def _():
        h_ref[...] = jnp.zeros_like(h_ref)
        xp_ref[:, 0:8, :] = jnp.zeros((nb, 8, SSD_CONV_DIM), F32)

    x = xbc_ref[...]
    xp_ref[:, 8:8 + c, :] = x
    w = convw_ref[...]
    y = (xp_ref[:, 5:5 + c, :] * w[0:1, :] + xp_ref[:, 6:6 + c, :] * w[1:2, :]
         + xp_ref[:, 7:7 + c, :] * w[2:3, :] + x * w[3:4, :]) + convb_ref[...]
    xp_ref[:, 0:8, :] = xp_ref[:, c:c + 8, :]
    cst_ref[...] = x[:, c - (CONV_WIDTH - 1):, :]
    y = _silu(y)
    xs = y[:, :, :SSD_INNER]
    bm = y[:, :, SSD_INNER:SSD_INNER + SSD_BC]
    cm = y[:, :, SSD_INNER + SSD_BC:]

    dt_all = _softplus(small_ref[...] + dtb_ref[...])
    da_all = dt_all * (-jnp.exp(alog_ref[...]))
    row = lax.broadcasted_iota(jnp.int32, (c, c), 0)
    col = lax.broadcasted_iota(jnp.int32, (c, c), 1)
    incl = row >= col
    tri = jnp.broadcast_to(incl.astype(BF16), (nb, c, c))
    acum_all = _cumsum_rows(tri, da_all)
    acum_all_t = jnp.swapaxes(acum_all, 1, 2)
    eacum_all = jnp.exp(acum_all)

    ys = []
    for g in range(SSD_GROUPS):
        bg = bm[:, :, g * SSD_STATE:(g + 1) * SSD_STATE]
        cg = cm[:, :, g * SSD_STATE:(g + 1) * SSD_STATE]
        scores = _bmm_nt(cg, bg)
        for hh in range(SSD_HPG):
            h = g * SSD_HPG + hh
            ln = SSD_DT_LANE + h
            dtc = dt_all[:, :, ln:ln + 1]
            ac = acum_all[:, :, ln:ln + 1]
            ar = acum_all_t[:, ln:ln + 1, :]
            alast = ar[:, :, c - 1:c]
            lmat = jnp.where(incl, jnp.exp(jnp.minimum(ac - ar, 0.0)), 0.0)
            xd = xs[:, :, h * SSD_HEADDIM:(h + 1) * SSD_HEADDIM] * dtc
            y_diag = _bmm(scores * lmat, xd)
            hin = h_ref[:, h]
            y_off = _bmm_nt(cg * eacum_all[:, :, ln:ln + 1], hin)
            states = _bmm(jnp.swapaxes(xd, 1, 2), bg * jnp.exp(alast - ac))
            h_ref[:, h] = hin * jnp.exp(alast) + states
            ys.append(y_diag + y_off)
    yf = jnp.concatenate(ys, axis=-1) + d_ref[...] * xs
    yf = yf * _silu(z_ref[...])
    gw = SSD_INNER // SSD_GROUPS
    for g in range(SSD_GROUPS):
        seg = yf[:, :, g * gw:(g + 1) * gw]
        o_ref[:, :, g * gw:(g + 1) * gw] = (seg * lax.rsqrt(jnp.mean(seg * seg, axis=-1, keepdims=True) + RMS_EPS)
                                            * normw_ref[:, g * gw:(g + 1) * gw])


def ssd_prompt(b_xbc, b_z, small, conv_w, conv_b, a_log, dt_bias, d_skip, norm_w):
    bsz, L, _ = b_xbc.shape
    c = SSD_CHUNK
    assert L % c == 0
    lanes = small.shape[-1]
    sl = slice(SSD_DT_LANE, SSD_DT_LANE + SSD_HEADS)
    alog_row = jnp.zeros((1, lanes), F32).at[0, sl].set(a_log)
    dtb_row = jnp.zeros((1, lanes), F32).at[0, sl].set(dt_bias)
    d_row = jnp.repeat(d_skip, SSD_HEADDIM).reshape(1, SSD_INNER)
    const2 = lambda n: (0, 0)
    return pl.pallas_call(
        _ssd_prompt_kernel,
        out_shape=(jax.ShapeDtypeStruct((bsz, L, SSD_INNER), F32),
                   jax.ShapeDtypeStruct((bsz, SSD_HEADS, SSD_HEADDIM, SSD_STATE), F32),
                   jax.ShapeDtypeStruct((bsz, CONV_WIDTH - 1, SSD_CONV_DIM), F32)),
        grid=(L // c,),
        in_specs=[
            pl.BlockSpec((bsz, c, SSD_CONV_DIM), lambda n: (0, n, 0)),
            pl.BlockSpec((bsz, c, SSD_INNER), lambda n: (0, n, 0)),
            pl.BlockSpec((bsz, c, lanes), lambda n: (0, n, 0)),
            pl.BlockSpec((CONV_WIDTH, SSD_CONV_DIM), const2),
            pl.BlockSpec((1, SSD_CONV_DIM), const2),
            pl.BlockSpec((1, lanes), const2),
            pl.BlockSpec((1, lanes), const2),
            pl.BlockSpec((1, SSD_INNER), const2),
            pl.BlockSpec((1, SSD_INNER), const2),
        ],
        out_specs=(
            pl.BlockSpec((bsz, c, SSD_INNER), lambda n: (0, n, 0)),
            pl.BlockSpec((bsz, SSD_HEADS, SSD_HEADDIM, SSD_STATE), lambda n: (0, 0, 0, 0)),
            pl.BlockSpec((bsz, CONV_WIDTH - 1, SSD_CONV_DIM), lambda n: (0, 0, 0)),
        ),
        scratch_shapes=[pltpu.VMEM((bsz, c + 8, SSD_CONV_DIM), F32)],
        compiler_params=pltpu.CompilerParams(
            dimension_semantics=("arbitrary",), vmem_limit_bytes=VMEM_LIMIT_BYTES),
        name="ssd_prompt",
    )(b_xbc, b_z, small, conv_w, conv_b.reshape(1, SSD_CONV_DIM), alog_row, dtb_row, d_row,
      norm_w.reshape(1, SSD_INNER))


ATT_BAND = 128
ATT_MASKED = -1e30
assert all(w // d == ATT_BAND for w, d in ATT_GROUPS)


def _alibi_slopes_np():
    idx = np.arange(1, ATT_HEADS + 1, dtype=np.float32)
    return np.exp2(-8.0 * idx / ATT_HEADS).astype(np.float32).reshape(N_ATT_GROUPS, ATT_HEADS_PER_GROUP)


def _attn_bias(gi, first):
    nk = 2 * ATT_BAND
    i = lax.broadcasted_iota(jnp.int32, (ATT_BAND, nk), 0)
    j = lax.broadcasted_iota(jnp.int32, (ATT_BAND, nk), 1)
    du = i + ATT_BAND - j
    duf = du.astype(F32)
    dil = ATT_GROUPS[gi][1]
    slopes = _alibi_slopes_np()[gi]
    if first:
        inside = jnp.minimum(du, j - ATT_BAND) >= 0
    else:
        inside = jnp.abs(2 * du - ATT_BAND) <= ATT_BAND
    return [jnp.where(inside, duf * float(-slopes[hh] * dil * LOG2E), ATT_MASKED)
            for hh in range(ATT_HEADS_PER_GROUP)]


ATT_LANE_CHUNKS = ATT_OUT // 128
ATT_HEADS_PER_CHUNK = 128 // ATT_HEAD_DIM
ATT_UNROLL = (5, 4, 4)
LOG2E = math.log2(math.e)
LN2 = math.log(2.0)


def _attn_block(q, keys, vals, bias):
    nk = keys.shape[0]
    low = lax.broadcasted_iota(jnp.int32, (ATT_BAND, 128), 1) < ATT_HEAD_DIM
    q2 = q * LOG2E
    kb = keys.astype(BF16)
    va = jnp.concatenate([vals.astype(BF16), jnp.ones((nk, 128), BF16)], axis=1)
    res = []
    for hh in range(ATT_HEADS_PER_CHUNK):
        qh = jnp.where(low if hh == 0 else jnp.logical_not(low), q2, 0.0).astype(BF16)
        s = lax.dot_general(qh, kb, (((1,), (1,)), ((), ())), preferred_element_type=F32) + bias[hh]
        m = jnp.max(s, axis=-1, keepdims=True)
        p = jnp.exp2(s - m)
        oa = jnp.dot(p.astype(BF16), va, preferred_element_type=F32)
        l = oa[:, 128:]
        res.append((oa[:, :128] * (1.0 / l), (m + jnp.log2(l)) * LN2))
    return jnp.where(low, res[0][0], res[1][0]), jnp.where(low, res[0][1], res[1][1])


def _attn_prompt_kernel(q_ref, kv_ref, o_ref, lse_ref):
    g = pl.program_id(0)
    L = q_ref.shape[2]
    nc = ATT_LANE_CHUNKS

    for gi, (win, dil) in enumerate(ATT_GROUPS):
        @pl.when(g == gi)
        def _(gi=gi, win=win, dil=dil):
            nblk = L // win

            def rows(start):
                return pl.ds(start, ATT_BAND, stride=dil) if dil > 1 else pl.ds(start, ATT_BAND)

            def unit(cur, prev, bias):
                for c in range(nc):
                    hs = slice(c * ATT_HEADS_PER_CHUNK, (c + 1) * ATT_HEADS_PER_CHUNK)
                    keys = jnp.concatenate([kv_ref[0, c, prev, :], kv_ref[0, c, cur, :]], axis=0)
                    vals = jnp.concatenate([kv_ref[0, nc + c, prev, :], kv_ref[0, nc + c, cur, :]], axis=0)
                    o, lse = _attn_block(q_ref[0, c, cur, :], keys, vals, bias[hs])
                    o_ref[0, c, cur, :] = o
                    lse_ref[0, c, cur, :] = lse

            bias0 = _attn_bias(gi, True)

            def body0(r, carry):
                unit(rows(r), rows(r), bias0)
                return carry

            lax.fori_loop(0, dil, body0, 0, unroll=min(dil, ATT_UNROLL[gi]))
            if nblk > 1:
                bias1 = _attn_bias(gi, False)

                def body(u, carry):
                    start = (u // dil + 1) * win + u % dil
                    unit(rows(start), rows(start - win), bias1)
                    return carry

                lax.fori_loop(0, (nblk - 1) * dil, body, 0, unroll=ATT_UNROLL[gi])


def attn_prompt(q5, kv5, bsz):
    ng, nc, _, lanes = q5.shape
    L = SEQ
    assert all(L % w == 0 for w, _ in ATT_GROUPS)
    blk = lambda n: pl.BlockSpec((1, n, L, lanes), lambda g, b: (g, 0, b, 0))
    out = jax.ShapeDtypeStruct((ng, nc, bsz * L, lanes), F32)
    return pl.pallas_call(
        _attn_prompt_kernel,
        out_shape=(out, out),
        grid=(ng, bsz),
        in_specs=[blk(nc), blk(2 * nc)],
        out_specs=(blk(nc), blk(nc)),
        compiler_params=pltpu.CompilerParams(
            dimension_semantics=("arbitrary", "arbitrary"), vmem_limit_bytes=VMEM_LIMIT_BYTES),
        name="attn_prompt",
    )(q5, kv5)


SMALL_LANES = 128
_OFF = np.cumsum((0,) + IN_SPLITS).tolist()
PROJ_AB_WIDTHS = (GDN_CONV_DIM, GDN_VW, SSD_INNER, SSD_CONV_DIM, SMALL_LANES)
PROJ_AB_N = sum(PROJ_AB_WIDTHS)
PROJ_C_N = 3 * ATT_HEADS * ATT_HEAD_DIM
ATT_Q_N = ATT_HEADS * ATT_HEAD_DIM


def prep_weights(w_in):
    o = _OFF
    pad = jnp.zeros(w_in.shape[:-1] + (SMALL_LANES - 2 * GDN_HEADS - SSD_HEADS,), w_in.dtype)
    w_ab = jnp.concatenate([w_in[..., o[0]:o[1]], w_in[..., o[1]:o[2]], w_in[..., o[4]:o[5]], w_in[..., o[5]:o[6]],
                            w_in[..., o[2]:o[4]], w_in[..., o[6]:o[7]], pad], axis=-1).astype(BF16)
    w_c = w_in[..., o[7]:o[8]].astype(BF16)
    w_g = w_in[..., o[8]:o[9]].astype(BF16)
    return w_ab, w_c, w_g


def _rmsnorm_rows(h, w_row):
    return h * lax.rsqrt(jnp.mean(h * h, axis=-1, keepdims=True) + RMS_EPS) * w_row


def _proj_ab_kernel(h_ref, nw_ref, w_ref, qkv_ref, gate_ref, z_ref, xbc_ref, small_ref):
    u = _rmsnorm_rows(h_ref[...], nw_ref[...]).astype(BF16)
    y = jnp.dot(u, w_ref[...], preferred_element_type=F32)
    lo = 0
    for ref, wd in zip((qkv_ref, gate_ref, z_ref, xbc_ref, small_ref), PROJ_AB_WIDTHS):
        ref[...] = y[:, lo:lo + wd]
        lo += wd


def proj_ab(h, norm_w, w_ab, tm):
    t = h.shape[0]
    assert t % tm == 0
    const = dict(pipeline_mode=pl.Buffered(1))
    return pl.pallas_call(
        _proj_ab_kernel,
        out_shape=tuple(jax.ShapeDtypeStruct((t, wd), F32) for wd in PROJ_AB_WIDTHS),
        grid=(t // tm,),
        in_specs=[
            pl.BlockSpec((tm, D_MODEL), lambda i: (i, 0)),
            pl.BlockSpec((1, D_MODEL), lambda i: (0, 0), **const),
            pl.BlockSpec((D_MODEL, PROJ_AB_N), lambda i: (0, 0), **const),
        ],
        out_specs=tuple(pl.BlockSpec((tm, wd), lambda i: (i, 0)) for wd in PROJ_AB_WIDTHS),
        compiler_params=pltpu.CompilerParams(
            dimension_semantics=("arbitrary",), vmem_limit_bytes=VMEM_LIMIT_BYTES),
        name="proj_ab",
    )(h, norm_w.reshape(1, D_MODEL), w_ab)


def _head_rmsnorm(x, w_row, seg):
    outs = []
    for c in range(x.shape[-1] // 128):
        xc = x[:, c * 128:(c + 1) * 128]
        hi, lo = _split2(xc * xc)
        ss = jnp.dot(hi, seg, preferred_element_type=F32) + jnp.dot(lo, seg, preferred_element_type=F32)
        outs.append(xc * lax.rsqrt(ss * (1.0 / ATT_HEAD_DIM) + RMS_EPS) * w_row)
    return outs


def _proj_c_kernel(h_ref, nw_ref, w_ref, qw_ref, kw_ref, *out_refs, chunked):
    u = _rmsnorm_rows(h_ref[...], nw_ref[...]).astype(BF16)
    y = jnp.dot(u, w_ref[...], preferred_element_type=F32)
    r = lax.broadcasted_iota(jnp.int32, (128, 128), 0) // ATT_HEAD_DIM
    c = lax.broadcasted_iota(jnp.int32, (128, 128), 1) // ATT_HEAD_DIM
    seg = (r == c).astype(BF16)
    qn = [t * ATT_HEAD_DIM ** -0.5 for t in _head_rmsnorm(y[:, :ATT_Q_N], qw_ref[...], seg)]
    kn = _head_rmsnorm(y[:, ATT_Q_N:2 * ATT_Q_N], kw_ref[...], seg)
    v = y[:, 2 * ATT_Q_N:]
    nc = ATT_LANE_CHUNKS
    if chunked:
        q5_ref, kv5_ref, kvrows_ref = out_refs
    else:
        q_ref, kvrows_ref = out_refs
    for g in range(N_ATT_GROUPS):
        for cc in range(nc):
            i = g * nc + cc
            vc = v[:, i * 128:(i + 1) * 128]
            kvrows_ref[g, :, cc * 128:(cc + 1) * 128] = kn[i]
            kvrows_ref[g, :, ATT_OUT + cc * 128:ATT_OUT + (cc + 1) * 128] = vc
            if chunked:
                q5_ref[g, cc] = qn[i]
                kv5_ref[g, cc] = kn[i]
                kv5_ref[g, nc + cc] = vc
            else:
                q_ref[:, i * 128:(i + 1) * 128] = qn[i]


def proj_c(h, norm_w, w_c, q_norm, k_norm, tm, chunked):
    t = h.shape[0]
    assert t % tm == 0
    const = dict(pipeline_mode=pl.Buffered(1))
    ng, nc = N_ATT_GROUPS, ATT_LANE_CHUNKS
    rows_shape = jax.ShapeDtypeStruct((ng, t, 2 * ATT_OUT), F32)
    rows_spec = pl.BlockSpec((ng, tm, 2 * ATT_OUT), lambda i: (0, i, 0))
    if chunked:
        out_shape = (jax.ShapeDtypeStruct((ng, nc, t, 128), F32), jax.ShapeDtypeStruct((ng, 2 * nc, t, 128), F32),
                     rows_shape)
        out_specs = (pl.BlockSpec((ng, nc, tm, 128), lambda i: (0, 0, i, 0)),
                     pl.BlockSpec((ng, 2 * nc, tm, 128), lambda i: (0, 0, i, 0)), rows_spec)
    else:
        out_shape = (jax.ShapeDtypeStruct((t, ATT_Q_N), F32), rows_shape)
        out_specs = (pl.BlockSpec((tm, ATT_Q_N), lambda i: (i, 0)), rows_spec)
    tile2 = lambda w: jnp.tile(w, 128 // ATT_HEAD_DIM).reshape(1, 128)
    return pl.pallas_call(
        functools.partial(_proj_c_kernel, chunked=chunked),
        out_shape=out_shape,
        grid=(t // tm,),
        in_specs=[
            pl.BlockSpec((tm, D_MODEL), lambda i: (i, 0)),
            pl.BlockSpec((1, D_MODEL), lambda i: (0, 0), **const),
            pl.BlockSpec((D_MODEL, PROJ_C_N), lambda i: (0, 0), **const),
            pl.BlockSpec((1, 128), lambda i: (0, 0), **const),
            pl.BlockSpec((1, 128), lambda i: (0, 0), **const),
        ],
        out_specs=out_specs,
        compiler_params=pltpu.CompilerParams(
            dimension_semantics=("arbitrary",), vmem_limit_bytes=VMEM_LIMIT_BYTES),
        name="proj_c",
    )(h, norm_w.reshape(1, D_MODEL), w_c, tile2(q_norm), tile2(k_norm))


def _merge_kernel(h_ref, oa_ref, ob_ref, *refs, chunked):
    if chunked:
        o5_ref, lse5_ref = refs[:2]
        refs = refs[2:]
        ocs = []
        for cc in range(ATT_LANE_CHUNKS):
            ls = [lse5_ref[g, cc] for g in range(N_ATT_GROUPS)]
            m = functools.reduce(jnp.maximum, ls)
            es = [jnp.exp(l - m) for l in ls]
            num = sum(e * o5_ref[g, cc] for g, e in enumerate(es))
            ocs.append(num * (1.0 / sum(es)))
        o_c = jnp.concatenate(ocs, axis=-1)
    else:
        o_c = refs[0][...]
        refs = refs[1:]
    nw_ref, wg_ref, gb_ref, wa_ref, wb_ref, wc_ref, wo_ref, out_ref = refs
    h = h_ref[...]
    u = _rmsnorm_rows(h, nw_ref[...]).astype(BF16)
    merged = None
    for i, (o, w_ref) in enumerate(((oa_ref[...], wa_ref), (ob_ref[...], wb_ref), (o_c, wc_ref))):
        sl = slice(i * D_MODEL, (i + 1) * D_MODEL)
        gate = jax.nn.sigmoid(jnp.dot(u, wg_ref[:, sl], preferred_element_type=F32) + gb_ref[:, sl])
        term = gate * jnp.dot(o.astype(BF16), w_ref[...], preferred_element_type=F32)
        merged = term if merged is None else merged + term
    out_ref[...] = h + jnp.dot(merged.astype(BF16), wo_ref[...], preferred_element_type=F32)


def merge_block(h, o_a, o_b, o_att, norm_w, w_g, gate_b, w_br_a, w_br_b, w_br_c, w_out, tm):
    t = h.shape[0]
    assert t % tm == 0
    chunked = isinstance(o_att, tuple)
    const = dict(pipeline_mode=pl.Buffered(1))
    full = lambda a: pl.BlockSpec(a.shape, lambda i: (0,) * a.ndim, **const)
    rows = lambda wd: pl.BlockSpec((tm, wd), lambda i: (i, 0))
    if chunked:
        att_specs = [pl.BlockSpec((N_ATT_GROUPS, ATT_LANE_CHUNKS, tm, 128), lambda i: (0, 0, i, 0))] * 2
        att_args = list(o_att)
    else:
        att_specs = [rows(ATT_OUT)]
        att_args = [o_att]
    params = [norm_w.reshape(1, D_MODEL), w_g, gate_b.reshape(1, N_BRANCH * D_MODEL), w_br_a, w_br_b, w_br_c, w_out]
    return pl.pallas_call(
        functools.partial(_merge_kernel, chunked=chunked),
        out_shape=jax.ShapeDtypeStruct(h.shape, F32),
        grid=(t // tm,),
        in_specs=[rows(D_MODEL), rows(GDN_VW), rows(SSD_INNER)] + att_specs + [full(a) for a in params],
        out_specs=rows(D_MODEL),
        compiler_params=pltpu.CompilerParams(
            dimension_semantics=("arbitrary",), vmem_limit_bytes=VMEM_LIMIT_BYTES),
        name="merge",
    )(h, o_a, o_b, *att_args, *params)


DEC_TB = 8


def _conv_step(cst_ref, x, w_ref):
    prev = [cst_ref[0, i] for i in range(CONV_WIDTH - 1)]
    y = x * w_ref[CONV_WIDTH - 1:CONV_WIDTH, :]
    for i, pr in enumerate(prev):
        y = y + pr * w_ref[i:i + 1, :]
    return y, prev[1:] + [x]


def _gdn_decode_kernel(qkv_ref, gate_ref, small_ref, cst_ref, s_ref, carried_ref, convw_ref, alog_ref, dtb_ref,
                       normw_ref, o_ref, so_ref, csto_ref):
    del carried_ref
    tb = qkv_ref.shape[0]
    x = qkv_ref[...]
    y, new_cst = _conv_step(cst_ref, x, convw_ref)
    for i, r in enumerate(new_cst):
        csto_ref[i] = r
    y = _silu(y)
    small = small_ref[...]
    beta_all = jax.nn.sigmoid(small)
    eg_all = jnp.exp(-jnp.exp(alog_ref[...]) * _softplus(small + dtb_ref[...]))
    lanes = small.shape[-1]
    r = lax.broadcasted_iota(jnp.int32, (lanes, GDN_HEADS * GDN_DV), 0)
    c = lax.broadcasted_iota(jnp.int32, (lanes, GDN_HEADS * GDN_DV), 1)
    sel = (r == GDN_HEADS + c // GDN_DV).astype(BF16)
    eg_wide = sum(jnp.dot(t, sel, preferred_element_type=F32) for t in _split3(eg_all))
    tok = lax.broadcasted_iota(jnp.int32, (tb, 1), 0)
    tok2 = lax.broadcasted_iota(jnp.int32, (2 * tb, 1), 0)
    for h in range(GDN_HEADS):
        lo = h * GDN_DK
        q = y[:, lo:lo + GDN_DK]
        k = y[:, GDN_QK + lo:GDN_QK + lo + GDN_DK]
        v = y[:, 2 * GDN_QK + h * GDN_DV:2 * GDN_QK + (h + 1) * GDN_DV]
        q = q * (lax.rsqrt(jnp.sum(q * q, axis=-1, keepdims=True) + RMS_EPS) * GDN_DK ** -0.5)
        k = k * lax.rsqrt(jnp.sum(k * k, axis=-1, keepdims=True) + RMS_EPS)
        beta = beta_all[:, h:h + 1]
        eg = eg_all[:, GDN_HEADS + h:GDN_HEADS + h + 1]
        qk = jnp.sum(q * k, axis=-1, keepdims=True)
        kq = jnp.concatenate([k, q], axis=0).astype(BF16)
        kq_s = jnp.zeros((2 * tb, GDN_DV), F32)
        for t in range(tb):
            r = jnp.dot(kq, s_ref[0, t, h].astype(BF16), preferred_element_type=F32)
            kq_s = jnp.where(tok2 % tb == t, r, kq_s)
        u = beta * (v - eg * kq_s[:tb])
        o = eg * kq_s[tb:] + qk * u
        kb = k.astype(BF16)
        for t in range(tb):
            outer = lax.dot_general(kb, jnp.where(tok == t, u, 0.0).astype(BF16), (((0,), (0,)), ((), ())),
                                    preferred_element_type=F32)
            so_ref[0, t, h] = s_ref[0, t, h] * eg_wide[t:t + 1, h * GDN_DV:(h + 1) * GDN_DV] + outer
        o = o * lax.rsqrt(jnp.mean(o * o, axis=-1, keepdims=True) + RMS_EPS) * normw_ref[...]
        o_ref[:, h * GDN_DV:(h + 1) * GDN_DV] = o * _silu(gate_ref[:, h * GDN_DV:(h + 1) * GDN_DV])


def _carried(new_states, shape, operand_index):
    if new_states is None:
        return jnp.zeros((8, 128), F32), {}
    assert new_states.shape == shape
    return new_states, {operand_index: 1}


def gdn_decode(a_qkv, a_gate, small, conv_state_t, state, new_states, layer, conv_w, a_log, dt_bias, norm_w):
    bsz = a_qkv.shape[0]
    prev, aliases = _carried(new_states, state.shape, 5)
    tb = DEC_TB
    lanes = small.shape[-1]
    alog_row = jnp.zeros((1, lanes), F32).at[0, GDN_HEADS:2 * GDN_HEADS].set(a_log)
    dtb_row = jnp.zeros((1, lanes), F32).at[0, GDN_HEADS:2 * GDN_HEADS].set(dt_bias)
    nprev = CONV_WIDTH - 1
    rows = lambda wd: pl.BlockSpec((tb, wd), lambda i: (i, 0))
    const2 = lambda i: (0, 0)
    return pl.pallas_call(
        _gdn_decode_kernel,
        out_shape=(jax.ShapeDtypeStruct((bsz, GDN_VW), F32),
                   jax.ShapeDtypeStruct(state.shape, F32),
                   jax.ShapeDtypeStruct((nprev, bsz, GDN_CONV_DIM), F32)),
        grid=(bsz // tb,),
        in_specs=[rows(GDN_CONV_DIM), rows(GDN_VW), rows(lanes),
                  pl.BlockSpec((1, nprev, tb, GDN_CONV_DIM), lambda i: (layer, 0, i, 0)),
                  pl.BlockSpec((1, tb, GDN_HEADS, GDN_DK, GDN_DV), lambda i: (layer, i, 0, 0, 0)),
                  pl.BlockSpec(memory_space=pl.ANY),
                  pl.BlockSpec((CONV_WIDTH, GDN_CONV_DIM), const2),
                  pl.BlockSpec((1, lanes), const2), pl.BlockSpec((1, lanes), const2),
                  pl.BlockSpec((1, GDN_DV), const2)],
        out_specs=(rows(GDN_VW),
                   pl.BlockSpec((1, tb, GDN_HEADS, GDN_DK, GDN_DV), lambda i: (layer, i, 0, 0, 0)),
                   pl.BlockSpec((nprev, tb, GDN_CONV_DIM), lambda i: (0, i, 0))),
        input_output_aliases=aliases,
        compiler_params=pltpu.CompilerParams(
            dimension_semantics=("arbitrary",), vmem_limit_bytes=VMEM_LIMIT_BYTES),
        name="gdn_decode",
    )(a_qkv, a_gate, small, conv_state_t, state, prev, conv_w, alog_row, dtb_row, norm_w.reshape(1, GDN_DV))


def _ssd_decode_kernel(xbc_ref, z_ref, small_ref, cst_ref, h_ref, carried_ref, convw_ref, convb_ref, alog_ref,
                       dtb_ref, d_ref, normw_ref, o_ref, ho_ref, csto_ref):
    del carried_ref
    tb = xbc_ref.shape[0]
    x = xbc_ref[...]
    y, new_cst = _conv_step(cst_ref, x, convw_ref)
    for i, r in enumerate(new_cst):
        csto_ref[i] = r
    y = _silu(y + convb_ref[...])
    xs = y[:, :SSD_INNER]
    bm = y[:, SSD_INNER:SSD_INNER + SSD_BC]
    cm = y[:, SSD_INNER + SSD_BC:]
    dt_all = _softplus(small_ref[...] + dtb_ref[...])
    eda_all = jnp.exp(dt_all * (-jnp.exp(alog_ref[...])))
    lanes = dt_all.shape[-1]

    def spread(v, width):
        r = lax.broadcasted_iota(jnp.int32, (lanes, SSD_HEADS * width), 0)
        c = lax.broadcasted_iota(jnp.int32, (lanes, SSD_HEADS * width), 1)
        sel = (r == SSD_DT_LANE + c // width).astype(BF16)
        return sum(jnp.dot(t, sel, preferred_element_type=F32) for t in _split3(v))

    eda_wide = spread(eda_all, SSD_STATE)
    xd = xs * spread(dt_all, SSD_HEADDIM)
    tok = lax.broadcasted_iota(jnp.int32, (tb, 1), 0)
    gp = SSD_HPG * SSD_HEADDIM
    ys = []
    for g in range(SSD_GROUPS):
        bg = bm[:, g * SSD_STATE:(g + 1) * SSD_STATE]
        cg = cm[:, g * SSD_STATE:(g + 1) * SSD_STATE].astype(BF16)
        xd_g = xd[:, g * gp:(g + 1) * gp].astype(BF16)
        y_g = jnp.zeros((tb, gp), F32)
        for t in range(tb):
            outer = lax.dot_general(xd_g, jnp.where(tok == t, bg, 0.0).astype(BF16), (((0,), (0,)), ((), ())),
                                    preferred_element_type=F32)
            hns = []
            for hh in range(SSD_HPG):
                h = g * SSD_HPG + hh
                hn = (h_ref[0, t, h] * eda_wide[t:t + 1, h * SSD_STATE:(h + 1) * SSD_STATE]
                      + outer[hh * SSD_HEADDIM:(hh + 1) * SSD_HEADDIM])
                ho_ref[0, t, h] = hn
                hns.append(hn.astype(BF16))
            y_t = lax.dot_general(cg, jnp.concatenate(hns, axis=0), (((1,), (1,)), ((), ())),
                                  preferred_element_type=F32)
            y_g = jnp.where(tok == t, y_t, y_g)
        ys.append(y_g)
    yf = jnp.concatenate(ys, axis=-1) + d_ref[...] * xs
    yf = yf * _silu(z_ref[...])
    gw = SSD_INNER // SSD_GROUPS
    for g in range(SSD_GROUPS):
        seg = yf[:, g * gw:(g + 1) * gw]
        o_ref[:, g * gw:(g + 1) * gw] = (seg * lax.rsqrt(jnp.mean(seg * seg, axis=-1, keepdims=True) + RMS_EPS)
                                         * normw_ref[:, g * gw:(g + 1) * gw])


def ssd_decode(b_xbc, b_z, small, conv_state_t, state, new_states, layer, conv_w, conv_b, a_log, dt_bias, d_skip,
               norm_w):
    bsz = b_xbc.shape[0]
    prev, aliases = _carried(new_states, state.shape, 5)
    tb = DEC_TB
    lanes = small.shape[-1]
    sl = slice(SSD_DT_LANE, SSD_DT_LANE + SSD_HEADS)
    alog_row = jnp.zeros((1, lanes), F32).at[0, sl].set(a_log)
    dtb_row = jnp.zeros((1, lanes), F32).at[0, sl].set(dt_bias)
    d_row = jnp.repeat(d_skip, SSD_HEADDIM).reshape(1, SSD_INNER)
    nprev = CONV_WIDTH - 1
    rows = lambda wd: pl.BlockSpec((tb, wd), lambda i: (i, 0))
    const2 = lambda i: (0, 0)
    return pl.pallas_call(
        _ssd_decode_kernel,
        out_shape=(jax.ShapeDtypeStruct((bsz, SSD_INNER), F32),
                   jax.ShapeDtypeStruct(state.shape, F32),
                   jax.ShapeDtypeStruct((nprev, bsz, SSD_CONV_DIM), F32)),
        grid=(bsz // tb,),
        in_specs=[rows(SSD_CONV_DIM), rows(SSD_INNER), rows(lanes),
                  pl.BlockSpec((1, nprev, tb, SSD_CONV_DIM), lambda i: (layer, 0, i, 0)),
                  pl.BlockSpec((1, tb, SSD_HEADS, SSD_HEADDIM, SSD_STATE), lambda i: (layer, i, 0, 0, 0)),
                  pl.BlockSpec(memory_space=pl.ANY),
                  pl.BlockSpec((CONV_WIDTH, SSD_CONV_DIM), const2), pl.BlockSpec((1, SSD_CONV_DIM), const2),
                  pl.BlockSpec((1, lanes), const2), pl.BlockSpec((1, lanes), const2),
                  pl.BlockSpec((1, SSD_INNER), const2), pl.BlockSpec((1, SSD_INNER), const2)],
        out_specs=(rows(SSD_INNER),
                   pl.BlockSpec((1, tb, SSD_HEADS, SSD_HEADDIM, SSD_STATE), lambda i: (layer, i, 0, 0, 0)),
                   pl.BlockSpec((nprev, tb, SSD_CONV_DIM), lambda i: (0, i, 0))),
        input_output_aliases=aliases,
        compiler_params=pltpu.CompilerParams(
            dimension_semantics=("arbitrary",), vmem_limit_bytes=VMEM_LIMIT_BYTES),
        name="ssd_decode",
    )(b_xbc, b_z, small, conv_state_t, state, prev, conv_w, conv_b.reshape(1, SSD_CONV_DIM),
      alog_row, dtb_row, d_row, norm_w.reshape(1, SSD_INNER))


ATT_DEC_TB = 8
ATT_DEC_TOK = 2


def _attn_decode_kernel(q_ref, kvn_ref, c0_ref, c1_ref, c2_ref, o_ref, acc_ref):
    tb = q_ref.shape[0]
    ntok = c0_ref.shape[1]
    hd = ATT_HEAD_DIM
    k = pl.program_id(1)
    slopes = _alibi_slopes_np()
    q_t = q_ref[...].T
    kvn_t = [kvn_ref[gi].T for gi in range(N_ATT_GROUPS)]
    lane_t = lax.broadcasted_iota(jnp.int32, (1, tb), 1)

    def column(x_t, tok):
        return jnp.sum(jnp.where(lane_t == tok, x_t, 0.0), axis=1, keepdims=True)

    @pl.when(k == 0)
    def _():
        acc_ref[...] = jnp.zeros_like(acc_ref)

    acc = acc_ref[...]
    lane = lax.broadcasted_iota(jnp.int32, acc_ref.shape, 1)
    for tt in range(ntok):
        tok = k * ntok + tt
        q_col = column(q_t, tok)
        og = [[None] * N_ATT_GROUPS for _ in range(ATT_HEADS_PER_GROUP)]
        lg = [[None] * N_ATT_GROUPS for _ in range(ATT_HEADS_PER_GROUP)]
        for gi, (c_ref, (win, dil)) in enumerate(zip((c0_ref, c1_ref, c2_ref), ATT_GROUPS)):
            kvn_col = column(kvn_t[gi], tok)
            j = lax.broadcasted_iota(jnp.int32, (1, win), 1)
            dist = win - j
            is_key = (dist % dil) == 0
            for hh in range(ATT_HEADS_PER_GROUP):
                lo = gi * ATT_OUT + hh * hd
                qc = q_col[lo:lo + hd]
                kn = kvn_col[hh * hd:(hh + 1) * hd]
                vn = kvn_col[ATT_OUT + hh * hd:ATT_OUT + (hh + 1) * hd]
                bias = jnp.where(is_key, dist.astype(F32) * float(-slopes[gi, hh]), ATT_MASKED)
                s = jnp.sum(c_ref[0, tt, 0, hh] * qc, axis=0, keepdims=True) + bias
                s_new = jnp.sum(qc * kn, axis=0, keepdims=True)
                m = jnp.maximum(jnp.max(s, axis=-1, keepdims=True), s_new)
                p = jnp.exp(s - m)
                p_new = jnp.exp(s_new - m)
                l = jnp.sum(p, axis=-1, keepdims=True) + p_new
                og[hh][gi] = (jnp.sum(c_ref[0, tt, 1, hh] * p, axis=-1, keepdims=True) + p_new * vn) * (1.0 / l)
                lg[hh][gi] = m + jnp.log(l)
        heads = []
        for hh in range(ATT_HEADS_PER_GROUP):
            m = functools.reduce(jnp.maximum, lg[hh])
            es = [jnp.exp(l - m) for l in lg[hh]]
            heads.append(sum(e * o for e, o in zip(es, og[hh])) * (1.0 / sum(es)))
        acc = jnp.where(lane == tok, jnp.concatenate(heads, axis=0), acc)
    acc_ref[...] = acc

    @pl.when(k == tb // ntok - 1)
    def _():
        o_ref[...] = acc.T[:tb, :]


def attn_decode(q_rows, kv_new, caches, layer):
    bsz = q_rows.shape[0]
    tb, ntok = ATT_DEC_TB, ATT_DEC_TOK
    views, specs = [], []
    for cache, (win, dil) in zip(caches, ATT_GROUPS):
        assert cache.shape[2] == win and win % dil == 0
        views.append(jnp.transpose(cache, (0, 1, 3, 4, 5, 2)))
        specs.append(pl.BlockSpec((1, ntok) + KV_TAIL + (win,),
                                  lambda i, k: (layer, i * (tb // ntok) + k, 0, 0, 0, 0)))
    return pl.pallas_call(
        _attn_decode_kernel,
        out_shape=jax.ShapeDtypeStruct((bsz, ATT_OUT), F32),
        grid=(bsz // tb, tb // ntok),
        in_specs=[pl.BlockSpec((tb, ATT_Q_N), lambda i, k: (i, 0)),
                  pl.BlockSpec((N_ATT_GROUPS, tb, 2 * ATT_OUT), lambda i, k: (0, i, 0))] + specs,
        out_specs=pl.BlockSpec((tb, ATT_OUT), lambda i, k: (i, 0)),
        scratch_shapes=[pltpu.VMEM((ATT_OUT, 128), F32)],
        compiler_params=pltpu.CompilerParams(
            dimension_semantics=("arbitrary", "arbitrary"), vmem_limit_bytes=VMEM_LIMIT_BYTES),
        name="attn_decode",
    )(q_rows, kv_new, *views)


PROMPT_TM = 512
PROMPT_FFN_TM = 256
KV_TAIL = (2, ATT_HEADS_PER_GROUP, ATT_HEAD_DIM)


def prompt_trunk(x, p):
    bsz, L, _ = x.shape
    t = bsz * L
    h = x.reshape(t, D_MODEL)
    new = [[] for _ in range(4 + N_ATT_GROUPS)]
    seq = lambda a: a.reshape(bsz, L, a.shape[-1])
    for l in range(DEPTH):
        h = ffn_block(h, p['ffn1_norm'][l], p['ffn1_w_in'][l], p['ffn1_w_out'][l], PROMPT_FFN_TM)
        qkv, gate, z, xbc, small = proj_ab(h, p['mix_norm'][l], p['w_ab'][l], PROMPT_TM)
        q5, kv5, kvrows = proj_c(h, p['mix_norm'][l], p['w_c'][l], p['q_norm'][l], p['k_norm'][l], PROMPT_TM, True)
        o_a, s_gdn, c_gdn = gdn_prompt(seq(qkv), seq(gate), seq(small), p['gdn_conv_w'][l], p['gdn_a_log'][l],
                                       p['gdn_dt_bias'][l], p['gdn_norm_w'][l])
        o_b, s_ssd, c_ssd = ssd_prompt(seq(xbc), seq(z), seq(small), p['ssd_conv_w'][l], p['ssd_conv_b'][l],
                                       p['ssd_a_log'][l], p['ssd_dt_bias'][l], p['ssd_d'][l], p['ssd_norm_w'][l])
        o5, lse5 = attn_prompt(q5, kv5, bsz)
        h = merge_block(h, o_a.reshape(t, GDN_VW), o_b.reshape(t, SSD_INNER), (o5, lse5), p['mix_norm'][l],
                        p['w_g'][l], p['gate_b'][l], p['w_br_a'][l], p['w_br_b'][l], p['w_br_c'][l], p['w_out'][l],
                        PROMPT_TM)
        h = ffn_block(h, p['ffn2_norm'][l], p['ffn2_w_in'][l], p['ffn2_w_out'][l], PROMPT_FFN_TM)
        kv_new = [kvrows[gi].reshape(bsz, L, 2 * ATT_OUT)[:, L - min(win, L):].reshape((bsz, min(win, L)) + KV_TAIL)
                  for gi, (win, _) in enumerate(ATT_GROUPS)]
        for lst, s in zip(new, (s_gdn, c_gdn, s_ssd, c_ssd, *kv_new)):
            lst.append(s)
    return h.reshape(bsz, L, D_MODEL), [jnp.stack(lst, axis=0) for lst in new]


def sample_trunk(x, p, state_gdn, state_gdn_conv, state_ssd, state_ssd_conv, caches):
    bsz = x.shape[0]
    assert x.shape[1] == 1
    h = x.reshape(bsz, D_MODEL)
    tm = bsz
    new = [[] for _ in range(2 + N_ATT_GROUPS)]
    gdn_conv_t = jnp.transpose(state_gdn_conv, (0, 2, 1, 3))
    ssd_conv_t = jnp.transpose(state_ssd_conv, (0, 2, 1, 3))
    s_gdn = s_ssd = None
    for l in range(DEPTH):
        h = ffn_block(h, p['ffn1_norm'][l], p['ffn1_w_in'][l], p['ffn1_w_out'][l], tm)
        qkv, gate, z, xbc, small = proj_ab(h, p['mix_norm'][l], p['w_ab'][l], tm)
        q_rows, kvrows = proj_c(h, p['mix_norm'][l], p['w_c'][l], p['q_norm'][l], p['k_norm'][l], tm, False)
        o_a, s_gdn, c_gdn = gdn_decode(qkv, gate, small, gdn_conv_t, state_gdn, s_gdn, l, p['gdn_conv_w'][l],
                                       p['gdn_a_log'][l], p['gdn_dt_bias'][l], p['gdn_norm_w'][l])
        o_b, s_ssd, c_ssd = ssd_decode(xbc, z, small, ssd_conv_t, state_ssd, s_ssd, l, p['ssd_conv_w'][l],
                                       p['ssd_conv_b'][l], p['ssd_a_log'][l], p['ssd_dt_bias'][l], p['ssd_d'][l],
                                       p['ssd_norm_w'][l])
        o_c = attn_decode(q_rows, kvrows, caches, l)
        h = merge_block(h, o_a, o_b, o_c, p['mix_norm'][l], p['w_g'][l], p['gate_b'][l], p['w_br_a'][l],
                        p['w_br_b'][l], p['w_br_c'][l], p['w_out'][l], tm)
        h = ffn_block(h, p['ffn2_norm'][l], p['ffn2_w_in'][l], p['ffn2_w_out'][l], tm)
        kv_new = [kvrows[gi].reshape((bsz, 1) + KV_TAIL) for gi in range(N_ATT_GROUPS)]
        for lst, s in zip(new, (c_gdn, c_ssd, *kv_new)):
            lst.append(s)
    c_gdn, c_ssd, *kv_new = [jnp.stack(lst, axis=0) for lst in new]
    return h.reshape(bsz, 1, D_MODEL), [s_gdn, jnp.transpose(c_gdn, (0, 2, 1, 3)), s_ssd,
                                        jnp.transpose(c_ssd, (0, 2, 1, 3)), *kv_new]


def kernel(x_prompt, x_sample, state_gdn, state_gdn_conv, state_ssd, state_ssd_conv,
           cache_kv_w128, cache_kv_w512, cache_kv_w2048,
           ffn1_norm, ffn1_w_in, ffn1_w_out, mix_norm, w_in, gate_b,
           gdn_conv_w, gdn_a_log, gdn_dt_bias, gdn_norm_w,
           ssd_conv_w, ssd_conv_b, ssd_a_log, ssd_dt_bias, ssd_d, ssd_norm_w,
           q_norm, k_norm, w_br_a, w_br_b, w_br_c, w_out,
           ffn2_norm, ffn2_w_in, ffn2_w_out):
    w_ab, w_c, w_g = prep_weights(w_in)
    p = {'ffn1_norm': ffn1_norm, 'ffn1_w_in': ffn1_w_in.astype(BF16), 'ffn1_w_out': ffn1_w_out.astype(BF16),
         'mix_norm': mix_norm, 'w_ab': w_ab, 'w_c': w_c, 'w_g': w_g, 'gate_b': gate_b,
         'gdn_conv_w': gdn_conv_w, 'gdn_a_log': gdn_a_log, 'gdn_dt_bias': gdn_dt_bias, 'gdn_norm_w': gdn_norm_w,
         'ssd_conv_w': ssd_conv_w, 'ssd_conv_b': ssd_conv_b, 'ssd_a_log': ssd_a_log,
         'ssd_dt_bias': ssd_dt_bias, 'ssd_d': ssd_d, 'ssd_norm_w': ssd_norm_w,
         'q_norm': q_norm, 'k_norm': k_norm, 'w_br_a': w_br_a.astype(BF16), 'w_br_b': w_br_b.astype(BF16),
         'w_br_c': w_br_c.astype(BF16), 'w_out': w_out.astype(BF16),
         'ffn2_norm': ffn2_norm, 'ffn2_w_in': ffn2_w_in.astype(BF16), 'ffn2_w_out': ffn2_w_out.astype(BF16)}
    y_prompt, st_p = prompt_trunk(x_prompt, p)
    y_sample, st_s = sample_trunk(x_sample, p, state_gdn, state_gdn_conv, state_ssd, state_ssd_conv,
                                  [cache_kv_w128, cache_kv_w512, cache_kv_w2048])
    gdn_p, gdn_conv_p, ssd_p, ssd_conv_p, kv128_p, kv512_p, kv2048_p = st_p
    gdn_s, gdn_conv_s, ssd_s, ssd_conv_s, kv128_s, kv512_s, kv2048_s = st_s
    return (y_prompt, y_sample, gdn_p, gdn_s, gdn_conv_p, gdn_conv_s, ssd_p, ssd_s, ssd_conv_p, ssd_conv_s,
            kv128_p, kv128_s, kv512_p, kv512_s, kv2048_p, kv2048_s)
```

```python
import functools
import math
import jax, jax.numpy as jnp
from jax import lax
import numpy as np
from jax.experimental import pallas as pl
from jax.experimental.pallas import tpu as pltpu

D_MODEL = 1024
BATCH = 8
SEQ = 2048
DEPTH = 4
DEC_BATCH = 128
DEC_SEQ = 1
PAST_LEN = 2048

CONV_WIDTH = 4
RMS_EPS = 1e-6
GDN_HEADS = 4
GDN_DK = 128
GDN_DV = 128
GDN_CHUNK = 64
GDN_QK = GDN_HEADS * GDN_DK
GDN_VW = GDN_HEADS * GDN_DV
GDN_CONV_DIM = 2 * GDN_QK + GDN_VW
SSD_HEADS = 8
SSD_HEADDIM = 64
SSD_GROUPS = 2
SSD_STATE = 128
SSD_CHUNK = 128
SSD_INNER = SSD_HEADS * SSD_HEADDIM
SSD_CONV_DIM = SSD_INNER + 2 * SSD_GROUPS * SSD_STATE
ATT_GROUPS = ((128, 1), (512, 4), (2048, 16))
N_ATT_GROUPS = len(ATT_GROUPS)
ATT_HEADS_PER_GROUP = 4
ATT_HEAD_DIM = 64
ATT_HEADS = N_ATT_GROUPS * ATT_HEADS_PER_GROUP
ATT_OUT = ATT_HEADS_PER_GROUP * ATT_HEAD_DIM
D_FF = ((8 * D_MODEL // 3 + 127) // 128) * 128
N_BRANCH = 3
IN_SPLITS = (GDN_CONV_DIM, GDN_VW, GDN_HEADS, GDN_HEADS, SSD_INNER, SSD_CONV_DIM, SSD_HEADS,
             3 * ATT_HEADS * ATT_HEAD_DIM, N_BRANCH * D_MODEL)
D_IN = sum(IN_SPLITS)

F32 = jnp.float32
BF16 = jnp.bfloat16
VMEM_LIMIT_BYTES = 56 * 1024 * 1024


def _layer_spec(stacked, layer):
    zeros = (0,) * (stacked.ndim - 1)
    return pl.BlockSpec((1,) + stacked.shape[1:], lambda *_: (layer,) + zeros, pipeline_mode=pl.Buffered(1))


def _ffn_kernel(h_ref, nw_ref, wi_ref, wo_ref, o_ref):
    h = h_ref[...]
    xn = h * lax.rsqrt(jnp.mean(h * h, axis=-1, keepdims=True) + RMS_EPS) * nw_ref[0]
    gu = jnp.dot(xn.astype(BF16), wi_ref[0], preferred_element_type=F32)
    g, up = gu[:, :D_FF], gu[:, D_FF:]
    act = (g * jax.nn.sigmoid(g) * up).astype(BF16)
    y = jnp.dot(act, wo_ref[0], preferred_element_type=F32)
    o_ref[...] = h + 0.5 * y


def ffn_block(h, norm_w, w_in_bf16, w_out_bf16, layer, tm):
    t = h.shape[0]
    assert t % tm == 0
    return pl.pallas_call(
        _ffn_kernel,
        out_shape=jax.ShapeDtypeStruct(h.shape, h.dtype),
        grid=(t // tm,),
        in_specs=[pl.BlockSpec((tm, D_MODEL), lambda i: (i, 0)),
                  _layer_spec(norm_w, layer), _layer_spec(w_in_bf16, layer), _layer_spec(w_out_bf16, layer)],
        out_specs=pl.BlockSpec((tm, D_MODEL), lambda i: (i, 0)),
        compiler_params=pltpu.CompilerParams(
            dimension_semantics=("arbitrary",), vmem_limit_bytes=VMEM_LIMIT_BYTES),
        name="ffn",
    )(h, norm_w, w_in_bf16, w_out_bf16)


_DN_BMM = (((2,), (1,)), ((0,), (0,)))
_DN_BMM_NT = (((2,), (2,)), ((0,), (0,)))


def _bmm(a, b):
    return lax.dot_general(a.astype(BF16), b.astype(BF16), _DN_BMM, preferred_element_type=F32)


def _bmm_nt(a, b):
    return lax.dot_general(a.astype(BF16), b.astype(BF16), _DN_BMM_NT, preferred_element_type=F32)


def _split2(x):
    hi = x.astype(BF16)
    return hi, (x - hi.astype(F32)).astype(BF16)


def _split3(x):
    x1 = x.astype(BF16)
    r1 = x - x1.astype(F32)
    x2 = r1.astype(BF16)
    x3 = (r1 - x2.astype(F32)).astype(BF16)
    return x1, x2, x3


def _bmm_hi(a, b):
    a1, a2 = _split2(a)
    b1, b2 = _split2(b)
    dot = functools.partial(lax.dot_general, dimension_numbers=_DN_BMM, preferred_element_type=F32)
    return dot(a1, b1) + (dot(a1, b2) + dot(a2, b1))


def _cumsum_rows(tri, x):
    dot = functools.partial(lax.dot_general, dimension_numbers=_DN_BMM, preferred_element_type=F32)
    x1, x2, x3 = _split3(x)
    return dot(tri, x1) + (dot(tri, x2) + dot(tri, x3))


def _softplus(x):
    return jnp.maximum(x, 0.0) + jnp.log(1.0 + jnp.exp(-jnp.abs(x)))


def _silu(x):
    return x * jax.nn.sigmoid(x)


GDN_INV_BLOCK = 16


def _unit_lower_solve(a, rhs, row, col, mm):
    c = a.shape[-1]
    eye = (row == col).astype(F32)
    same_blk = (row // GDN_INV_BLOCK) == (col // GDN_INV_BLOCK)
    d = jnp.where(same_blk, a, 0.0)
    n = a - d
    x = eye - d
    pw = d
    k = 2
    while k < GDN_INV_BLOCK:
        pw = mm(pw, pw)
        x = x + mm(x, pw)
        k *= 2
    m = mm(x, n)
    y = mm(x, rhs)
    nblk = c // GDN_INV_BLOCK
    q = eye - m
    pw = m
    k = 2
    while k < nblk:
        pw = mm(pw, pw)
        q = q + mm(q, pw)
        k *= 2
    return mm(q, y)


def _gdn_prompt_kernel(qkv_ref, gate_ref, small_ref, convw_ref, alog_ref, dtb_ref, normw_ref,
                       o_ref, s_ref, cst_ref, xp_ref):
    c = GDN_CHUNK
    nb = qkv_ref.shape[0]

    @pl.when(pl.program_id(0) == 0)
    def _():
        s_ref[...] = jnp.zeros_like(s_ref)
        xp_ref[:, 0:8, :] = jnp.zeros((nb, 8, GDN_CONV_DIM), F32)

    x = qkv_ref[...]
    xp_ref[:, 8:8 + c, :] = x
    w = convw_ref[...]
    y = (xp_ref[:, 5:5 + c, :] * w[0:1, :] + xp_ref[:, 6:6 + c, :] * w[1:2, :]
         + xp_ref[:, 7:7 + c, :] * w[2:3, :] + x * w[3:4, :])
    xp_ref[:, 0:8, :] = xp_ref[:, c:c + 8, :]
    cst_ref[...] = x[:, c - (CONV_WIDTH - 1):, :]
    y = _silu(y)

    small = small_ref[...]
    beta_all = jax.nn.sigmoid(small)
    g_all = -jnp.exp(alog_ref[...]) * _softplus(small + dtb_ref[...])
    row = lax.broadcasted_iota(jnp.int32, (c, c), 0)
    col = lax.broadcasted_iota(jnp.int32, (c, c), 1)
    incl = row >= col
    strict = row > col
    tri = jnp.broadcast_to(incl.astype(BF16), (nb, c, c))
    gam_all = _cumsum_rows(tri, g_all)
    gam_all_t = jnp.swapaxes(gam_all, 1, 2)
    egam_all = jnp.exp(gam_all)

    for h in range(GDN_HEADS):
        lo = h * GDN_DK
        q = y[:, :, lo:lo + GDN_DK]
        k = y[:, :, GDN_QK + lo:GDN_QK + lo + GDN_DK]
        v = y[:, :, 2 * GDN_QK + h * GDN_DV:2 * GDN_QK + (h + 1) * GDN_DV]
        q = q * (lax.rsqrt(jnp.sum(q * q, axis=-1, keepdims=True) + RMS_EPS) * GDN_DK ** -0.5)
        k = k * lax.rsqrt(jnp.sum(k * k, axis=-1, keepdims=True) + RMS_EPS)
        beta = beta_all[:, :, h:h + 1]
        gam = gam_all[:, :, GDN_HEADS + h:GDN_HEADS + h + 1]
        gam_row = gam_all_t[:, GDN_HEADS + h:GDN_HEADS + h + 1, :]
        egam = egam_all[:, :, GDN_HEADS + h:GDN_HEADS + h + 1]
        gam_last = gam_row[:, :, c - 1:c]
        dec = jnp.where(incl, jnp.exp(jnp.minimum(gam - gam_row, 0.0)), 0.0)
        kk = _bmm_nt(k, k)
        a_mat = jnp.where(strict, beta * kk * dec, 0.0)
        rhs = jnp.concatenate([v * beta, k * (beta * egam)], axis=-1)
        sol = _unit_lower_solve(a_mat, rhs, row, col, _bmm)
        u0, wk = sol[:, :, :GDN_DV], sol[:, :, GDN_DV:]
        qk = _bmm_nt(q, k) * dec
        q_dec = q * egam
        k_dec = k * jnp.exp(gam_last - gam)
        s = s_ref[:, h]
        u = u0 - _bmm(wk, s)
        o = _bmm(q_dec, s) + _bmm(qk, u)
        s_ref[:, h] = s * jnp.exp(gam_last) + _bmm(jnp.swapaxes(k_dec, 1, 2), u)
        o = o * lax.rsqrt(jnp.mean(o * o, axis=-1, keepdims=True) + RMS_EPS) * normw_ref[...]
        o_ref[:, :, h * GDN_DV:(h + 1) * GDN_DV] = o * _silu(gate_ref[:, :, h * GDN_DV:(h + 1) * GDN_DV])


def gdn_prompt(a_qkv, a_gate, small, conv_w, a_log, dt_bias, norm_w):
    bsz, L, _ = a_qkv.shape
    c = GDN_CHUNK
    assert L % c == 0
    lanes = small.shape[-1]
    alog_row = jnp.zeros((1, lanes), F32).at[0, GDN_HEADS:2 * GDN_HEADS].set(a_log)
    dtb_row = jnp.zeros((1, lanes), F32).at[0, GDN_HEADS:2 * GDN_HEADS].set(dt_bias)
    const2 = lambda n: (0, 0)
    return pl.pallas_call(
        _gdn_prompt_kernel,
        out_shape=(jax.ShapeDtypeStruct((bsz, L, GDN_VW), F32),
                   jax.ShapeDtypeStruct((bsz, GDN_HEADS, GDN_DK, GDN_DV), F32),
                   jax.ShapeDtypeStruct((bsz, CONV_WIDTH - 1, GDN_CONV_DIM), F32)),
        grid=(L // c,),
        in_specs=[
            pl.BlockSpec((bsz, c, GDN_CONV_DIM), lambda n: (0, n, 0)),
            pl.BlockSpec((bsz, c, GDN_VW), lambda n: (0, n, 0)),
            pl.BlockSpec((bsz, c, lanes), lambda n: (0, n, 0)),
            pl.BlockSpec((CONV_WIDTH, GDN_CONV_DIM), const2),
            pl.BlockSpec((1, lanes), const2),
            pl.BlockSpec((1, lanes), const2),
            pl.BlockSpec((1, GDN_DV), const2),
        ],
        out_specs=(
            pl.BlockSpec((bsz, c, GDN_VW), lambda n: (0, n, 0)),
            pl.BlockSpec((bsz, GDN_HEADS, GDN_DK, GDN_DV), lambda n: (0, 0, 0, 0)),
            pl.BlockSpec((bsz, CONV_WIDTH - 1, GDN_CONV_DIM), lambda n: (0, 0, 0)),
        ),
        scratch_shapes=[pltpu.VMEM((bsz, c + 8, GDN_CONV_DIM), F32)],
        compiler_params=pltpu.CompilerParams(
            dimension_semantics=("arbitrary",), vmem_limit_bytes=VMEM_LIMIT_BYTES),
        name="gdn_prompt",
    )(a_qkv, a_gate, small, conv_w, alog_row, dtb_row, norm_w.reshape(1, GDN_DV))


SSD_DT_LANE = 2 * GDN_HEADS
SSD_BC = SSD_GROUPS * SSD_STATE
SSD_HPG = SSD_HEADS // SSD_GROUPS


def _ssd_prompt_kernel(xbc_ref, z_ref, small_ref, convw_ref, convb_ref, alog_ref, dtb_ref, d_ref, normw_ref,
                       o_ref, h_ref, cst_ref, xp_ref):
    c = SSD_CHUNK
    nb = xbc_ref.shape[0]

    @pl.when(pl.program_id(0) == 0)
    def _():
        h_ref[...] = jnp.zeros_like(h_ref)
        xp_ref[:, 0:8, :] = jnp.zeros((nb, 8, SSD_CONV_DIM), F32)

    x = xbc_ref[...]
    xp_ref[:, 8:8 + c, :] = x
    w = convw_ref[...]
    y = (xp_ref[:, 5:5 + c, :] * w[0:1, :] + xp_ref[:, 6:6 + c, :] * w[1:2, :]
         + xp_ref[:, 7:7 + c, :] * w[2:3, :] + x * w[3:4, :]) + convb_ref[...]
    xp_ref[:, 0:8, :] = xp_ref[:, c:c + 8, :]
    cst_ref[...] = x[:, c - (CONV_WIDTH - 1):, :]
    y = _silu(y)
    xs = y[:, :, :SSD_INNER]
    bm = y[:, :, SSD_INNER:SSD_INNER + SSD_BC]
    cm = y[:, :, SSD_INNER + SSD_BC:]

    dt_all = _softplus(small_ref[...] + dtb_ref[...])
    da_all = dt_all * (-jnp.exp(alog_ref[...]))
    row = lax.broadcasted_iota(jnp.int32, (c, c), 0)
    col = lax.broadcasted_iota(jnp.int32, (c, c), 1)
    incl = row >= col
    tri = jnp.broadcast_to(incl.astype(BF16), (nb, c, c))
    acum_all = _cumsum_rows(tri, da_all)
    acum_all_t = jnp.swapaxes(acum_all, 1, 2)
    eacum_all = jnp.exp(acum_all)

    ys = []
    for g in range(SSD_GROUPS):
        bg = bm[:, :, g * SSD_STATE:(g + 1) * SSD_STATE]
        cg = cm[:, :, g * SSD_STATE:(g + 1) * SSD_STATE]
        scores = _bmm_nt(cg, bg)
        for hh in range(SSD_HPG):
            h = g * SSD_HPG + hh
            ln = SSD_DT_LANE + h
            dtc = dt_all[:, :, ln:ln + 1]
            ac = acum_all[:, :, ln:ln + 1]
            ar = acum_all_t[:, ln:ln + 1, :]
            alast = ar[:, :, c - 1:c]
            lmat = jnp.where(incl, jnp.exp(jnp.minimum(ac - ar, 0.0)), 0.0)
            xd = xs[:, :, h * SSD_HEADDIM:(h + 1) * SSD_HEADDIM] * dtc
            y_diag = _bmm(scores * lmat, xd)
            hin = h_ref[:, h]
            y_off = _bmm_nt(cg * eacum_all[:, :, ln:ln + 1], hin)
            states = _bmm(jnp.swapaxes(xd, 1, 2), bg * jnp.exp(alast - ac))
            h_ref[:, h] = hin * jnp.exp(alast) + states
            ys.append(y_diag + y_off)
    yf = jnp.concatenate(ys, axis=-1) + d_ref[...] * xs
    yf = yf * _silu(z_ref[...])
    gw = SSD_INNER // SSD_GROUPS
    for g in range(SSD_GROUPS):
        seg = yf[:, :, g * gw:(g + 1) * gw]
        o_ref[:, :, g * gw:(g + 1) * gw] = (seg * lax.rsqrt(jnp.mean(seg * seg, axis=-1, keepdims=True) + RMS_EPS)
                                            * normw_ref[:, g * gw:(g + 1) * gw])


def ssd_prompt(b_xbc, b_z, small, conv_w, conv_b, a_log, dt_bias, d_skip, norm_w):
    bsz, L, _ = b_xbc.shape
    c = SSD_CHUNK
    assert L % c == 0
    lanes = small.shape[-1]
    sl = slice(SSD_DT_LANE, SSD_DT_LANE + SSD_HEADS)
    alog_row = jnp.zeros((1, lanes), F32).at[0, sl].set(a_log)
    dtb_row = jnp.zeros((1, lanes), F32).at[0, sl].set(dt_bias)
    d_row = jnp.repeat(d_skip, SSD_HEADDIM).reshape(1, SSD_INNER)
    const2 = lambda n: (0, 0)
    return pl.pallas_call(
        _ssd_prompt_kernel,
        out_shape=(jax.ShapeDtypeStruct((bsz, L, SSD_INNER), F32),
                   jax.ShapeDtypeStruct((bsz, SSD_HEADS, SSD_HEADDIM, SSD_STATE), F32),
                   jax.ShapeDtypeStruct((bsz, CONV_WIDTH - 1, SSD_CONV_DIM), F32)),
        grid=(L // c,),
        in_specs=[
            pl.BlockSpec((bsz, c, SSD_CONV_DIM), lambda n: (0, n, 0)),
            pl.BlockSpec((bsz, c, SSD_INNER), lambda n: (0, n, 0)),
            pl.BlockSpec((bsz, c, lanes), lambda n: (0, n, 0)),
            pl.BlockSpec((CONV_WIDTH, SSD_CONV_DIM), const2),
            pl.BlockSpec((1, SSD_CONV_DIM), const2),
            pl.BlockSpec((1, lanes), const2),
            pl.BlockSpec((1, lanes), const2),
            pl.BlockSpec((1, SSD_INNER), const2),
            pl.BlockSpec((1, SSD_INNER), const2),
        ],
        out_specs=(
            pl.BlockSpec((bsz, c, SSD_INNER), lambda n: (0, n, 0)),
            pl.BlockSpec((bsz, SSD_HEADS, SSD_HEADDIM, SSD_STATE), lambda n: (0, 0, 0, 0)),
            pl.BlockSpec((bsz, CONV_WIDTH - 1, SSD_CONV_DIM), lambda n: (0, 0, 0)),
        ),
        scratch_shapes=[pltpu.VMEM((bsz, c + 8, SSD_CONV_DIM), F32)],
        compiler_params=pltpu.CompilerParams(
            dimension_semantics=("arbitrary",), vmem_limit_bytes=VMEM_LIMIT_BYTES),
        name="ssd_prompt",
    )(b_xbc, b_z, small, conv_w, conv_b.reshape(1, SSD_CONV_DIM), alog_row, dtb_row, d_row,
      norm_w.reshape(1, SSD_INNER))


ATT_BAND = 128
ATT_MASKED = -1e30
assert all(w // d == ATT_BAND for w, d in ATT_GROUPS)


def _alibi_slopes_np():
    idx = np.arange(1, ATT_HEADS + 1, dtype=np.float32)
    return np.exp2(-8.0 * idx / ATT_HEADS).astype(np.float32).reshape(N_ATT_GROUPS, ATT_HEADS_PER_GROUP)


def _attn_bias(gi, first):
    nk = 2 * ATT_BAND
    i = lax.broadcasted_iota(jnp.int32, (ATT_BAND, nk), 0)
    j = lax.broadcasted_iota(jnp.int32, (ATT_BAND, nk), 1)
    du = i + ATT_BAND - j
    duf = du.astype(F32)
    dil = ATT_GROUPS[gi][1]
    slopes = _alibi_slopes_np()[gi]
    if first:
        inside = jnp.minimum(du, j - ATT_BAND) >= 0
    else:
        inside = jnp.abs(2 * du - ATT_BAND) <= ATT_BAND
    return [jnp.where(inside, duf * float(-slopes[hh] * dil * LOG2E), ATT_MASKED)
            for hh in range(ATT_HEADS_PER_GROUP)]


ATT_LANE_CHUNKS = ATT_OUT // 128
ATT_HEADS_PER_CHUNK = 128 // ATT_HEAD_DIM
ATT_UNROLL = (5, 4, 4)
LOG2E = math.log2(math.e)
LN2 = math.log(2.0)


def _attn_block(q, keys, vals, bias):
    nk = keys.shape[0]
    low = lax.broadcasted_iota(jnp.int32, (ATT_BAND, 128), 1) < ATT_HEAD_DIM
    q2 = q * LOG2E
    kb = keys.astype(BF16)
    va = jnp.concatenate([vals.astype(BF16), jnp.ones((nk, 128), BF16)], axis=1)
    res = []
    for hh in range(ATT_HEADS_PER_CHUNK):
        qh = jnp.where(low if hh == 0 else jnp.logical_not(low), q2, 0.0).astype(BF16)
        s = lax.dot_general(qh, kb, (((1,), (1,)), ((), ())), preferred_element_type=F32) + bias[hh]
        m = jnp.max(s, axis=-1, keepdims=True)
        p = jnp.exp2(s - m)
        oa = jnp.dot(p.astype(BF16), va, preferred_element_type=F32)
        l = oa[:, 128:]
        res.append((oa[:, :128] * (1.0 / l), (m + jnp.log2(l)) * LN2))
    return jnp.where(low, res[0][0], res[1][0]), jnp.where(low, res[0][1], res[1][1])


def _attn_prompt_kernel(q_ref, kv_ref, o_ref, lse_ref):
    g = pl.program_id(0)
    L = q_ref.shape[2]
    nc = ATT_LANE_CHUNKS

    for gi, (win, dil) in enumerate(ATT_GROUPS):
        @pl.when(g == gi)
        def _(gi=gi, win=win, dil=dil):
            nblk = L // win

            def rows(start):
                return pl.ds(start, ATT_BAND, stride=dil) if dil > 1 else pl.ds(start, ATT_BAND)

            def unit(cur, prev, bias):
                for c in range(nc):
                    hs = slice(c * ATT_HEADS_PER_CHUNK, (c + 1) * ATT_HEADS_PER_CHUNK)
                    keys = jnp.concatenate([kv_ref[0, c, prev, :], kv_ref[0, c, cur, :]], axis=0)
                    vals = jnp.concatenate([kv_ref[0, nc + c, prev, :], kv_ref[0, nc + c, cur, :]], axis=0)
                    o, lse = _attn_block(q_ref[0, c, cur, :], keys, vals, bias[hs])
                    o_ref[0, c, cur, :] = o
                    lse_ref[0, c, cur, :] = lse

            bias0 = _attn_bias(gi, True)

            def body0(r, carry):
                unit(rows(r), rows(r), bias0)
                return carry

            lax.fori_loop(0, dil, body0, 0, unroll=min(dil, ATT_UNROLL[gi]))
            if nblk > 1:
                bias1 = _attn_bias(gi, False)

                def body(u, carry):
                    start = (u // dil + 1) * win + u % dil
                    unit(rows(start), rows(start - win), bias1)
                    return carry

                lax.fori_loop(0, (nblk - 1) * dil, body, 0, unroll=ATT_UNROLL[gi])


def attn_prompt(q5, kv5, bsz):
    ng, nc, _, lanes = q5.shape
    L = SEQ
    assert all(L % w == 0 for w, _ in ATT_GROUPS)
    blk = lambda n: pl.BlockSpec((1, n, L, lanes), lambda g, b: (g, 0, b, 0))
    out = jax.ShapeDtypeStruct((ng, nc, bsz * L, lanes), F32)
    return pl.pallas_call(
        _attn_prompt_kernel,
        out_shape=(out, out),
        grid=(ng, bsz),
        in_specs=[blk(nc), blk(2 * nc)],
        out_specs=(blk(nc), blk(nc)),
        compiler_params=pltpu.CompilerParams(
            dimension_semantics=("arbitrary", "arbitrary"), vmem_limit_bytes=VMEM_LIMIT_BYTES),
        name="attn_prompt",
    )(q5, kv5)


SMALL_LANES = 128
_OFF = np.cumsum((0,) + IN_SPLITS).tolist()
PROJ_AB_WIDTHS = (GDN_CONV_DIM, GDN_VW, SSD_INNER, SSD_CONV_DIM, SMALL_LANES)
PROJ_AB_N = sum(PROJ_AB_WIDTHS)
PROJ_C_N = 3 * ATT_HEADS * ATT_HEAD_DIM
ATT_Q_N = ATT_HEADS * ATT_HEAD_DIM


def prep_weights(w_in):
    o = _OFF
    pad = jnp.zeros(w_in.shape[:-1] + (SMALL_LANES - 2 * GDN_HEADS - SSD_HEADS,), w_in.dtype)
    w_ab = jnp.concatenate([w_in[..., o[0]:o[1]], w_in[..., o[1]:o[2]], w_in[..., o[4]:o[5]], w_in[..., o[5]:o[6]],
                            w_in[..., o[2]:o[4]], w_in[..., o[6]:o[7]], pad], axis=-1).astype(BF16)
    w_c = w_in[..., o[7]:o[8]].astype(BF16)
    w_g = w_in[..., o[8]:o[9]].astype(BF16)
    return w_ab, w_c, w_g


def _rmsnorm_rows(h, w_row):
    return h * lax.rsqrt(jnp.mean(h * h, axis=-1, keepdims=True) + RMS_EPS) * w_row


def _proj_ab_kernel(h_ref, nw_ref, w_ref, qkv_ref, gate_ref, z_ref, xbc_ref, small_ref):
    u = _rmsnorm_rows(h_ref[...], nw_ref[0]).astype(BF16)
    y = jnp.dot(u, w_ref[0], preferred_element_type=F32)
    lo = 0
    for ref, wd in zip((qkv_ref, gate_ref, z_ref, xbc_ref, small_ref), PROJ_AB_WIDTHS):
        ref[...] = y[:, lo:lo + wd]
        lo += wd


def proj_ab(h, norm_w, w_ab, layer, tm):
    t = h.shape[0]
    assert t % tm == 0
    return pl.pallas_call(
        _proj_ab_kernel,
        out_shape=tuple(jax.ShapeDtypeStruct((t, wd), F32) for wd in PROJ_AB_WIDTHS),
        grid=(t // tm,),
        in_specs=[pl.BlockSpec((tm, D_MODEL), lambda i: (i, 0)), _layer_spec(norm_w, layer), _layer_spec(w_ab, layer)],
        out_specs=tuple(pl.BlockSpec((tm, wd), lambda i: (i, 0)) for wd in PROJ_AB_WIDTHS),
        compiler_params=pltpu.CompilerParams(
            dimension_semantics=("arbitrary",), vmem_limit_bytes=VMEM_LIMIT_BYTES),
        name="proj_ab",
    )(h, norm_w, w_ab)


def _head_rmsnorm(x, w_row, seg):
    outs = []
    for c in range(x.shape[-1] // 128):
        xc = x[:, c * 128:(c + 1) * 128]
        hi, lo = _split2(xc * xc)
        ss = jnp.dot(hi, seg, preferred_element_type=F32) + jnp.dot(lo, seg, preferred_element_type=F32)
        outs.append(xc * lax.rsqrt(ss * (1.0 / ATT_HEAD_DIM) + RMS_EPS) * w_row)
    return outs


def _proj_c_kernel(h_ref, nw_ref, w_ref, qw_ref, kw_ref, *out_refs, chunked):
    u = _rmsnorm_rows(h_ref[...], nw_ref[0]).astype(BF16)
    y = jnp.dot(u, w_ref[0], preferred_element_type=F32)
    r = lax.broadcasted_iota(jnp.int32, (128, 128), 0) // ATT_HEAD_DIM
    c = lax.broadcasted_iota(jnp.int32, (128, 128), 1) // ATT_HEAD_DIM
    seg = (r == c).astype(BF16)
    qn = [t * ATT_HEAD_DIM ** -0.5 for t in _head_rmsnorm(y[:, :ATT_Q_N], qw_ref[...], seg)]
    kn = _head_rmsnorm(y[:, ATT_Q_N:2 * ATT_Q_N], kw_ref[...], seg)
    v = y[:, 2 * ATT_Q_N:]
    nc = ATT_LANE_CHUNKS
    if chunked:
        q5_ref, kv5_ref, *kvrows_refs = out_refs
    else:
        q_ref, *kvrows_refs = out_refs
    for g in range(N_ATT_GROUPS):
        for cc in range(nc):
            i = g * nc + cc
            vc = v[:, i * 128:(i + 1) * 128]
            kvrows_refs[g][:, cc * 128:(cc + 1) * 128] = kn[i]
            kvrows_refs[g][:, ATT_OUT + cc * 128:ATT_OUT + (cc + 1) * 128] = vc
            if chunked:
                q5_ref[g, cc] = qn[i]
                kv5_ref[g, cc] = kn[i]
                kv5_ref[g, nc + cc] = vc
            else:
                q_ref[:, i * 128:(i + 1) * 128] = qn[i]


def proj_c(h, norm_w, w_c, layer, q_norm, k_norm, tm, chunked):
    t = h.shape[0]
    assert t % tm == 0
    const = dict(pipeline_mode=pl.Buffered(1))
    ng, nc = N_ATT_GROUPS, ATT_LANE_CHUNKS
    if chunked:
        out_shape = [jax.ShapeDtypeStruct((ng, nc, t, 128), F32), jax.ShapeDtypeStruct((ng, 2 * nc, t, 128), F32)]
        out_specs = [pl.BlockSpec((ng, nc, tm, 128), lambda i: (0, 0, i, 0)),
                     pl.BlockSpec((ng, 2 * nc, tm, 128), lambda i: (0, 0, i, 0))]
    else:
        out_shape = [jax.ShapeDtypeStruct((t, ATT_Q_N), F32)]
        out_specs = [pl.BlockSpec((tm, ATT_Q_N), lambda i: (i, 0))]
    out_shape += [jax.ShapeDtypeStruct((t, 2 * ATT_OUT), F32)] * ng
    out_specs += [pl.BlockSpec((tm, 2 * ATT_OUT), lambda i: (i, 0))] * ng
    tile2 = lambda w: jnp.tile(w, 128 // ATT_HEAD_DIM).reshape(1, 128)
    return pl.pallas_call(
        functools.partial(_proj_c_kernel, chunked=chunked),
        out_shape=tuple(out_shape),
        grid=(t // tm,),
        in_specs=[
            pl.BlockSpec((tm, D_MODEL), lambda i: (i, 0)),
            _layer_spec(norm_w, layer),
            _layer_spec(w_c, layer),
            pl.BlockSpec((1, 128), lambda i: (0, 0), **const),
            pl.BlockSpec((1, 128), lambda i: (0, 0), **const),
        ],
        out_specs=tuple(out_specs),
        compiler_params=pltpu.CompilerParams(
            dimension_semantics=("arbitrary",), vmem_limit_bytes=VMEM_LIMIT_BYTES),
        name="proj_c",
    )(h, norm_w, w_c, tile2(q_norm), tile2(k_norm))


ROW_SUBTILE = 256


def _merge_kernel(h_ref, oa_ref, ob_ref, *refs, chunked):
    att_refs, refs = (refs[:2], refs[2:]) if chunked else (refs[:1], refs[1:])
    nw_ref, wg_ref, gb_ref, wa_ref, wb_ref, wc_ref, wo_ref, out_ref = refs
    tm = h_ref.shape[0]
    sub = min(tm, ROW_SUBTILE)
    for r0 in range(0, tm, sub):
        rs = slice(r0, r0 + sub)
        if chunked:
            o5_ref, lse5_ref = att_refs
            ocs = []
            for cc in range(ATT_LANE_CHUNKS):
                ls = [lse5_ref[g, cc, rs, :] for g in range(N_ATT_GROUPS)]
                m = functools.reduce(jnp.maximum, ls)
                es = [jnp.exp(l - m) for l in ls]
                num = sum(e * o5_ref[g, cc, rs, :] for g, e in enumerate(es))
                ocs.append(num * (1.0 / sum(es)))
            o_c = jnp.concatenate(ocs, axis=-1)
        else:
            o_c = att_refs[0][rs, :]
        h = h_ref[rs, :]
        u = _rmsnorm_rows(h, nw_ref[0]).astype(BF16)
        merged = None
        for i, (o, w_ref) in enumerate(((oa_ref[rs, :], wa_ref), (ob_ref[rs, :], wb_ref), (o_c, wc_ref))):
            sl = slice(i * D_MODEL, (i + 1) * D_MODEL)
            gate = jax.nn.sigmoid(jnp.dot(u, wg_ref[0, :, sl], preferred_element_type=F32) + gb_ref[0, :, sl])
            term = gate * jnp.dot(o.astype(BF16), w_ref[0], preferred_element_type=F32)
            merged = term if merged is None else merged + term
        out_ref[rs, :] = h + jnp.dot(merged.astype(BF16), wo_ref[0], preferred_element_type=F32)


def merge_block(h, o_a, o_b, o_att, params, layer, tm):
    t = h.shape[0]
    assert t % tm == 0
    chunked = isinstance(o_att, tuple)
    rows = lambda wd: pl.BlockSpec((tm, wd), lambda i: (i, 0))
    if chunked:
        att_specs = [pl.BlockSpec((N_ATT_GROUPS, ATT_LANE_CHUNKS, tm, 128), lambda i: (0, 0, i, 0))] * 2
        att_args = list(o_att)
    else:
        att_specs = [rows(ATT_OUT)]
        att_args = [o_att]
    return pl.pallas_call(
        functools.partial(_merge_kernel, chunked=chunked),
        out_shape=jax.ShapeDtypeStruct(h.shape, F32),
        grid=(t // tm,),
        in_specs=[rows(D_MODEL), rows(GDN_VW), rows(SSD_INNER)] + att_specs + [_layer_spec(a, layer) for a in params],
        out_specs=rows(D_MODEL),
        compiler_params=pltpu.CompilerParams(
            dimension_semantics=("arbitrary",), vmem_limit_bytes=VMEM_LIMIT_BYTES),
        name="merge",
    )(h, o_a, o_b, *att_args, *params)


DEC_TB = 8


def _conv_step(cst_ref, x, w_ref):
    prev = [cst_ref[0, i] for i in range(CONV_WIDTH - 1)]
    y = x * w_ref[CONV_WIDTH - 1:CONV_WIDTH, :]
    for i, pr in enumerate(prev):
        y = y + pr * w_ref[i:i + 1, :]
    return y, prev[1:] + [x]


def _gdn_decode_kernel(qkv_ref, gate_ref, small_ref, cst_ref, s_ref, carried_ref, convw_ref, alog_ref, dtb_ref,
                       normw_ref, o_ref, so_ref, csto_ref):
    del carried_ref
    tb = qkv_ref.shape[0]
    x = qkv_ref[...]
    y, new_cst = _conv_step(cst_ref, x, convw_ref)
    for i, r in enumerate(new_cst):
        csto_ref[i] = r
    y = _silu(y)
    small = small_ref[...]
    beta_all = jax.nn.sigmoid(small)
    eg_all = jnp.exp(-jnp.exp(alog_ref[...]) * _softplus(small + dtb_ref[...]))
    lanes = small.shape[-1]
    r = lax.broadcasted_iota(jnp.int32, (lanes, GDN_HEADS * GDN_DV), 0)
    c = lax.broadcasted_iota(jnp.int32, (lanes, GDN_HEADS * GDN_DV), 1)
    sel = (r == GDN_HEADS + c // GDN_DV).astype(BF16)
    eg_wide = sum(jnp.dot(t, sel, preferred_element_type=F32) for t in _split3(eg_all))
    tok = lax.broadcasted_iota(jnp.int32, (tb, 1), 0)
    tok2 = lax.broadcasted_iota(jnp.int32, (2 * tb, 1), 0)
    for h in range(GDN_HEADS):
        lo = h * GDN_DK
        q = y[:, lo:lo + GDN_DK]
        k = y[:, GDN_QK + lo:GDN_QK + lo + GDN_DK]
        v = y[:, 2 * GDN_QK + h * GDN_DV:2 * GDN_QK + (h + 1) * GDN_DV]
        q = q * (lax.rsqrt(jnp.sum(q * q, axis=-1, keepdims=True) + RMS_EPS) * GDN_DK ** -0.5)
        k = k * lax.rsqrt(jnp.sum(k * k, axis=-1, keepdims=True) + RMS_EPS)
        beta = beta_all[:, h:h + 1]
        eg = eg_all[:, GDN_HEADS + h:GDN_HEADS + h + 1]
        qk = jnp.sum(q * k, axis=-1, keepdims=True)
        kq = jnp.concatenate([k, q], axis=0).astype(BF16)
        kq_s = jnp.zeros((2 * tb, GDN_DV), F32)
        for t in range(tb):
            r = jnp.dot(kq, s_ref[0, t, h].astype(BF16), preferred_element_type=F32)
            kq_s = jnp.where(tok2 % tb == t, r, kq_s)
        u = beta * (v - eg * kq_s[:tb])
        o = eg * kq_s[tb:] + qk * u
        kb = k.astype(BF16)
        for t in range(tb):
            outer = lax.dot_general(kb, jnp.where(tok == t, u, 0.0).astype(BF16), (((0,), (0,)), ((), ())),
                                    preferred_element_type=F32)
            so_ref[0, t, h] = s_ref[0, t, h] * eg_wide[t:t + 1, h * GDN_DV:(h + 1) * GDN_DV] + outer
        o = o * lax.rsqrt(jnp.mean(o * o, axis=-1, keepdims=True) + RMS_EPS) * normw_ref[...]
        o_ref[:, h * GDN_DV:(h + 1) * GDN_DV] = o * _silu(gate_ref[:, h * GDN_DV:(h + 1) * GDN_DV])


def _carried(new_states, shape, operand_index):
    if new_states is None:
        return jnp.zeros((8, 128), F32), {}
    assert new_states.shape == shape
    return new_states, {operand_index: 1}


def gdn_decode(a_qkv, a_gate, small, conv_state_t, state, new_states, layer, conv_w, a_log, dt_bias, norm_w):
    bsz = a_qkv.shape[0]
    prev, aliases = _carried(new_states, state.shape, 5)
    tb = DEC_TB
    lanes = small.shape[-1]
    alog_row = jnp.zeros((1, lanes), F32).at[0, GDN_HEADS:2 * GDN_HEADS].set(a_log)
    dtb_row = jnp.zeros((1, lanes), F32).at[0, GDN_HEADS:2 * GDN_HEADS].set(dt_bias)
    nprev = CONV_WIDTH - 1
    rows = lambda wd: pl.BlockSpec((tb, wd), lambda i: (i, 0))
    const2 = lambda i: (0, 0)
    return pl.pallas_call(
        _gdn_decode_kernel,
        out_shape=(jax.ShapeDtypeStruct((bsz, GDN_VW), F32),
                   jax.ShapeDtypeStruct(state.shape, F32),
                   jax.ShapeDtypeStruct((nprev, bsz, GDN_CONV_DIM), F32)),
        grid=(bsz // tb,),
        in_specs=[rows(GDN_CONV_DIM), rows(GDN_VW), rows(lanes),
                  pl.BlockSpec((1, nprev, tb, GDN_CONV_DIM), lambda i: (layer, 0, i, 0)),
                  pl.BlockSpec((1, tb, GDN_HEADS, GDN_DK, GDN_DV), lambda i: (layer, i, 0, 0, 0)),
                  pl.BlockSpec(memory_space=pl.ANY),
                  pl.BlockSpec((CONV_WIDTH, GDN_CONV_DIM), const2),
                  pl.BlockSpec((1, lanes), const2), pl.BlockSpec((1, lanes), const2),
                  pl.BlockSpec((1, GDN_DV), const2)],
        out_specs=(rows(GDN_VW),
                   pl.BlockSpec((1, tb, GDN_HEADS, GDN_DK, GDN_DV), lambda i: (layer, i, 0, 0, 0)),
                   pl.BlockSpec((nprev, tb, GDN_CONV_DIM), lambda i: (0, i, 0))),
        input_output_aliases=aliases,
        compiler_params=pltpu.CompilerParams(
            dimension_semantics=("arbitrary",), vmem_limit_bytes=VMEM_LIMIT_BYTES),
        name="gdn_decode",
    )(a_qkv, a_gate, small, conv_state_t, state, prev, conv_w, alog_row, dtb_row, norm_w.reshape(1, GDN_DV))


def _ssd_decode_kernel(xbc_ref, z_ref, small_ref, cst_ref, h_ref, carried_ref, convw_ref, convb_ref, alog_ref,
                       dtb_ref, d_ref, normw_ref, o_ref, ho_ref, csto_ref):
    del carried_ref
    tb = xbc_ref.shape[0]
    x = xbc_ref[...]
    y, new_cst = _conv_step(cst_ref, x, convw_ref)
    for i, r in enumerate(new_cst):
        csto_ref[i] = r
    y = _silu(y + convb_ref[...])
    xs = y[:, :SSD_INNER]
    bm = y[:, SSD_INNER:SSD_INNER + SSD_BC]
    cm = y[:, SSD_INNER + SSD_BC:]
    dt_all = _softplus(small_ref[...] + dtb_ref[...])
    eda_all = jnp.exp(dt_all * (-jnp.exp(alog_ref[...])))
    lanes = dt_all.shape[-1]

    def spread(v, width):
        r = lax.broadcasted_iota(jnp.int32, (lanes, SSD_HEADS * width), 0)
        c = lax.broadcasted_iota(jnp.int32, (lanes, SSD_HEADS * width), 1)
        sel = (r == SSD_DT_LANE + c // width).astype(BF16)
        return sum(jnp.dot(t, sel, preferred_element_type=F32) for t in _split3(v))

    eda_wide = spread(eda_all, SSD_STATE)
    xd = xs * spread(dt_all, SSD_HEADDIM)
    tok = lax.broadcasted_iota(jnp.int32, (tb, 1), 0)
    gp = SSD_HPG * SSD_HEADDIM
    ys = []
    for g in range(SSD_GROUPS):
        bg = bm[:, g * SSD_STATE:(g + 1) * SSD_STATE]
        cg = cm[:, g * SSD_STATE:(g + 1) * SSD_STATE].astype(BF16)
        xd_g = xd[:, g * gp:(g + 1) * gp].astype(BF16)
        y_g = jnp.zeros((tb, gp), F32)
        for t in range(tb):
            outer = lax.dot_general(xd_g, jnp.where(tok == t, bg, 0.0).astype(BF16), (((0,), (0,)), ((), ())),
                                    preferred_element_type=F32)
            hns = []
            for hh in range(SSD_HPG):
                h = g * SSD_HPG + hh
                hn = (h_ref[0, t, h] * eda_wide[t:t + 1, h * SSD_STATE:(h + 1) * SSD_STATE]
                      + outer[hh * SSD_HEADDIM:(hh + 1) * SSD_HEADDIM])
                ho_ref[0, t, h] = hn
                hns.append(hn.astype(BF16))
            y_t = lax.dot_general(cg, jnp.concatenate(hns, axis=0), (((1,), (1,)), ((), ())),
                                  preferred_element_type=F32)
            y_g = jnp.where(tok == t, y_t, y_g)
        ys.append(y_g)
    yf = jnp.concatenate(ys, axis=-1) + d_ref[...] * xs
    yf = yf * _silu(z_ref[...])
    gw = SSD_INNER // SSD_GROUPS
    for g in range(SSD_GROUPS):
        seg = yf[:, g * gw:(g + 1) * gw]
        o_ref[:, g * gw:(g + 1) * gw] = (seg * lax.rsqrt(jnp.mean(seg * seg, axis=-1, keepdims=True) + RMS_EPS)
                                         * normw_ref[:, g * gw:(g + 1) * gw])


def ssd_decode(b_xbc, b_z, small, conv_state_t, state, new_states, layer, conv_w, conv_b, a_log, dt_bias, d_skip,
               norm_w):
    bsz = b_xbc.shape[0]
    prev, aliases = _carried(new_states, state.shape, 5)
    tb = DEC_TB
    lanes = small.shape[-1]
    sl = slice(SSD_DT_LANE, SSD_DT_LANE + SSD_HEADS)
    alog_row = jnp.zeros((1, lanes), F32).at[0, sl].set(a_log)
    dtb_row = jnp.zeros((1, lanes), F32).at[0, sl].set(dt_bias)
    d_row = jnp.repeat(d_skip, SSD_HEADDIM).reshape(1, SSD_INNER)
    nprev = CONV_WIDTH - 1
    rows = lambda wd: pl.BlockSpec((tb, wd), lambda i: (i, 0))
    const2 = lambda i: (0, 0)
    return pl.pallas_call(
        _ssd_decode_kernel,
        out_shape=(jax.ShapeDtypeStruct((bsz, SSD_INNER), F32),
                   jax.ShapeDtypeStruct(state.shape, F32),
                   jax.ShapeDtypeStruct((nprev, bsz, SSD_CONV_DIM), F32)),
        grid=(bsz // tb,),
        in_specs=[rows(SSD_CONV_DIM), rows(SSD_INNER), rows(lanes),
                  pl.BlockSpec((1, nprev, tb, SSD_CONV_DIM), lambda i: (layer, 0, i, 0)),
                  pl.BlockSpec((1, tb, SSD_HEADS, SSD_HEADDIM, SSD_STATE), lambda i: (layer, i, 0, 0, 0)),
                  pl.BlockSpec(memory_space=pl.ANY),
                  pl.BlockSpec((CONV_WIDTH, SSD_CONV_DIM), const2), pl.BlockSpec((1, SSD_CONV_DIM), const2),
                  pl.BlockSpec((1, lanes), const2), pl.BlockSpec((1, lanes), const2),
                  pl.BlockSpec((1, SSD_INNER), const2), pl.BlockSpec((1, SSD_INNER), const2)],
        out_specs=(rows(SSD_INNER),
                   pl.BlockSpec((1, tb, SSD_HEADS, SSD_HEADDIM, SSD_STATE), lambda i: (layer, i, 0, 0, 0)),
                   pl.BlockSpec((nprev, tb, SSD_CONV_DIM), lambda i: (0, i, 0))),
        input_output_aliases=aliases,
        compiler_params=pltpu.CompilerParams(
            dimension_semantics=("arbitrary",), vmem_limit_bytes=VMEM_LIMIT_BYTES),
        name="ssd_decode",
    )(b_xbc, b_z, small, conv_state_t, state, prev, conv_w, conv_b.reshape(1, SSD_CONV_DIM),
      alog_row, dtb_row, d_row, norm_w.reshape(1, SSD_INNER))


ATT_DEC_TB = 8
ATT_DEC_TOK = 2


def _attn_decode_kernel(q_ref, kvn0_ref, kvn1_ref, kvn2_ref, c0_ref, c1_ref, c2_ref, o_ref, acc_ref):
    tb = q_ref.shape[0]
    ntok = c0_ref.shape[1]
    hd = ATT_HEAD_DIM
    k = pl.program_id(1)
    slopes = _alibi_slopes_np()
    q_t = q_ref[...].T
    kvn_t = [ref[...].T for ref in (kvn0_ref, kvn1_ref, kvn2_ref)]
    lane_t = lax.broadcasted_iota(jnp.int32, (1, tb), 1)

    def column(x_t, tok):
        return jnp.sum(jnp.where(lane_t == tok, x_t, 0.0), axis=1, keepdims=True)

    @pl.when(k == 0)
    def _():
        acc_ref[...] = jnp.zeros_like(acc_ref)

    acc = acc_ref[...]
    lane = lax.broadcasted_iota(jnp.int32, acc_ref.shape, 1)
    for tt in range(ntok):
        tok = k * ntok + tt
        q_col = column(q_t, tok)
        og = [[None] * N_ATT_GROUPS for _ in range(ATT_HEADS_PER_GROUP)]
        lg = [[None] * N_ATT_GROUPS for _ in range(ATT_HEADS_PER_GROUP)]
        for gi, (c_ref, (win, dil)) in enumerate(zip((c0_ref, c1_ref, c2_ref), ATT_GROUPS)):
            kvn_col = column(kvn_t[gi], tok)
            j = lax.broadcasted_iota(jnp.int32, (1, win), 1)
            dist = win - j
            is_key = (dist % dil) == 0
            for hh in range(ATT_HEADS_PER_GROUP):
                lo = gi * ATT_OUT + hh * hd
                qc = q_col[lo:lo + hd]
                kn = kvn_col[hh * hd:(hh + 1) * hd]
                vn = kvn_col[ATT_OUT + hh * hd:ATT_OUT + (hh + 1) * hd]
                bias = jnp.where(is_key, dist.astype(F32) * float(-slopes[gi, hh]), ATT_MASKED)
                s = jnp.sum(c_ref[0, tt, 0, hh] * qc, axis=0, keepdims=True) + bias
                s_new = jnp.sum(qc * kn, axis=0, keepdims=True)
                m = jnp.maximum(jnp.max(s, axis=-1, keepdims=True), s_new)
                p = jnp.exp(s - m)
                p_new = jnp.exp(s_new - m)
                l = jnp.sum(p, axis=-1, keepdims=True) + p_new
                og[hh][gi] = (jnp.sum(c_ref[0, tt, 1, hh] * p, axis=-1, keepdims=True) + p_new * vn) * (1.0 / l)
                lg[hh][gi] = m + jnp.log(l)
        heads = []
        for hh in range(ATT_HEADS_PER_GROUP):
            m = functools.reduce(jnp.maximum, lg[hh])
            es = [jnp.exp(l - m) for l in lg[hh]]
            heads.append(sum(e * o for e, o in zip(es, og[hh])) * (1.0 / sum(es)))
        acc = jnp.where(lane == tok, jnp.concatenate(heads, axis=0), acc)
    acc_ref[...] = acc

    @pl.when(k == tb // ntok - 1)
    def _():
        o_ref[...] = acc.T[:tb, :]


def attn_decode(q_rows, kv_new, caches, layer):
    bsz = q_rows.shape[0]
    tb, ntok = ATT_DEC_TB, ATT_DEC_TOK
    views, specs = [], []
    for cache, (win, dil) in zip(caches, ATT_GROUPS):
        assert cache.shape[2] == win and win % dil == 0
        views.append(jnp.transpose(cache, (0, 1, 3, 4, 5, 2)))
        specs.append(pl.BlockSpec((1, ntok) + KV_TAIL + (win,),
                                  lambda i, k: (layer, i * (tb // ntok) + k, 0, 0, 0, 0)))
    return pl.pallas_call(
        _attn_decode_kernel,
        out_shape=jax.ShapeDtypeStruct((bsz, ATT_OUT), F32),
        grid=(bsz // tb, tb // ntok),
        in_specs=([pl.BlockSpec((tb, ATT_Q_N), lambda i, k: (i, 0))]
                  + [pl.BlockSpec((tb, 2 * ATT_OUT), lambda i, k: (i, 0))] * N_ATT_GROUPS + specs),
        out_specs=pl.BlockSpec((tb, ATT_OUT), lambda i, k: (i, 0)),
        scratch_shapes=[pltpu.VMEM((ATT_OUT, 128), F32)],
        compiler_params=pltpu.CompilerParams(
            dimension_semantics=("arbitrary", "arbitrary"), vmem_limit_bytes=VMEM_LIMIT_BYTES),
        name="attn_decode",
    )(q_rows, *kv_new, *views)


PROMPT_TM = 512
PROMPT_FFN_TM = 256
KV_TAIL = (2, ATT_HEADS_PER_GROUP, ATT_HEAD_DIM)


def prompt_trunk(x, p):
    bsz, L, _ = x.shape
    t = bsz * L
    h = x.reshape(t, D_MODEL)
    new = [[] for _ in range(4 + N_ATT_GROUPS)]
    seq = lambda a: a.reshape(bsz, L, a.shape[-1])
    for l in range(DEPTH):
        h = ffn_block(h, p['ffn1_norm'], p['ffn1_w_in'], p['ffn1_w_out'], l, PROMPT_FFN_TM)
        qkv, gate, z, xbc, small = proj_ab(h, p['mix_norm'], p['w_ab'], l, PROMPT_TM)
        q5, kv5, *kvrows = proj_c(h, p['mix_norm'], p['w_c'], l, p['q_norm'][l], p['k_norm'][l], PROMPT_TM, True)
        o_a, s_gdn, c_gdn = gdn_prompt(seq(qkv), seq(gate), seq(small), p['gdn_conv_w'][l], p['gdn_a_log'][l],
                                       p['gdn_dt_bias'][l], p['gdn_norm_w'][l])
        o_b, s_ssd, c_ssd = ssd_prompt(seq(xbc), seq(z), seq(small), p['ssd_conv_w'][l], p['ssd_conv_b'][l],
                                       p['ssd_a_log'][l], p['ssd_dt_bias'][l], p['ssd_d'][l], p['ssd_norm_w'][l])
        o5, lse5 = attn_prompt(q5, kv5, bsz)
        h = merge_block(h, o_a.reshape(t, GDN_VW), o_b.reshape(t, SSD_INNER), (o5, lse5), p['merge'], l, PROMPT_TM)
        h = ffn_block(h, p['ffn2_norm'], p['ffn2_w_in'], p['ffn2_w_out'], l, PROMPT_FFN_TM)
        kv_new = [kvrows[gi].reshape(bsz, L, 2 * ATT_OUT)[:, L - min(win, L):].reshape((bsz, min(win, L)) + KV_TAIL)
                  for gi, (win, _) in enumerate(ATT_GROUPS)]
        for lst, s in zip(new, (s_gdn, c_gdn, s_ssd, c_ssd, *kv_new)):
            lst.append(s)
    return h.reshape(bsz, L, D_MODEL), [jnp.stack(lst, axis=0) for lst in new]


def sample_trunk(x, p, state_gdn, state_gdn_conv, state_ssd, state_ssd_conv, caches):
    bsz = x.shape[0]
    assert x.shape[1] == 1
    h = x.reshape(bsz, D_MODEL)
    tm = bsz
    new = [[] for _ in range(2 + N_ATT_GROUPS)]
    gdn_conv_t = jnp.transpose(state_gdn_conv, (0, 2, 1, 3))
    ssd_conv_t = jnp.transpose(state_ssd_conv, (0, 2, 1, 3))
    s_gdn = s_ssd = None
    for l in range(DEPTH):
        h = ffn_block(h, p['ffn1_norm'], p['ffn1_w_in'], p['ffn1_w_out'], l, tm)
        qkv, gate, z, xbc, small = proj_ab(h, p['mix_norm'], p['w_ab'], l, tm)
        q_rows, *kvrows = proj_c(h, p['mix_norm'], p['w_c'], l, p['q_norm'][l], p['k_norm'][l], tm, False)
        o_a, s_gdn, c_gdn = gdn_decode(qkv, gate, small, gdn_conv_t, state_gdn, s_gdn, l, p['gdn_conv_w'][l],
                                       p['gdn_a_log'][l], p['gdn_dt_bias'][l], p['gdn_norm_w'][l])
        o_b, s_ssd, c_ssd = ssd_decode(xbc, z, small, ssd_conv_t, state_ssd, s_ssd, l, p['ssd_conv_w'][l],
                                       p['ssd_conv_b'][l], p['ssd_a_log'][l], p['ssd_dt_bias'][l], p['ssd_d'][l],
                                       p['ssd_norm_w'][l])
        o_c = attn_decode(q_rows, kvrows, caches, l)
        h = merge_block(h, o_a, o_b, o_c, p['merge'], l, tm)
        h = ffn_block(h, p['ffn2_norm'], p['ffn2_w_in'], p['ffn2_w_out'], l, tm)
        kv_new = [kvrows[gi].reshape((bsz, 1) + KV_TAIL) for gi in range(N_ATT_GROUPS)]
        for lst, s in zip(new, (c_gdn, c_ssd, *kv_new)):
            lst.append(s)
    c_gdn, c_ssd, *kv_new = [jnp.stack(lst, axis=0) for lst in new]
    return h.reshape(bsz, 1, D_MODEL), [s_gdn, jnp.transpose(c_gdn, (0, 2, 1, 3)), s_ssd,
                                        jnp.transpose(c_ssd, (0, 2, 1, 3)), *kv_new]


def kernel(x_prompt, x_sample, state_gdn, state_gdn_conv, state_ssd, state_ssd_conv,
           cache_kv_w128, cache_kv_w512, cache_kv_w2048,
           ffn1_norm, ffn1_w_in, ffn1_w_out, mix_norm, w_in, gate_b,
           gdn_conv_w, gdn_a_log, gdn_dt_bias, gdn_norm_w,
           ssd_conv_w, ssd_conv_b, ssd_a_log, ssd_dt_bias, ssd_d, ssd_norm_w,
           q_norm, k_norm, w_br_a, w_br_b, w_br_c, w_out,
           ffn2_norm, ffn2_w_in, ffn2_w_out):
    w_ab, w_c, w_g = prep_weights(w_in)
    row = lambda a: a.reshape(DEPTH, 1, a.shape[-1])
    mix_norm_r = row(mix_norm)
    p = {'ffn1_norm': row(ffn1_norm), 'ffn1_w_in': ffn1_w_in.astype(BF16), 'ffn1_w_out': ffn1_w_out.astype(BF16),
         'mix_norm': mix_norm_r, 'w_ab': w_ab, 'w_c': w_c,
         'merge': (mix_norm_r, w_g, row(gate_b), w_br_a.astype(BF16), w_br_b.astype(BF16), w_br_c.astype(BF16),
                   w_out.astype(BF16)),
         'gdn_conv_w': gdn_conv_w, 'gdn_a_log': gdn_a_log, 'gdn_dt_bias': gdn_dt_bias, 'gdn_norm_w': gdn_norm_w,
         'ssd_conv_w': ssd_conv_w, 'ssd_conv_b': ssd_conv_b, 'ssd_a_log': ssd_a_log,
         'ssd_dt_bias': ssd_dt_bias, 'ssd_d': ssd_d, 'ssd_norm_w': ssd_norm_w,
         'q_norm': q_norm, 'k_norm': k_norm,
         'ffn2_norm': row(ffn2_norm), 'ffn2_w_in': ffn2_w_in.astype(BF16), 'ffn2_w_out': ffn2_w_out.astype(BF16)}
    y_prompt, st_p = prompt_trunk(x_prompt, p)
    y_sample, st_s = sample_trunk(x_sample, p, state_gdn, state_gdn_conv, state_ssd, state_ssd_conv,
                                  [cache_kv_w128, cache_kv_w512, cache_kv_w2048])
    gdn_p, gdn_conv_p, ssd_p, ssd_conv_p, kv128_p, kv512_p, kv2048_p = st_p
    gdn_s, gdn_conv_s, ssd_s, ssd_conv_s, kv128_s, kv512_s, kv2048_s = st_s
    return (y_prompt, y_sample, gdn_p, gdn_s, gdn_conv_p, gdn_conv_s, ssd_p, ssd_s, ssd_conv_p, ssd_conv_s,
            kv128_p, kv128_s, kv512_p, kv512_s, kv2048_p, kv2048_s)
```

```python
import functools
import math
import jax, jax.numpy as jnp
from jax import lax
import numpy as np
from jax.experimental import pallas as pl
from jax.experimental.pallas import tpu as pltpu

D_MODEL = 1024
BATCH = 8
SEQ = 2048
DEPTH = 4
DEC_BATCH = 128
DEC_SEQ = 1
PAST_LEN = 2048

CONV_WIDTH = 4
RMS_EPS = 1e-6
GDN_HEADS = 4
GDN_DK = 128
GDN_DV = 128
GDN_CHUNK = 64
GDN_QK = GDN_HEADS * GDN_DK
GDN_VW = GDN_HEADS * GDN_DV
GDN_CONV_DIM = 2 * GDN_QK + GDN_VW
SSD_HEADS = 8
SSD_HEADDIM = 64
SSD_GROUPS = 2
SSD_STATE = 128
SSD_CHUNK = 128
SSD_INNER = SSD_HEADS * SSD_HEADDIM
SSD_CONV_DIM = SSD_INNER + 2 * SSD_GROUPS * SSD_STATE
ATT_GROUPS = ((128, 1), (512, 4), (2048, 16))
N_ATT_GROUPS = len(ATT_GROUPS)
ATT_HEADS_PER_GROUP = 4
ATT_HEAD_DIM = 64
ATT_HEADS = N_ATT_GROUPS * ATT_HEADS_PER_GROUP
ATT_OUT = ATT_HEADS_PER_GROUP * ATT_HEAD_DIM
D_FF = ((8 * D_MODEL // 3 + 127) // 128) * 128
N_BRANCH = 3
IN_SPLITS = (GDN_CONV_DIM, GDN_VW, GDN_HEADS, GDN_HEADS, SSD_INNER, SSD_CONV_DIM, SSD_HEADS,
             3 * ATT_HEADS * ATT_HEAD_DIM, N_BRANCH * D_MODEL)
D_IN = sum(IN_SPLITS)

F32 = jnp.float32
BF16 = jnp.bfloat16
VMEM_LIMIT_BYTES = 56 * 1024 * 1024


def _layer_spec(stacked, layer):
    zeros = (0,) * (stacked.ndim - 1)
    return pl.BlockSpec((1,) + stacked.shape[1:], lambda *_: (layer,) + zeros, pipeline_mode=pl.Buffered(1))


def _ffn_kernel(h_ref, nw_ref, wi_ref, wo_ref, o_ref):
    h = h_ref[...]
    xn = h * lax.rsqrt(jnp.mean(h * h, axis=-1, keepdims=True) + RMS_EPS) * nw_ref[0]
    gu = jnp.dot(xn.astype(BF16), wi_ref[0], preferred_element_type=F32)
    g, up = gu[:, :D_FF], gu[:, D_FF:]
    act = (g * jax.nn.sigmoid(g) * up).astype(BF16)
    y = jnp.dot(act, wo_ref[0], preferred_element_type=F32)
    o_ref[...] = h + 0.5 * y


def ffn_block(h, norm_w, w_in_bf16, w_out_bf16, layer, tm):
    t = h.shape[0]
    assert t % tm == 0
    return pl.pallas_call(
        _ffn_kernel,
        out_shape=jax.ShapeDtypeStruct(h.shape, h.dtype),
        grid=(t // tm,),
        in_specs=[pl.BlockSpec((tm, D_MODEL), lambda i: (i, 0)),
                  _layer_spec(norm_w, layer), _layer_spec(w_in_bf16, layer), _layer_spec(w_out_bf16, layer)],
        out_specs=pl.BlockSpec((tm, D_MODEL), lambda i: (i, 0)),
        compiler_params=pltpu.CompilerParams(
            dimension_semantics=("arbitrary",), vmem_limit_bytes=VMEM_LIMIT_BYTES),
        name="ffn",
    )(h, norm_w, w_in_bf16, w_out_bf16)


_DN_BMM = (((2,), (1,)), ((0,), (0,)))
_DN_BMM_NT = (((2,), (2,)), ((0,), (0,)))


def _bmm(a, b):
    return lax.dot_general(a.astype(BF16), b.astype(BF16), _DN_BMM, preferred_element_type=F32)


def _bmm_nt(a, b):
    return lax.dot_general(a.astype(BF16), b.astype(BF16), _DN_BMM_NT, preferred_element_type=F32)


def _split2(x):
    hi = x.astype(BF16)
    return hi, (x - hi.astype(F32)).astype(BF16)


def _split3(x):
    x1 = x.astype(BF16)
    r1 = x - x1.astype(F32)
    x2 = r1.astype(BF16)
    x3 = (r1 - x2.astype(F32)).astype(BF16)
    return x1, x2, x3


def _bmm_hi(a, b):
    a1, a2 = _split2(a)
    b1, b2 = _split2(b)
    dot = functools.partial(lax.dot_general, dimension_numbers=_DN_BMM, preferred_element_type=F32)
    return dot(a1, b1) + (dot(a1, b2) + dot(a2, b1))


def _cumsum_rows(tri, x):
    dot = functools.partial(lax.dot_general, dimension_numbers=_DN_BMM, preferred_element_type=F32)
    x1, x2, x3 = _split3(x)
    return dot(tri, x1) + (dot(tri, x2) + dot(tri, x3))


def _softplus(x):
    return jnp.maximum(x, 0.0) + jnp.log(1.0 + jnp.exp(-jnp.abs(x)))


def _silu(x):
    return x * jax.nn.sigmoid(x)


GDN_INV_BLOCK = 16


def _unit_lower_solve(a, rhs, row, col, mm):
    c = a.shape[-1]
    eye = (row == col).astype(F32)
    same_blk = (row // GDN_INV_BLOCK) == (col // GDN_INV_BLOCK)
    d = jnp.where(same_blk, a, 0.0)
    n = a - d
    x = eye - d
    pw = d
    k = 2
    while k < GDN_INV_BLOCK:
        pw = mm(pw, pw)
        x = x + mm(x, pw)
        k *= 2
    m = mm(x, n)
    y = mm(x, rhs)
    nblk = c // GDN_INV_BLOCK
    q = eye - m
    pw = m
    k = 2
    while k < nblk:
        pw = mm(pw, pw)
        q = q + mm(q, pw)
        k *= 2
    return mm(q, y)


def _gdn_prompt_kernel(qkv_ref, gate_ref, small_ref, convw_ref, alog_ref, dtb_ref, normw_ref,
                       o_ref, s_ref, cst_ref, xp_ref):
    c = GDN_CHUNK
    nb = qkv_ref.shape[0]

    @pl.when(pl.program_id(0) == 0)
    def _():
        s_ref[...] = jnp.zeros_like(s_ref)
        xp_ref[:, 0:8, :] = jnp.zeros((nb, 8, GDN_CONV_DIM), F32)

    x = qkv_ref[...]
    xp_ref[:, 8:8 + c, :] = x
    w = convw_ref[...]
    y = (xp_ref[:, 5:5 + c, :] * w[0:1, :] + xp_ref[:, 6:6 + c, :] * w[1:2, :]
         + xp_ref[:, 7:7 + c, :] * w[2:3, :] + x * w[3:4, :])
    xp_ref[:, 0:8, :] = xp_ref[:, c:c + 8, :]
    cst_ref[...] = x[:, c - (CONV_WIDTH - 1):, :]
    y = _silu(y)

    small = small_ref[...]
    beta_all = jax.nn.sigmoid(small)
    g_all = -jnp.exp(alog_ref[...]) * _softplus(small + dtb_ref[...])
    row = lax.broadcasted_iota(jnp.int32, (c, c), 0)
    col = lax.broadcasted_iota(jnp.int32, (c, c), 1)
    incl = row >= col
    strict = row > col
    tri = jnp.broadcast_to(incl.astype(BF16), (nb, c, c))
    gam_all = _cumsum_rows(tri, g_all)
    gam_all_t = jnp.swapaxes(gam_all, 1, 2)
    egam_all = jnp.exp(gam_all)

    for h in range(GDN_HEADS):
        lo = h * GDN_DK
        q = y[:, :, lo:lo + GDN_DK]
        k = y[:, :, GDN_QK + lo:GDN_QK + lo + GDN_DK]
        v = y[:, :, 2 * GDN_QK + h * GDN_DV:2 * GDN_QK + (h + 1) * GDN_DV]
        q = q * (lax.rsqrt(jnp.sum(q * q, axis=-1, keepdims=True) + RMS_EPS) * GDN_DK ** -0.5)
        k = k * lax.rsqrt(jnp.sum(k * k, axis=-1, keepdims=True) + RMS_EPS)
        beta = beta_all[:, :, h:h + 1]
        gam = gam_all[:, :, GDN_HEADS + h:GDN_HEADS + h + 1]
        gam_row = gam_all_t[:, GDN_HEADS + h:GDN_HEADS + h + 1, :]
        egam = egam_all[:, :, GDN_HEADS + h:GDN_HEADS + h + 1]
        gam_last = gam_row[:, :, c - 1:c]
        dec = jnp.where(incl, jnp.exp(jnp.minimum(gam - gam_row, 0.0)), 0.0)
        kk = _bmm_nt(k, k)
        a_mat = jnp.where(strict, beta * kk * dec, 0.0)
        rhs = jnp.concatenate([v * beta, k * (beta * egam)], axis=-1)
        sol = _unit_lower_solve(a_mat, rhs, row, col, _bmm)
        u0, wk = sol[:, :, :GDN_DV], sol[:, :, GDN_DV:]
        qk = _bmm_nt(q, k) * dec
        q_dec = q * egam
        k_dec = k * jnp.exp(gam_last - gam)
        s = s_ref[:, h]
        u = u0 - _bmm(wk, s)
        o = _bmm(q_dec, s) + _bmm(qk, u)
        s_ref[:, h] = s * jnp.exp(gam_last) + _bmm(jnp.swapaxes(k_dec, 1, 2), u)
        o = o * lax.rsqrt(jnp.mean(o * o, axis=-1, keepdims=True) + RMS_EPS) * normw_ref[...]
        o_ref[:, :, h * GDN_DV:(h + 1) * GDN_DV] = o * _silu(gate_ref[:, :, h * GDN_DV:(h + 1) * GDN_DV])


def gdn_prompt(a_qkv, a_gate, small, conv_w, a_log, dt_bias, norm_w):
    bsz, L, _ = a_qkv.shape
    c = GDN_CHUNK
    assert L % c == 0
    lanes = small.shape[-1]
    alog_row = jnp.zeros((1, lanes), F32).at[0, GDN_HEADS:2 * GDN_HEADS].set(a_log)
    dtb_row = jnp.zeros((1, lanes), F32).at[0, GDN_HEADS:2 * GDN_HEADS].set(dt_bias)
    const2 = lambda n: (0, 0)
    return pl.pallas_call(
        _gdn_prompt_kernel,
        out_shape=(jax.ShapeDtypeStruct((bsz, L, GDN_VW), F32),
                   jax.ShapeDtypeStruct((bsz, GDN_HEADS, GDN_DK, GDN_DV), F32),
                   jax.ShapeDtypeStruct((bsz, CONV_WIDTH - 1, GDN_CONV_DIM), F32)),
        grid=(L // c,),
        in_specs=[
            pl.BlockSpec((bsz, c, GDN_CONV_DIM), lambda n: (0, n, 0)),
            pl.BlockSpec((bsz, c, GDN_VW), lambda n: (0, n, 0)),
            pl.BlockSpec((bsz, c, lanes), lambda n: (0, n, 0)),
            pl.BlockSpec((CONV_WIDTH, GDN_CONV_DIM), const2),
            pl.BlockSpec((1, lanes), const2),
            pl.BlockSpec((1, lanes), const2),
            pl.BlockSpec((1, GDN_DV), const2),
        ],
        out_specs=(
            pl.BlockSpec((bsz, c, GDN_VW), lambda n: (0, n, 0)),
            pl.BlockSpec((bsz, GDN_HEADS, GDN_DK, GDN_DV), lambda n: (0, 0, 0, 0)),
            pl.BlockSpec((bsz, CONV_WIDTH - 1, GDN_CONV_DIM), lambda n: (0, 0, 0)),
        ),
        scratch_shapes=[pltpu.VMEM((bsz, c + 8, GDN_CONV_DIM), F32)],
        compiler_params=pltpu.CompilerParams(
            dimension_semantics=("arbitrary",), vmem_limit_bytes=VMEM_LIMIT_BYTES),
        name="gdn_prompt",
    )(a_qkv, a_gate, small, conv_w, alog_row, dtb_row, norm_w.reshape(1, GDN_DV))


SSD_DT_LANE = 2 * GDN_HEADS
SSD_BC = SSD_GROUPS * SSD_STATE
SSD_HPG = SSD_HEADS // SSD_GROUPS


def _ssd_prompt_kernel(xbc_ref, z_ref, small_ref, convw_ref, convb_ref, alog_ref, dtb_ref, d_ref, normw_ref,
                       o_ref, h_ref, cst_ref, xp_ref):
    c = SSD_CHUNK
    nb = xbc_ref.shape[0]

    @pl.when(pl.program_id(0) == 0)
    def _():
        h_ref[...] = jnp.zeros_like(h_ref)
        xp_ref[:, 0:8, :] = jnp.zeros((nb, 8, SSD_CONV_DIM), F32)

    x = xbc_ref[...]
    xp_ref[:, 8:8 + c, :] = x
    w = convw_ref[...]
    y = (xp_ref[:, 5:5 + c, :] * w[0:1, :] + xp_ref[:, 6:6 + c, :] * w[1:2, :]
         + xp_ref[:, 7:7 + c, :] * w[2:3, :] + x * w[3:4, :]) + convb_ref[...]
    xp_ref[:, 0:8, :] = xp_ref[:, c:c + 8, :]
    cst_ref[...] = x[:, c - (CONV_WIDTH - 1):, :]
    y = _silu(y)
    xs = y[:, :, :SSD_INNER]
    bm = y[:, :, SSD_INNER:SSD_INNER + SSD_BC]
    cm = y[:, :, SSD_INNER + SSD_BC:]

    dt_all = _softplus(small_ref[...] + dtb_ref[...])
    da_all = dt_all * (-jnp.exp(alog_ref[...]))
    row = lax.broadcasted_iota(jnp.int32, (c, c), 0)
    col = lax.broadcasted_iota(jnp.int32, (c, c), 1)
    incl = row >= col
    tri = jnp.broadcast_to(incl.astype(BF16), (nb, c, c))
    acum_all = _cumsum_rows(tri, da_all)
    acum_all_t = jnp.swapaxes(acum_all, 1, 2)
    eacum_all = jnp.exp(acum_all)

    ys = []
    for g in range(SSD_GROUPS):
        bg = bm[:, :, g * SSD_STATE:(g + 1) * SSD_STATE]
        cg = cm[:, :, g * SSD_STATE:(g + 1) * SSD_STATE]
        scores = _bmm_nt(cg, bg)
        for hh in range(SSD_HPG):
            h = g * SSD_HPG + hh
            ln = SSD_DT_LANE + h
            dtc = dt_all[:, :, ln:ln + 1]
            ac = acum_all[:, :, ln:ln + 1]
            ar = acum_all_t[:, ln:ln + 1, :]
            alast = ar[:, :, c - 1:c]
            lmat = jnp.where(incl, jnp.exp(jnp.minimum(ac - ar, 0.0)), 0.0)
            xd = xs[:, :, h * SSD_HEADDIM:(h + 1) * SSD_HEADDIM] * dtc
            y_diag = _bmm(scores * lmat, xd)
            hin = h_ref[:, h]
            y_off = _bmm_nt(cg * eacum_all[:, :, ln:ln + 1], hin)
            states = _bmm(jnp.swapaxes(xd, 1, 2), bg * jnp.exp(alast - ac))
            h_ref[:, h] = hin * jnp.exp(alast) + states
            ys.append(y_diag + y_off)
    yf = jnp.concatenate(ys, axis=-1) + d_ref[...] * xs
    yf = yf * _silu(z_ref[...])
    gw = SSD_INNER // SSD_GROUPS
    for g in range(SSD_GROUPS):
        seg = yf[:, :, g * gw:(g + 1) * gw]
        o_ref[:, :, g * gw:(g + 1) * gw] = (seg * lax.rsqrt(jnp.mean(seg * seg, axis=-1, keepdims=True) + RMS_EPS)
                                            * normw_ref[:, g * gw:(g + 1) * gw])


def ssd_prompt(b_xbc, b_z, small, conv_w, conv_b, a_log, dt_bias, d_skip, norm_w):
    bsz, L, _ = b_xbc.shape
    c = SSD_CHUNK
    assert L % c == 0
    lanes = small.shape[-1]
    sl = slice(SSD_DT_LANE, SSD_DT_LANE + SSD_HEADS)
    alog_row = jnp.zeros((1, lanes), F32).at[0, sl].set(a_log)
    dtb_row = jnp.zeros((1, lanes), F32).at[0, sl].set(dt_bias)
    d_row = jnp.repeat(d_skip, SSD_HEADDIM).reshape(1, SSD_INNER)
    const2 = lambda n: (0, 0)
    return pl.pallas_call(
        _ssd_prompt_kernel,
        out_shape=(jax.ShapeDtypeStruct((bsz, L, SSD_INNER), F32),
                   jax.ShapeDtypeStruct((bsz, SSD_HEADS, SSD_HEADDIM, SSD_STATE), F32),
                   jax.ShapeDtypeStruct((bsz, CONV_WIDTH - 1, SSD_CONV_DIM), F32)),
        grid=(L // c,),
        in_specs=[
            pl.BlockSpec((bsz, c, SSD_CONV_DIM), lambda n: (0, n, 0)),
            pl.BlockSpec((bsz, c, SSD_INNER), lambda n: (0, n, 0)),
            pl.BlockSpec((bsz, c, lanes), lambda n: (0, n, 0)),
            pl.BlockSpec((CONV_WIDTH, SSD_CONV_DIM), const2),
            pl.BlockSpec((1, SSD_CONV_DIM), const2),
            pl.BlockSpec((1, lanes), const2),
            pl.BlockSpec((1, lanes), const2),
            pl.BlockSpec((1, SSD_INNER), const2),
            pl.BlockSpec((1, SSD_INNER), const2),
        ],
        out_specs=(
            pl.BlockSpec((bsz, c, SSD_INNER), lambda n: (0, n, 0)),
            pl.BlockSpec((bsz, SSD_HEADS, SSD_HEADDIM, SSD_STATE), lambda n: (0, 0, 0, 0)),
            pl.BlockSpec((bsz, CONV_WIDTH - 1, SSD_CONV_DIM), lambda n: (0, 0, 0)),
        ),
        scratch_shapes=[pltpu.VMEM((bsz, c + 8, SSD_CONV_DIM), F32)],
        compiler_params=pltpu.CompilerParams(
            dimension_semantics=("arbitrary",), vmem_limit_bytes=VMEM_LIMIT_BYTES),
        name="ssd_prompt",
    )(b_xbc, b_z, small, conv_w, conv_b.reshape(1, SSD_CONV_DIM), alog_row, dtb_row, d_row,
      norm_w.reshape(1, SSD_INNER))


ATT_BAND = 128
ATT_MASKED = -1e30
assert all(w // d == ATT_BAND for w, d in ATT_GROUPS)


def _alibi_slopes_np():
    idx = np.arange(1, ATT_HEADS + 1, dtype=np.float32)
    return np.exp2(-8.0 * idx / ATT_HEADS).astype(np.float32).reshape(N_ATT_GROUPS, ATT_HEADS_PER_GROUP)


def _attn_bias(gi, first):
    nk = 2 * ATT_BAND
    i = lax.broadcasted_iota(jnp.int32, (ATT_BAND, nk), 0)
    j = lax.broadcasted_iota(jnp.int32, (ATT_BAND, nk), 1)
    du = i + ATT_BAND - j
    duf = du.astype(F32)
    dil = ATT_GROUPS[gi][1]
    slopes = _alibi_slopes_np()[gi]
    if first:
        inside = jnp.minimum(du, j - ATT_BAND) >= 0
    else:
        inside = jnp.abs(2 * du - ATT_BAND) <= ATT_BAND
    return [jnp.where(inside, duf * float(-slopes[hh] * dil * LOG2E), ATT_MASKED)
            for hh in range(ATT_HEADS_PER_GROUP)]


ATT_LANE_CHUNKS = ATT_OUT // 128
ATT_HEADS_PER_CHUNK = 128 // ATT_HEAD_DIM
ATT_UNROLL = (5, 4, 4)
LOG2E = math.log2(math.e)
LN2 = math.log(2.0)


def _attn_block(q, keys, vals, bias):
    nk = keys.shape[0]
    low = lax.broadcasted_iota(jnp.int32, (ATT_BAND, 128), 1) < ATT_HEAD_DIM
    q2 = q * LOG2E
    kb = keys.astype(BF16)
    va = jnp.concatenate([vals.astype(BF16), jnp.ones((nk, 128), BF16)], axis=1)
    res = []
    for hh in range(ATT_HEADS_PER_CHUNK):
        qh = jnp.where(low if hh == 0 else jnp.logical_not(low), q2, 0.0).astype(BF16)
        s = lax.dot_general(qh, kb, (((1,), (1,)), ((), ())), preferred_element_type=F32) + bias[hh]
        m = jnp.max(s, axis=-1, keepdims=True)
        p = jnp.exp2(s - m)
        oa = jnp.dot(p.astype(BF16), va, preferred_element_type=F32)
        l = oa[:, 128:]
        res.append((oa[:, :128] * (1.0 / l), (m + jnp.log2(l)) * LN2))
    return jnp.where(low, res[0][0], res[1][0]), jnp.where(low, res[0][1], res[1][1])


def _attn_prompt_kernel(q_ref, kv_ref, o_ref, lse_ref):
    g = pl.program_id(0)
    L = q_ref.shape[2]
    nc = ATT_LANE_CHUNKS

    for gi, (win, dil) in enumerate(ATT_GROUPS):
        @pl.when(g == gi)
        def _(gi=gi, win=win, dil=dil):
            nblk = L // win

            def rows(start):
                return pl.ds(start, ATT_BAND, stride=dil) if dil > 1 else pl.ds(start, ATT_BAND)

            def unit(cur, prev, bias):
                for c in range(nc):
                    hs = slice(c * ATT_HEADS_PER_CHUNK, (c + 1) * ATT_HEADS_PER_CHUNK)
                    keys = jnp.concatenate([kv_ref[0, c, prev, :], kv_ref[0, c, cur, :]], axis=0)
                    vals = jnp.concatenate([kv_ref[0, nc + c, prev, :], kv_ref[0, nc + c, cur, :]], axis=0)
                    o, lse = _attn_block(q_ref[0, c, cur, :], keys, vals, bias[hs])
                    o_ref[0, c, cur, :] = o
                    lse_ref[0, c, cur, :] = lse

            bias0 = _attn_bias(gi, True)

            def body0(r, carry):
                unit(rows(r), rows(r), bias0)
                return carry

            lax.fori_loop(0, dil, body0, 0, unroll=min(dil, ATT_UNROLL[gi]))
            if nblk > 1:
                bias1 = _attn_bias(gi, False)

                def body(u, carry):
                    start = (u // dil + 1) * win + u % dil
                    unit(rows(start), rows(start - win), bias1)
                    return carry

                lax.fori_loop(0, (nblk - 1) * dil, body, 0, unroll=ATT_UNROLL[gi])


def attn_prompt(q5, kv5, bsz):
    ng, nc, _, lanes = q5.shape
    L = SEQ
    assert all(L % w == 0 for w, _ in ATT_GROUPS)
    blk = lambda n: pl.BlockSpec((1, n, L, lanes), lambda g, b: (g, 0, b, 0))
    out = jax.ShapeDtypeStruct((ng, nc, bsz * L, lanes), F32)
    return pl.pallas_call(
        _attn_prompt_kernel,
        out_shape=(out, out),
        grid=(ng, bsz),
        in_specs=[blk(nc), blk(2 * nc)],
        out_specs=(blk(nc), blk(nc)),
        compiler_params=pltpu.CompilerParams(
            dimension_semantics=("arbitrary", "arbitrary"), vmem_limit_bytes=VMEM_LIMIT_BYTES),
        name="attn_prompt",
    )(q5, kv5)


SMALL_LANES = 128
_OFF = np.cumsum((0,) + IN_SPLITS).tolist()
PROJ_AB_WIDTHS = (GDN_CONV_DIM, GDN_VW, SSD_INNER, SSD_CONV_DIM, SMALL_LANES)
PROJ_AB_N = sum(PROJ_AB_WIDTHS)
PROJ_C_N = 3 * ATT_HEADS * ATT_HEAD_DIM
ATT_Q_N = ATT_HEADS * ATT_HEAD_DIM


def prep_weights(w_in):
    o = _OFF
    wt = jnp.transpose(w_in, (0, 2, 1))
    pad = jnp.zeros((w_in.shape[0], SMALL_LANES - 2 * GDN_HEADS - SSD_HEADS, w_in.shape[1]), w_in.dtype)
    w_ab = jnp.concatenate([wt[:, o[0]:o[1]], wt[:, o[1]:o[2]], wt[:, o[4]:o[5]], wt[:, o[5]:o[6]],
                            wt[:, o[2]:o[4]], wt[:, o[6]:o[7]], pad], axis=1)
    back = lambda a: jnp.transpose(a.astype(BF16), (0, 2, 1))
    return back(w_ab), back(wt[:, o[7]:o[8]]), back(wt[:, o[8]:o[9]])


def _rmsnorm_rows(h, w_row):
    return h * lax.rsqrt(jnp.mean(h * h, axis=-1, keepdims=True) + RMS_EPS) * w_row


def _proj_ab_kernel(h_ref, nw_ref, w_ref, qkv_ref, gate_ref, z_ref, xbc_ref, small_ref):
    u = _rmsnorm_rows(h_ref[...], nw_ref[0]).astype(BF16)
    y = jnp.dot(u, w_ref[0], preferred_element_type=F32)
    lo = 0
    for ref, wd in zip((qkv_ref, gate_ref, z_ref, xbc_ref, small_ref), PROJ_AB_WIDTHS):
        ref[...] = y[:, lo:lo + wd]
        lo += wd


def proj_ab(h, norm_w, w_ab, layer, tm):
    t = h.shape[0]
    assert t % tm == 0
    return pl.pallas_call(
        _proj_ab_kernel,
        out_shape=tuple(jax.ShapeDtypeStruct((t, wd), F32) for wd in PROJ_AB_WIDTHS),
        grid=(t // tm,),
        in_specs=[pl.BlockSpec((tm, D_MODEL), lambda i: (i, 0)), _layer_spec(norm_w, layer), _layer_spec(w_ab, layer)],
        out_specs=tuple(pl.BlockSpec((tm, wd), lambda i: (i, 0)) for wd in PROJ_AB_WIDTHS),
        compiler_params=pltpu.CompilerParams(
            dimension_semantics=("arbitrary",), vmem_limit_bytes=VMEM_LIMIT_BYTES),
        name="proj_ab",
    )(h, norm_w, w_ab)


def _head_rmsnorm(x, w_row, seg):
    outs = []
    for c in range(x.shape[-1] // 128):
        xc = x[:, c * 128:(c + 1) * 128]
        hi, lo = _split2(xc * xc)
        ss = jnp.dot(hi, seg, preferred_element_type=F32) + jnp.dot(lo, seg, preferred_element_type=F32)
        outs.append(xc * lax.rsqrt(ss * (1.0 / ATT_HEAD_DIM) + RMS_EPS) * w_row)
    return outs


def _proj_c_kernel(h_ref, nw_ref, w_ref, qw_ref, kw_ref, *out_refs, chunked):
    u = _rmsnorm_rows(h_ref[...], nw_ref[0]).astype(BF16)
    y = jnp.dot(u, w_ref[0], preferred_element_type=F32)
    r = lax.broadcasted_iota(jnp.int32, (128, 128), 0) // ATT_HEAD_DIM
    c = lax.broadcasted_iota(jnp.int32, (128, 128), 1) // ATT_HEAD_DIM
    seg = (r == c).astype(BF16)
    qn = [t * ATT_HEAD_DIM ** -0.5 for t in _head_rmsnorm(y[:, :ATT_Q_N], qw_ref[...], seg)]
    kn = _head_rmsnorm(y[:, ATT_Q_N:2 * ATT_Q_N], kw_ref[...], seg)
    v = y[:, 2 * ATT_Q_N:]
    nc = ATT_LANE_CHUNKS
    if chunked:
        q5_ref, kv5_ref, *kvrows_refs = out_refs
    else:
        q_ref, *kvrows_refs = out_refs
    for g in range(N_ATT_GROUPS):
        for cc in range(nc):
            i = g * nc + cc
            vc = v[:, i * 128:(i + 1) * 128]
            kvrows_refs[g][:, cc * 128:(cc + 1) * 128] = kn[i]
            kvrows_refs[g][:, ATT_OUT + cc * 128:ATT_OUT + (cc + 1) * 128] = vc
            if chunked:
                q5_ref[g, cc] = qn[i]
                kv5_ref[g, cc] = kn[i]
                kv5_ref[g, nc + cc] = vc
            else:
                q_ref[:, i * 128:(i + 1) * 128] = qn[i]


def proj_c(h, norm_w, w_c, layer, q_norm, k_norm, tm, chunked):
    t = h.shape[0]
    assert t % tm == 0
    const = dict(pipeline_mode=pl.Buffered(1))
    ng, nc = N_ATT_GROUPS, ATT_LANE_CHUNKS
    if chunked:
        out_shape = [jax.ShapeDtypeStruct((ng, nc, t, 128), F32), jax.ShapeDtypeStruct((ng, 2 * nc, t, 128), F32)]
        out_specs = [pl.BlockSpec((ng, nc, tm, 128), lambda i: (0, 0, i, 0)),
                     pl.BlockSpec((ng, 2 * nc, tm, 128), lambda i: (0, 0, i, 0))]
    else:
        out_shape = [jax.ShapeDtypeStruct((t, ATT_Q_N), F32)]
        out_specs = [pl.BlockSpec((tm, ATT_Q_N), lambda i: (i, 0))]
    out_shape += [jax.ShapeDtypeStruct((t, 2 * ATT_OUT), F32)] * ng
    out_specs += [pl.BlockSpec((tm, 2 * ATT_OUT), lambda i: (i, 0))] * ng
    tile2 = lambda w: jnp.tile(w, 128 // ATT_HEAD_DIM).reshape(1, 128)
    return pl.pallas_call(
        functools.partial(_proj_c_kernel, chunked=chunked),
        out_shape=tuple(out_shape),
        grid=(t // tm,),
        in_specs=[
            pl.BlockSpec((tm, D_MODEL), lambda i: (i, 0)),
            _layer_spec(norm_w, layer),
            _layer_spec(w_c, layer),
            pl.BlockSpec((1, 128), lambda i: (0, 0), **const),
            pl.BlockSpec((1, 128), lambda i: (0, 0), **const),
        ],
        out_specs=tuple(out_specs),
        compiler_params=pltpu.CompilerParams(
            dimension_semantics=("arbitrary",), vmem_limit_bytes=VMEM_LIMIT_BYTES),
        name="proj_c",
    )(h, norm_w, w_c, tile2(q_norm), tile2(k_norm))


ROW_SUBTILE = 256


def _merge_kernel(h_ref, oa_ref, ob_ref, *refs, chunked):
    att_refs, refs = (refs[:2], refs[2:]) if chunked else (refs[:1], refs[1:])
    nw_ref, wg_ref, gb_ref, wa_ref, wb_ref, wc_ref, wo_ref, out_ref = refs
    tm = h_ref.shape[0]
    sub = min(tm, ROW_SUBTILE)
    for r0 in range(0, tm, sub):
        rs = slice(r0, r0 + sub)
        if chunked:
            o5_ref, lse5_ref = att_refs
            ocs = []
            for cc in range(ATT_LANE_CHUNKS):
                ls = [lse5_ref[g, cc, rs, :] for g in range(N_ATT_GROUPS)]
                m = functools.reduce(jnp.maximum, ls)
                es = [jnp.exp(l - m) for l in ls]
                num = sum(e * o5_ref[g, cc, rs, :] for g, e in enumerate(es))
                ocs.append(num * (1.0 / sum(es)))
            o_c = jnp.concatenate(ocs, axis=-1)
        else:
            o_c = att_refs[0][rs, :]
        h = h_ref[rs, :]
        u = _rmsnorm_rows(h, nw_ref[0]).astype(BF16)
        merged = None
        for i, (o, w_ref) in enumerate(((oa_ref[rs, :], wa_ref), (ob_ref[rs, :], wb_ref), (o_c, wc_ref))):
            sl = slice(i * D_MODEL, (i + 1) * D_MODEL)
            gate = jax.nn.sigmoid(jnp.dot(u, wg_ref[0, :, sl], preferred_element_type=F32) + gb_ref[0, :, sl])
            term = gate * jnp.dot(o.astype(BF16), w_ref[0], preferred_element_type=F32)
            merged = term if merged is None else merged + term
        out_ref[rs, :] = h + jnp.dot(merged.astype(BF16), wo_ref[0], preferred_element_type=F32)


def merge_block(h, o_a, o_b, o_att, params, layer, tm):
    t = h.shape[0]
    assert t % tm == 0
    chunked = isinstance(o_att, tuple)
    rows = lambda wd: pl.BlockSpec((tm, wd), lambda i: (i, 0))
    if chunked:
        att_specs = [pl.BlockSpec((N_ATT_GROUPS, ATT_LANE_CHUNKS, tm, 128), lambda i: (0, 0, i, 0))] * 2
        att_args = list(o_att)
    else:
        att_specs = [rows(ATT_OUT)]
        att_args = [o_att]
    return pl.pallas_call(
        functools.partial(_merge_kernel, chunked=chunked),
        out_shape=jax.ShapeDtypeStruct(h.shape, F32),
        grid=(t // tm,),
        in_specs=[rows(D_MODEL), rows(GDN_VW), rows(SSD_INNER)] + att_specs + [_layer_spec(a, layer) for a in params],
        out_specs=rows(D_MODEL),
        compiler_params=pltpu.CompilerParams(
            dimension_semantics=("arbitrary",), vmem_limit_bytes=VMEM_LIMIT_BYTES),
        name="merge",
    )(h, o_a, o_b, *att_args, *params)


DEC_TB = 8


def _conv_step(cst_ref, x, w_ref):
    prev = [cst_ref[0, i] for i in range(CONV_WIDTH - 1)]
    y = x * w_ref[CONV_WIDTH - 1:CONV_WIDTH, :]
    for i, pr in enumerate(prev):
        y = y + pr * w_ref[i:i + 1, :]
    return y, prev[1:] + [x]


def _gdn_decode_kernel(qkv_ref, gate_ref, small_ref, cst_ref, s_ref, carried_ref, convw_ref, alog_ref, dtb_ref,
                       normw_ref, o_ref, so_ref, csto_ref):
    del carried_ref
    tb = qkv_ref.shape[0]
    x = qkv_ref[...]
    y, new_cst = _conv_step(cst_ref, x, convw_ref)
    for i, r in enumerate(new_cst):
        csto_ref[i] = r
    y = _silu(y)
    small = small_ref[...]
    beta_all = jax.nn.sigmoid(small)
    eg_all = jnp.exp(-jnp.exp(alog_ref[...]) * _softplus(small + dtb_ref[...]))
    lanes = small.shape[-1]
    r = lax.broadcasted_iota(jnp.int32, (lanes, GDN_HEADS * GDN_DV), 0)
    c = lax.broadcasted_iota(jnp.int32, (lanes, GDN_HEADS * GDN_DV), 1)
    sel = (r == GDN_HEADS + c // GDN_DV).astype(BF16)
    eg_wide = sum(jnp.dot(t, sel, preferred_element_type=F32) for t in _split3(eg_all))
    tok = lax.broadcasted_iota(jnp.int32, (tb, 1), 0)
    tok2 = lax.broadcasted_iota(jnp.int32, (2 * tb, 1), 0)
    for h in range(GDN_HEADS):
        lo = h * GDN_DK
        q = y[:, lo:lo + GDN_DK]
        k = y[:, GDN_QK + lo:GDN_QK + lo + GDN_DK]
        v = y[:, 2 * GDN_QK + h * GDN_DV:2 * GDN_QK + (h + 1) * GDN_DV]
        q = q * (lax.rsqrt(jnp.sum(q * q, axis=-1, keepdims=True) + RMS_EPS) * GDN_DK ** -0.5)
        k = k * lax.rsqrt(jnp.sum(k * k, axis=-1, keepdims=True) + RMS_EPS)
        beta = beta_all[:, h:h + 1]
        eg = eg_all[:, GDN_HEADS + h:GDN_HEADS + h + 1]
        qk = jnp.sum(q * k, axis=-1, keepdims=True)
        kq = jnp.concatenate([k, q], axis=0).astype(BF16)
        kq_s = jnp.zeros((2 * tb, GDN_DV), F32)
        for t in range(tb):
            r = jnp.dot(kq, s_ref[0, t, h].astype(BF16), preferred_element_type=F32)
            kq_s = jnp.where(tok2 % tb == t, r, kq_s)
        u = beta * (v - eg * kq_s[:tb])
        o = eg * kq_s[tb:] + qk * u
        kb = k.astype(BF16)
        for t in range(tb):
            outer = lax.dot_general(kb, jnp.where(tok == t, u, 0.0).astype(BF16), (((0,), (0,)), ((), ())),
                                    preferred_element_type=F32)
            so_ref[0, t, h] = s_ref[0, t, h] * eg_wide[t:t + 1, h * GDN_DV:(h + 1) * GDN_DV] + outer
        o = o * lax.rsqrt(jnp.mean(o * o, axis=-1, keepdims=True) + RMS_EPS) * normw_ref[...]
        o_ref[:, h * GDN_DV:(h + 1) * GDN_DV] = o * _silu(gate_ref[:, h * GDN_DV:(h + 1) * GDN_DV])


def _carried(new_states, shape, operand_index):
    if new_states is None:
        return jnp.zeros((8, 128), F32), {}
    assert new_states.shape == shape
    return new_states, {operand_index: 1}


def gdn_decode(a_qkv, a_gate, small, conv_state_t, state, new_states, layer, conv_w, a_log, dt_bias, norm_w):
    bsz = a_qkv.shape[0]
    prev, aliases = _carried(new_states, state.shape, 5)
    tb = DEC_TB
    lanes = small.shape[-1]
    alog_row = jnp.zeros((1, lanes), F32).at[0, GDN_HEADS:2 * GDN_HEADS].set(a_log)
    dtb_row = jnp.zeros((1, lanes), F32).at[0, GDN_HEADS:2 * GDN_HEADS].set(dt_bias)
    nprev = CONV_WIDTH - 1
    rows = lambda wd: pl.BlockSpec((tb, wd), lambda i: (i, 0))
    const2 = lambda i: (0, 0)
    return pl.pallas_call(
        _gdn_decode_kernel,
        out_shape=(jax.ShapeDtypeStruct((bsz, GDN_VW), F32),
                   jax.ShapeDtypeStruct(state.shape, F32),
                   jax.ShapeDtypeStruct((nprev, bsz, GDN_CONV_DIM), F32)),
        grid=(bsz // tb,),
        in_specs=[rows(GDN_CONV_DIM), rows(GDN_VW), rows(lanes),
                  pl.BlockSpec((1, nprev, tb, GDN_CONV_DIM), lambda i: (layer, 0, i, 0)),
                  pl.BlockSpec((1, tb, GDN_HEADS, GDN_DK, GDN_DV), lambda i: (layer, i, 0, 0, 0)),
                  pl.BlockSpec(memory_space=pl.ANY),
                  pl.BlockSpec((CONV_WIDTH, GDN_CONV_DIM), const2),
                  pl.BlockSpec((1, lanes), const2), pl.BlockSpec((1, lanes), const2),
                  pl.BlockSpec((1, GDN_DV), const2)],
        out_specs=(rows(GDN_VW),
                   pl.BlockSpec((1, tb, GDN_HEADS, GDN_DK, GDN_DV), lambda i: (layer, i, 0, 0, 0)),
                   pl.BlockSpec((nprev, tb, GDN_CONV_DIM), lambda i: (0, i, 0))),
        input_output_aliases=aliases,
        compiler_params=pltpu.CompilerParams(
            dimension_semantics=("arbitrary",), vmem_limit_bytes=VMEM_LIMIT_BYTES),
        name="gdn_decode",
    )(a_qkv, a_gate, small, conv_state_t, state, prev, conv_w, alog_row, dtb_row, norm_w.reshape(1, GDN_DV))


def _ssd_decode_kernel(xbc_ref, z_ref, small_ref, cst_ref, h_ref, carried_ref, convw_ref, convb_ref, alog_ref,
                       dtb_ref, d_ref, normw_ref, o_ref, ho_ref, csto_ref):
    del carried_ref
    tb = xbc_ref.shape[0]
    x = xbc_ref[...]
    y, new_cst = _conv_step(cst_ref, x, convw_ref)
    for i, r in enumerate(new_cst):
        csto_ref[i] = r
    y = _silu(y + convb_ref[...])
    xs = y[:, :SSD_INNER]
    bm = y[:, SSD_INNER:SSD_INNER + SSD_BC]
    cm = y[:, SSD_INNER + SSD_BC:]
    dt_all = _softplus(small_ref[...] + dtb_ref[...])
    eda_all = jnp.exp(dt_all * (-jnp.exp(alog_ref[...])))
    lanes = dt_all.shape[-1]

    def spread(v, width):
        r = lax.broadcasted_iota(jnp.int32, (lanes, SSD_HEADS * width), 0)
        c = lax.broadcasted_iota(jnp.int32, (lanes, SSD_HEADS * width), 1)
        sel = (r == SSD_DT_LANE + c // width).astype(BF16)
        return sum(jnp.dot(t, sel, preferred_element_type=F32) for t in _split3(v))

    eda_wide = spread(eda_all, SSD_STATE)
    xd = xs * spread(dt_all, SSD_HEADDIM)
    tok = lax.broadcasted_iota(jnp.int32, (tb, 1), 0)
    gp = SSD_HPG * SSD_HEADDIM
    ys = []
    for g in range(SSD_GROUPS):
        bg = bm[:, g * SSD_STATE:(g + 1) * SSD_STATE]
        cg = cm[:, g * SSD_STATE:(g + 1) * SSD_STATE].astype(BF16)
        xd_g = xd[:, g * gp:(g + 1) * gp].astype(BF16)
        y_g = jnp.zeros((tb, gp), F32)
        for t in range(tb):
            outer = lax.dot_general(xd_g, jnp.where(tok == t, bg, 0.0).astype(BF16), (((0,), (0,)), ((), ())),
                                    preferred_element_type=F32)
            hns = []
            for hh in range(SSD_HPG):
                h = g * SSD_HPG + hh
                hn = (h_ref[0, t, h] * eda_wide[t:t + 1, h * SSD_STATE:(h + 1) * SSD_STATE]
                      + outer[hh * SSD_HEADDIM:(hh + 1) * SSD_HEADDIM])
                ho_ref[0, t, h] = hn
                hns.append(hn.astype(BF16))
            y_t = lax.dot_general(cg, jnp.concatenate(hns, axis=0), (((1,), (1,)), ((), ())),
                                  preferred_element_type=F32)
            y_g = jnp.where(tok == t, y_t, y_g)
        ys.append(y_g)
    yf = jnp.concatenate(ys, axis=-1) + d_ref[...] * xs
    yf = yf * _silu(z_ref[...])
    gw = SSD_INNER // SSD_GROUPS
    for g in range(SSD_GROUPS):
        seg = yf[:, g * gw:(g + 1) * gw]
        o_ref[:, g * gw:(g + 1) * gw] = (seg * lax.rsqrt(jnp.mean(seg * seg, axis=-1, keepdims=True) + RMS_EPS)
                                         * normw_ref[:, g * gw:(g + 1) * gw])


def ssd_decode(b_xbc, b_z, small, conv_state_t, state, new_states, layer, conv_w, conv_b, a_log, dt_bias, d_skip,
               norm_w):
    bsz = b_xbc.shape[0]
    prev, aliases = _carried(new_states, state.shape, 5)
    tb = DEC_TB
    lanes = small.shape[-1]
    sl = slice(SSD_DT_LANE, SSD_DT_LANE + SSD_HEADS)
    alog_row = jnp.zeros((1, lanes), F32).at[0, sl].set(a_log)
    dtb_row = jnp.zeros((1, lanes), F32).at[0, sl].set(dt_bias)
    d_row = jnp.repeat(d_skip, SSD_HEADDIM).reshape(1, SSD_INNER)
    nprev = CONV_WIDTH - 1
    rows = lambda wd: pl.BlockSpec((tb, wd), lambda i: (i, 0))
    const2 = lambda i: (0, 0)
    return pl.pallas_call(
        _ssd_decode_kernel,
        out_shape=(jax.ShapeDtypeStruct((bsz, SSD_INNER), F32),
                   jax.ShapeDtypeStruct(state.shape, F32),
                   jax.ShapeDtypeStruct((nprev, bsz, SSD_CONV_DIM), F32)),
        grid=(bsz // tb,),
        in_specs=[rows(SSD_CONV_DIM), rows(SSD_INNER), rows(lanes),
                  pl.BlockSpec((1, nprev, tb, SSD_CONV_DIM), lambda i: (layer, 0, i, 0)),
                  pl.BlockSpec((1, tb, SSD_HEADS, SSD_HEADDIM, SSD_STATE), lambda i: (layer, i, 0, 0, 0)),
                  pl.BlockSpec(memory_space=pl.ANY),
                  pl.BlockSpec((CONV_WIDTH, SSD_CONV_DIM), const2), pl.BlockSpec((1, SSD_CONV_DIM), const2),
                  pl.BlockSpec((1, lanes), const2), pl.BlockSpec((1, lanes), const2),
                  pl.BlockSpec((1, SSD_INNER), const2), pl.BlockSpec((1, SSD_INNER), const2)],
        out_specs=(rows(SSD_INNER),
                   pl.BlockSpec((1, tb, SSD_HEADS, SSD_HEADDIM, SSD_STATE), lambda i: (layer, i, 0, 0, 0)),
                   pl.BlockSpec((nprev, tb, SSD_CONV_DIM), lambda i: (0, i, 0))),
        input_output_aliases=aliases,
        compiler_params=pltpu.CompilerParams(
            dimension_semantics=("arbitrary",), vmem_limit_bytes=VMEM_LIMIT_BYTES),
        name="ssd_decode",
    )(b_xbc, b_z, small, conv_state_t, state, prev, conv_w, conv_b.reshape(1, SSD_CONV_DIM),
      alog_row, dtb_row, d_row, norm_w.reshape(1, SSD_INNER))


ATT_DEC_TB = 8
ATT_DEC_TOK = 2


def _attn_decode_kernel(q_ref, kvn0_ref, kvn1_ref, kvn2_ref, c0_ref, c1_ref, c2_ref, o_ref, acc_ref):
    tb = q_ref.shape[0]
    ntok = c0_ref.shape[1]
    hd = ATT_HEAD_DIM
    k = pl.program_id(1)
    slopes = _alibi_slopes_np()
    q_t = q_ref[...].T
    kvn_t = [ref[...].T for ref in (kvn0_ref, kvn1_ref, kvn2_ref)]
    lane_t = lax.broadcasted_iota(jnp.int32, (1, tb), 1)

    def column(x_t, tok):
        return jnp.sum(jnp.where(lane_t == tok, x_t, 0.0), axis=1, keepdims=True)

    @pl.when(k == 0)
    def _():
        acc_ref[...] = jnp.zeros_like(acc_ref)

    acc = acc_ref[...]
    lane = lax.broadcasted_iota(jnp.int32, acc_ref.shape, 1)
    for tt in range(ntok):
        tok = k * ntok + tt
        q_col = column(q_t, tok)
        og = [[None] * N_ATT_GROUPS for _ in range(ATT_HEADS_PER_GROUP)]
        lg = [[None] * N_ATT_GROUPS for _ in range(ATT_HEADS_PER_GROUP)]
        for gi, (c_ref, (win, dil)) in enumerate(zip((c0_ref, c1_ref, c2_ref), ATT_GROUPS)):
            kvn_col = column(kvn_t[gi], tok)
            j = lax.broadcasted_iota(jnp.int32, (1, win), 1)
            dist = win - j
            is_key = (dist % dil) == 0
            for hh in range(ATT_HEADS_PER_GROUP):
                lo = gi * ATT_OUT + hh * hd
                qc = q_col[lo:lo + hd]
                kn = kvn_col[hh * hd:(hh + 1) * hd]
                vn = kvn_col[ATT_OUT + hh * hd:ATT_OUT + (hh + 1) * hd]
                bias = jnp.where(is_key, dist.astype(F32) * float(-slopes[gi, hh]), ATT_MASKED)
                s = jnp.sum(c_ref[0, tt, 0, hh] * qc, axis=0, keepdims=True) + bias
                s_new = jnp.sum(qc * kn, axis=0, keepdims=True)
                m = jnp.maximum(jnp.max(s, axis=-1, keepdims=True), s_new)
                p = jnp.exp(s - m)
                p_new = jnp.exp(s_new - m)
                l = jnp.sum(p, axis=-1, keepdims=True) + p_new
                og[hh][gi] = (jnp.sum(c_ref[0, tt, 1, hh] * p, axis=-1, keepdims=True) + p_new * vn) * (1.0 / l)
                lg[hh][gi] = m + jnp.log(l)
        heads = []
        for hh in range(ATT_HEADS_PER_GROUP):
            m = functools.reduce(jnp.maximum, lg[hh])
            es = [jnp.exp(l - m) for l in lg[hh]]
            heads.append(sum(e * o for e, o in zip(es, og[hh])) * (1.0 / sum(es)))
        acc = jnp.where(lane == tok, jnp.concatenate(heads, axis=0), acc)
    acc_ref[...] = acc

    @pl.when(k == tb // ntok - 1)
    def _():
        o_ref[...] = acc.T[:tb, :]


def attn_decode(q_rows, kv_new, caches, layer):
    bsz = q_rows.shape[0]
    tb, ntok = ATT_DEC_TB, ATT_DEC_TOK
    views, specs = [], []
    for cache, (win, dil) in zip(caches, ATT_GROUPS):
        assert cache.shape[2] == win and win % dil == 0
        views.append(jnp.transpose(cache, (0, 1, 3, 4, 5, 2)))
        specs.append(pl.BlockSpec((1, ntok) + KV_TAIL + (win,),
                                  lambda i, k: (layer, i * (tb // ntok) + k, 0, 0, 0, 0)))
    return pl.pallas_call(
        _attn_decode_kernel,
        out_shape=jax.ShapeDtypeStruct((bsz, ATT_OUT), F32),
        grid=(bsz // tb, tb // ntok),
        in_specs=([pl.BlockSpec((tb, ATT_Q_N), lambda i, k: (i, 0))]
                  + [pl.BlockSpec((tb, 2 * ATT_OUT), lambda i, k: (i, 0))] * N_ATT_GROUPS + specs),
        out_specs=pl.BlockSpec((tb, ATT_OUT), lambda i, k: (i, 0)),
        scratch_shapes=[pltpu.VMEM((ATT_OUT, 128), F32)],
        compiler_params=pltpu.CompilerParams(
            dimension_semantics=("arbitrary", "arbitrary"), vmem_limit_bytes=VMEM_LIMIT_BYTES),
        name="attn_decode",
    )(q_rows, *kv_new, *views)


PROMPT_TM = 512
PROMPT_FFN_TM = 512
KV_TAIL = (2, ATT_HEADS_PER_GROUP, ATT_HEAD_DIM)


def prompt_trunk(x, p):
    bsz, L, _ = x.shape
    t = bsz * L
    h = x.reshape(t, D_MODEL)
    new = [[] for _ in range(4 + N_ATT_GROUPS)]
    seq = lambda a: a.reshape(bsz, L, a.shape[-1])
    for l in range(DEPTH):
        h = ffn_block(h, p['ffn1_norm'], p['ffn1_w_in'], p['ffn1_w_out'], l, PROMPT_FFN_TM)
        qkv, gate, z, xbc, small = proj_ab(h, p['mix_norm'], p['w_ab'], l, PROMPT_TM)
        q5, kv5, *kvrows = proj_c(h, p['mix_norm'], p['w_c'], l, p['q_norm'][l], p['k_norm'][l], PROMPT_TM, True)
        o_a, s_gdn, c_gdn = gdn_prompt(seq(qkv), seq(gate), seq(small), p['gdn_conv_w'][l], p['gdn_a_log'][l],
                                       p['gdn_dt_bias'][l], p['gdn_norm_w'][l])
        o_b, s_ssd, c_ssd = ssd_prompt(seq(xbc), seq(z), seq(small), p['ssd_conv_w'][l], p['ssd_conv_b'][l],
                                       p['ssd_a_log'][l], p['ssd_dt_bias'][l], p['ssd_d'][l], p['ssd_norm_w'][l])
        o5, lse5 = attn_prompt(q5, kv5, bsz)
        h = merge_block(h, o_a.reshape(t, GDN_VW), o_b.reshape(t, SSD_INNER), (o5, lse5), p['merge'], l, PROMPT_TM)
        h = ffn_block(h, p['ffn2_norm'], p['ffn2_w_in'], p['ffn2_w_out'], l, PROMPT_FFN_TM)
        kv_new = [kvrows[gi].reshape(bsz, L, 2 * ATT_OUT)[:, L - min(win, L):].reshape((bsz, min(win, L)) + KV_TAIL)
                  for gi, (win, _) in enumerate(ATT_GROUPS)]
        for lst, s in zip(new, (s_gdn, c_gdn, s_ssd, c_ssd, *kv_new)):
            lst.append(s)
    return h.reshape(bsz, L, D_MODEL), [jnp.stack(lst, axis=0) for lst in new]


def sample_trunk(x, p, state_gdn, state_gdn_conv, state_ssd, state_ssd_conv, caches):
    bsz = x.shape[0]
    assert x.shape[1] == 1
    h = x.reshape(bsz, D_MODEL)
    tm = bsz
    new = [[] for _ in range(2 + N_ATT_GROUPS)]
    gdn_conv_t = jnp.transpose(state_gdn_conv, (0, 2, 1, 3))
    ssd_conv_t = jnp.transpose(state_ssd_conv, (0, 2, 1, 3))
    s_gdn = s_ssd = None
    for l in range(DEPTH):
        h = ffn_block(h, p['ffn1_norm'], p['ffn1_w_in'], p['ffn1_w_out'], l, tm)
        qkv, gate, z, xbc, small = proj_ab(h, p['mix_norm'], p['w_ab'], l, tm)
        q_rows, *kvrows = proj_c(h, p['mix_norm'], p['w_c'], l, p['q_norm'][l], p['k_norm'][l], tm, False)
        o_a, s_gdn, c_gdn = gdn_decode(qkv, gate, small, gdn_conv_t, state_gdn, s_gdn, l, p['gdn_conv_w'][l],
                                       p['gdn_a_log'][l], p['gdn_dt_bias'][l], p['gdn_norm_w'][l])
        o_b, s_ssd, c_ssd = ssd_decode(xbc, z, small, ssd_conv_t, state_ssd, s_ssd, l, p['ssd_conv_w'][l],
                                       p['ssd_conv_b'][l], p['ssd_a_log'][l], p['ssd_dt_bias'][l], p['ssd_d'][l],
                                       p['ssd_norm_w'][l])
        o_c = attn_decode(q_rows, kvrows, caches, l)
        h = merge_block(h, o_a, o_b, o_c, p['merge'], l, tm)
        h = ffn_block(h, p['ffn2_norm'], p['ffn2_w_in'], p['ffn2_w_out'], l, tm)
        kv_new = [kvrows[gi].reshape((bsz, 1) + KV_TAIL) for gi in range(N_ATT_GROUPS)]
        for lst, s in zip(new, (c_gdn, c_ssd, *kv_new)):
            lst.append(s)
    c_gdn, c_ssd, *kv_new = [jnp.stack(lst, axis=0) for lst in new]
    return h.reshape(bsz, 1, D_MODEL), [s_gdn, jnp.transpose(c_gdn, (0, 2, 1, 3)), s_ssd,
                                        jnp.transpose(c_ssd, (0, 2, 1, 3)), *kv_new]


def kernel(x_prompt, x_sample, state_gdn, state_gdn_conv, state_ssd, state_ssd_conv,
           cache_kv_w128, cache_kv_w512, cache_kv_w2048,
           ffn1_norm, ffn1_w_in, ffn1_w_out, mix_norm, w_in, gate_b,
           gdn_conv_w, gdn_a_log, gdn_dt_bias, gdn_norm_w,
           ssd_conv_w, ssd_conv_b, ssd_a_log, ssd_dt_bias, ssd_d, ssd_norm_w,
           q_norm, k_norm, w_br_a, w_br_b, w_br_c, w_out,
           ffn2_norm, ffn2_w_in, ffn2_w_out):
    w_ab, w_c, w_g = prep_weights(w_in)
    row = lambda a: a.reshape(DEPTH, 1, a.shape[-1])
    mix_norm_r = row(mix_norm)
    p = {'ffn1_norm': row(ffn1_norm), 'ffn1_w_in': ffn1_w_in.astype(BF16), 'ffn1_w_out': ffn1_w_out.astype(BF16),
         'mix_norm': mix_norm_r, 'w_ab': w_ab, 'w_c': w_c,
         'merge': (mix_norm_r, w_g, row(gate_b), w_br_a.astype(BF16), w_br_b.astype(BF16), w_br_c.astype(BF16),
                   w_out.astype(BF16)),
         'gdn_conv_w': gdn_conv_w, 'gdn_a_log': gdn_a_log, 'gdn_dt_bias': gdn_dt_bias, 'gdn_norm_w': gdn_norm_w,
         'ssd_conv_w': ssd_conv_w, 'ssd_conv_b': ssd_conv_b, 'ssd_a_log': ssd_a_log,
         'ssd_dt_bias': ssd_dt_bias, 'ssd_d': ssd_d, 'ssd_norm_w': ssd_norm_w,
         'q_norm': q_norm, 'k_norm': k_norm,
         'ffn2_norm': row(ffn2_norm), 'ffn2_w_in': ffn2_w_in.astype(BF16), 'ffn2_w_out': ffn2_w_out.astype(BF16)}
    y_prompt, st_p = prompt_trunk(x_prompt, p)
    y_sample, st_s = sample_trunk(x_sample, p, state_gdn, state_gdn_conv, state_ssd, state_ssd_conv,
                                  [cache_kv_w128, cache_kv_w512, cache_kv_w2048])
    gdn_p, gdn_conv_p, ssd_p, ssd_conv_p, kv128_p, kv512_p, kv2048_p = st_p
    gdn_s, gdn_conv_s, ssd_s, ssd_conv_s, kv128_s, kv512_s, kv2048_s = st_s
    return (y_prompt, y_sample, gdn_p, gdn_s, gdn_conv_p, gdn_conv_s, ssd_p, ssd_s, ssd_conv_p, ssd_conv_s,
            kv128_p, kv128_s, kv512_p, kv512_s, kv2048_p, kv2048_s)
```

```python
import functools
import math
import jax, jax.numpy as jnp
from jax import lax
import numpy as np
from jax.experimental import pallas as pl
from jax.experimental.pallas import tpu as pltpu

D_MODEL = 1024
BATCH = 8
SEQ = 2048
DEPTH = 4
DEC_BATCH = 128
DEC_SEQ = 1
PAST_LEN = 2048

CONV_WIDTH = 4
RMS_EPS = 1e-6
GDN_HEADS = 4
GDN_DK = 128
GDN_DV = 128
GDN_CHUNK = 64
GDN_QK = GDN_HEADS * GDN_DK
GDN_VW = GDN_HEADS * GDN_DV
GDN_CONV_DIM = 2 * GDN_QK + GDN_VW
SSD_HEADS = 8
SSD_HEADDIM = 64
SSD_GROUPS = 2
SSD_STATE = 128
SSD_CHUNK = 128
SSD_INNER = SSD_HEADS * SSD_HEADDIM
SSD_CONV_DIM = SSD_INNER + 2 * SSD_GROUPS * SSD_STATE
ATT_GROUPS = ((128, 1), (512, 4), (2048, 16))
N_ATT_GROUPS = len(ATT_GROUPS)
ATT_HEADS_PER_GROUP = 4
ATT_HEAD_DIM = 64
ATT_HEADS = N_ATT_GROUPS * ATT_HEADS_PER_GROUP
ATT_OUT = ATT_HEADS_PER_GROUP * ATT_HEAD_DIM
D_FF = ((8 * D_MODEL // 3 + 127) // 128) * 128
N_BRANCH = 3
IN_SPLITS = (GDN_CONV_DIM, GDN_VW, GDN_HEADS, GDN_HEADS, SSD_INNER, SSD_CONV_DIM, SSD_HEADS,
             3 * ATT_HEADS * ATT_HEAD_DIM, N_BRANCH * D_MODEL)
D_IN = sum(IN_SPLITS)

F32 = jnp.float32
BF16 = jnp.bfloat16
VMEM_LIMIT_BYTES = 56 * 1024 * 1024


def _layer_spec(stacked, layer):
    zeros = (0,) * (stacked.ndim - 1)
    return pl.BlockSpec((1,) + stacked.shape[1:], lambda *_: (layer,) + zeros, pipeline_mode=pl.Buffered(1))


def _ffn_kernel(h_ref, nw_ref, wi_ref, wo_ref, o_ref):
    h = h_ref[...]
    xn = h * lax.rsqrt(jnp.mean(h * h, axis=-1, keepdims=True) + RMS_EPS) * nw_ref[0]
    gu = jnp.dot(xn.astype(BF16), wi_ref[0], preferred_element_type=F32)
    g, up = gu[:, :D_FF], gu[:, D_FF:]
    act = (g * jax.nn.sigmoid(g) * up).astype(BF16)
    y = jnp.dot(act, wo_ref[0], preferred_element_type=F32)
    o_ref[...] = h + 0.5 * y


def ffn_block(h, norm_w, w_in_bf16, w_out_bf16, layer, tm):
    t = h.shape[0]
    assert t % tm == 0
    return pl.pallas_call(
        _ffn_kernel,
        out_shape=jax.ShapeDtypeStruct(h.shape, h.dtype),
        grid=(t // tm,),
        in_specs=[pl.BlockSpec((tm, D_MODEL), lambda i: (i, 0)),
                  _layer_spec(norm_w, layer), _layer_spec(w_in_bf16, layer), _layer_spec(w_out_bf16, layer)],
        out_specs=pl.BlockSpec((tm, D_MODEL), lambda i: (i, 0)),
        compiler_params=pltpu.CompilerParams(
            dimension_semantics=("arbitrary",), vmem_limit_bytes=VMEM_LIMIT_BYTES),
        name="ffn",
    )(h, norm_w, w_in_bf16, w_out_bf16)


_DN_BMM = (((2,), (1,)), ((0,), (0,)))
_DN_BMM_NT = (((2,), (2,)), ((0,), (0,)))


def _bmm(a, b):
    return lax.dot_general(a.astype(BF16), b.astype(BF16), _DN_BMM, preferred_element_type=F32)


def _bmm_nt(a, b):
    return lax.dot_general(a.astype(BF16), b.astype(BF16), _DN_BMM_NT, preferred_element_type=F32)


def _split2(x):
    hi = x.astype(BF16)
    return hi, (x - hi.astype(F32)).astype(BF16)


def _split3(x):
    x1 = x.astype(BF16)
    r1 = x - x1.astype(F32)
    x2 = r1.astype(BF16)
    x3 = (r1 - x2.astype(F32)).astype(BF16)
    return x1, x2, x3


def _bmm_hi(a, b):
    a1, a2 = _split2(a)
    b1, b2 = _split2(b)
    dot = functools.partial(lax.dot_general, dimension_numbers=_DN_BMM, preferred_element_type=F32)
    return dot(a1, b1) + (dot(a1, b2) + dot(a2, b1))


def _cumsum_rows(tri, x):
    dot = functools.partial(lax.dot_general, dimension_numbers=_DN_BMM, preferred_element_type=F32)
    x1, x2, x3 = _split3(x)
    return dot(tri, x1) + (dot(tri, x2) + dot(tri, x3))


def _softplus(x):
    return jnp.maximum(x, 0.0) + jnp.log(1.0 + jnp.exp(-jnp.abs(x)))


def _silu(x):
    return x * jax.nn.sigmoid(x)


GDN_INV_BLOCK = 16


def _unit_lower_solve(a, rhs, row, col, mm):
    c = a.shape[-1]
    eye = (row == col).astype(F32)
    same_blk = (row // GDN_INV_BLOCK) == (col // GDN_INV_BLOCK)
    d = jnp.where(same_blk, a, 0.0)
    n = a - d
    x = eye - d
    pw = d
    k = 2
    while k < GDN_INV_BLOCK:
        pw = mm(pw, pw)
        x = x + mm(x, pw)
        k *= 2
    m = mm(x, n)
    y = mm(x, rhs)
    nblk = c // GDN_INV_BLOCK
    q = eye - m
    pw = m
    k = 2
    while k < nblk:
        pw = mm(pw, pw)
        q = q + mm(q, pw)
        k *= 2
    return mm(q, y)


def _gdn_prompt_kernel(qkv_ref, gate_ref, small_ref, convw_ref, alog_ref, dtb_ref, normw_ref,
                       o_ref, s_ref, cst_ref, xp_ref):
    c = GDN_CHUNK
    nb = qkv_ref.shape[0]

    @pl.when(pl.program_id(0) == 0)
    def _():
        s_ref[...] = jnp.zeros_like(s_ref)
        xp_ref[:, 0:8, :] = jnp.zeros((nb, 8, GDN_CONV_DIM), F32)

    x = qkv_ref[...]
    xp_ref[:, 8:8 + c, :] = x
    w = convw_ref[...]
    y = (xp_ref[:, 5:5 + c, :] * w[0:1, :] + xp_ref[:, 6:6 + c, :] * w[1:2, :]
         + xp_ref[:, 7:7 + c, :] * w[2:3, :] + x * w[3:4, :])
    xp_ref[:, 0:8, :] = xp_ref[:, c:c + 8, :]
    cst_ref[...] = x[:, c - (CONV_WIDTH - 1):, :]
    y = _silu(y)

    small = small_ref[...]
    beta_all = jax.nn.sigmoid(small)
    g_all = -jnp.exp(alog_ref[...]) * _softplus(small + dtb_ref[...])
    row = lax.broadcasted_iota(jnp.int32, (c, c), 0)
    col = lax.broadcasted_iota(jnp.int32, (c, c), 1)
    incl = row >= col
    strict = row > col
    tri = jnp.broadcast_to(incl.astype(BF16), (nb, c, c))
    gam_all = _cumsum_rows(tri, g_all)
    gam_all_t = jnp.swapaxes(gam_all, 1, 2)
    egam_all = jnp.exp(gam_all)

    for h in range(GDN_HEADS):
        lo = h * GDN_DK
        q = y[:, :, lo:lo + GDN_DK]
        k = y[:, :, GDN_QK + lo:GDN_QK + lo + GDN_DK]
        v = y[:, :, 2 * GDN_QK + h * GDN_DV:2 * GDN_QK + (h + 1) * GDN_DV]
        q = q * (lax.rsqrt(jnp.sum(q * q, axis=-1, keepdims=True) + RMS_EPS) * GDN_DK ** -0.5)
        k = k * lax.rsqrt(jnp.sum(k * k, axis=-1, keepdims=True) + RMS_EPS)
        beta = beta_all[:, :, h:h + 1]
        gam = gam_all[:, :, GDN_HEADS + h:GDN_HEADS + h + 1]
        gam_row = gam_all_t[:, GDN_HEADS + h:GDN_HEADS + h + 1, :]
        egam = egam_all[:, :, GDN_HEADS + h:GDN_HEADS + h + 1]
        gam_last = gam_row[:, :, c - 1:c]
        dec = jnp.where(incl, jnp.exp(jnp.minimum(gam - gam_row, 0.0)), 0.0)
        kk = _bmm_nt(k, k)
        a_mat = jnp.where(strict, beta * kk * dec, 0.0)
        rhs = jnp.concatenate([v * beta, k * (beta * egam)], axis=-1)
        sol = _unit_lower_solve(a_mat, rhs, row, col, _bmm)
        u0, wk = sol[:, :, :GDN_DV], sol[:, :, GDN_DV:]
        qk = _bmm_nt(q, k) * dec
        q_dec = q * egam
        k_dec = k * jnp.exp(gam_last - gam)
        s = s_ref[:, h]
        u = u0 - _bmm(wk, s)
        o = _bmm(q_dec, s) + _bmm(qk, u)
        s_ref[:, h] = s * jnp.exp(gam_last) + _bmm(jnp.swapaxes(k_dec, 1, 2), u)
        o = o * lax.rsqrt(jnp.mean(o * o, axis=-1, keepdims=True) + RMS_EPS) * normw_ref[...]
        o_ref[:, :, h * GDN_DV:(h + 1) * GDN_DV] = o * _silu(gate_ref[:, :, h * GDN_DV:(h + 1) * GDN_DV])


def gdn_prompt(a_qkv, a_gate, small, conv_w, a_log, dt_bias, norm_w):
    bsz, L, _ = a_qkv.shape
    c = GDN_CHUNK
    assert L % c == 0
    lanes = small.shape[-1]
    alog_row = jnp.zeros((1, lanes), F32).at[0, GDN_HEADS:2 * GDN_HEADS].set(a_log)
    dtb_row = jnp.zeros((1, lanes), F32).at[0, GDN_HEADS:2 * GDN_HEADS].set(dt_bias)
    const2 = lambda n: (0, 0)
    return pl.pallas_call(
        _gdn_prompt_kernel,
        out_shape=(jax.ShapeDtypeStruct((bsz, L, GDN_VW), F32),
                   jax.ShapeDtypeStruct((bsz, GDN_HEADS, GDN_DK, GDN_DV), F32),
                   jax.ShapeDtypeStruct((bsz, CONV_WIDTH - 1, GDN_CONV_DIM), F32)),
        grid=(L // c,),
        in_specs=[
            pl.BlockSpec((bsz, c, GDN_CONV_DIM), lambda n: (0, n, 0)),
            pl.BlockSpec((bsz, c, GDN_VW), lambda n: (0, n, 0)),
            pl.BlockSpec((bsz, c, lanes), lambda n: (0, n, 0)),
            pl.BlockSpec((CONV_WIDTH, GDN_CONV_DIM), const2),
            pl.BlockSpec((1, lanes), const2),
            pl.BlockSpec((1, lanes), const2),
            pl.BlockSpec((1, GDN_DV), const2),
        ],
        out_specs=(
            pl.BlockSpec((bsz, c, GDN_VW), lambda n: (0, n, 0)),
            pl.BlockSpec((bsz, GDN_HEADS, GDN_DK, GDN_DV), lambda n: (0, 0, 0, 0)),
            pl.BlockSpec((bsz, CONV_WIDTH - 1, GDN_CONV_DIM), lambda n: (0, 0, 0)),
        ),
        scratch_shapes=[pltpu.VMEM((bsz, c + 8, GDN_CONV_DIM), F32)],
        compiler_params=pltpu.CompilerParams(
            dimension_semantics=("arbitrary",), vmem_limit_bytes=VMEM_LIMIT_BYTES),
        name="gdn_prompt",
    )(a_qkv, a_gate, small, conv_w, alog_row, dtb_row, norm_w.reshape(1, GDN_DV))


SSD_DT_LANE = 2 * GDN_HEADS
SSD_BC = SSD_GROUPS * SSD_STATE
SSD_HPG = SSD_HEADS // SSD_GROUPS


def _ssd_prompt_kernel(xbc_ref, z_ref, small_ref, convw_ref, convb_ref, alog_ref, dtb_ref, d_ref, normw_ref,
                       o_ref, h_ref, cst_ref, xp_ref):
    c = SSD_CHUNK
    nb = xbc_ref.shape[0]

    @pl.when(pl.program_id(0) == 0)
    def _():
        h_ref[...] = jnp.zeros_like(h_ref)
        xp_ref[:, 0:8, :] = jnp.zeros((nb, 8, SSD_CONV_DIM), F32)

    x = xbc_ref[...]
    xp_ref[:, 8:8 + c, :] = x
    w = convw_ref[...]
    y = (xp_ref[:, 5:5 + c, :] * w[0:1, :] + xp_ref[:, 6:6 + c, :] * w[1:2, :]
         + xp_ref[:, 7:7 + c, :] * w[2:3, :] + x * w[3:4, :]) + convb_ref[...]
    xp_ref[:, 0:8, :] = xp_ref[:, c:c + 8, :]
    cst_ref[...] = x[:, c - (CONV_WIDTH - 1):, :]
    y = _silu(y)
    xs = y[:, :, :SSD_INNER]
    bm = y[:, :, SSD_INNER:SSD_INNER + SSD_BC]
    cm = y[:, :, SSD_INNER + SSD_BC:]

    dt_all = _softplus(small_ref[...] + dtb_ref[...])
    da_all = dt_all * (-jnp.exp(alog_ref[...]))
    row = lax.broadcasted_iota(jnp.int32, (c, c), 0)
    col = lax.broadcasted_iota(jnp.int32, (c, c), 1)
    incl = row >= col
    tri = jnp.broadcast_to(incl.astype(BF16), (nb, c, c))
    acum_all = _cumsum_rows(tri, da_all)
    acum_all_t = jnp.swapaxes(acum_all, 1, 2)
    eacum_all = jnp.exp(acum_all)

    ys = []
    for g in range(SSD_GROUPS):
        bg = bm[:, :, g * SSD_STATE:(g + 1) * SSD_STATE]
        cg = cm[:, :, g * SSD_STATE:(g + 1) * SSD_STATE]
        scores = _bmm_nt(cg, bg)
        for hh in range(SSD_HPG):
            h = g * SSD_HPG + hh
            ln = SSD_DT_LANE + h
            dtc = dt_all[:, :, ln:ln + 1]
            ac = acum_all[:, :, ln:ln + 1]
            ar = acum_all_t[:, ln:ln + 1, :]
            alast = ar[:, :, c - 1:c]
            lmat = jnp.where(incl, jnp.exp(jnp.minimum(ac - ar, 0.0)), 0.0)
            xd = xs[:, :, h * SSD_HEADDIM:(h + 1) * SSD_HEADDIM] * dtc
            y_diag = _bmm(scores * lmat, xd)
            hin = h_ref[:, h]
            y_off = _bmm_nt(cg * eacum_all[:, :, ln:ln + 1], hin)
            states = _bmm(jnp.swapaxes(xd, 1, 2), bg * jnp.exp(alast - ac))
            h_ref[:, h] = hin * jnp.exp(alast) + states
            ys.append(y_diag + y_off)
    yf = jnp.concatenate(ys, axis=-1) + d_ref[...] * xs
    yf = yf * _silu(z_ref[...])
    gw = SSD_INNER // SSD_GROUPS
    for g in range(SSD_GROUPS):
        seg = yf[:, :, g * gw:(g + 1) * gw]
        o_ref[:, :, g * gw:(g + 1) * gw] = (seg * lax.rsqrt(jnp.mean(seg * seg, axis=-1, keepdims=True) + RMS_EPS)
                                            * normw_ref[:, g * gw:(g + 1) * gw])


def ssd_prompt(b_xbc, b_z, small, conv_w, conv_b, a_log, dt_bias, d_skip, norm_w):
    bsz, L, _ = b_xbc.shape
    c = SSD_CHUNK
    assert L % c == 0
    lanes = small.shape[-1]
    sl = slice(SSD_DT_LANE, SSD_DT_LANE + SSD_HEADS)
    alog_row = jnp.zeros((1, lanes), F32).at[0, sl].set(a_log)
    dtb_row = jnp.zeros((1, lanes), F32).at[0, sl].set(dt_bias)
    d_row = jnp.repeat(d_skip, SSD_HEADDIM).reshape(1, SSD_INNER)
    const2 = lambda n: (0, 0)
    return pl.pallas_call(
        _ssd_prompt_kernel,
        out_shape=(jax.ShapeDtypeStruct((bsz, L, SSD_INNER), F32),
                   jax.ShapeDtypeStruct((bsz, SSD_HEADS, SSD_HEADDIM, SSD_STATE), F32),
                   jax.ShapeDtypeStruct((bsz, CONV_WIDTH - 1, SSD_CONV_DIM), F32)),
        grid=(L // c,),
        in_specs=[
            pl.BlockSpec((bsz, c, SSD_CONV_DIM), lambda n: (0, n, 0)),
            pl.BlockSpec((bsz, c, SSD_INNER), lambda n: (0, n, 0)),
            pl.BlockSpec((bsz, c, lanes), lambda n: (0, n, 0)),
            pl.BlockSpec((CONV_WIDTH, SSD_CONV_DIM), const2),
            pl.BlockSpec((1, SSD_CONV_DIM), const2),
            pl.BlockSpec((1, lanes), const2),
            pl.BlockSpec((1, lanes), const2),
            pl.BlockSpec((1, SSD_INNER), const2),
            pl.BlockSpec((1, SSD_INNER), const2),
        ],
        out_specs=(
            pl.BlockSpec((bsz, c, SSD_INNER), lambda n: (0, n, 0)),
            pl.BlockSpec((bsz, SSD_HEADS, SSD_HEADDIM, SSD_STATE), lambda n: (0, 0, 0, 0)),
            pl.BlockSpec((bsz, CONV_WIDTH - 1, SSD_CONV_DIM), lambda n: (0, 0, 0)),
        ),
        scratch_shapes=[pltpu.VMEM((bsz, c + 8, SSD_CONV_DIM), F32)],
        compiler_params=pltpu.CompilerParams(
            dimension_semantics=("arbitrary",), vmem_limit_bytes=VMEM_LIMIT_BYTES),
        name="ssd_prompt",
    )(b_xbc, b_z, small, conv_w, conv_b.reshape(1, SSD_CONV_DIM), alog_row, dtb_row, d_row,
      norm_w.reshape(1, SSD_INNER))


ATT_BAND = 128
ATT_MASKED = -1e30
assert all(w // d == ATT_BAND for w, d in ATT_GROUPS)


def _alibi_slopes_np():
    idx = np.arange(1, ATT_HEADS + 1, dtype=np.float32)
    return np.exp2(-8.0 * idx / ATT_HEADS).astype(np.float32).reshape(N_ATT_GROUPS, ATT_HEADS_PER_GROUP)


def _attn_bias(gi, first):
    nk = 2 * ATT_BAND
    i = lax.broadcasted_iota(jnp.int32, (ATT_BAND, nk), 0)
    j = lax.broadcasted_iota(jnp.int32, (ATT_BAND, nk), 1)
    du = i + ATT_BAND - j
    duf = du.astype(F32)
    dil = ATT_GROUPS[gi][1]
    slopes = _alibi_slopes_np()[gi]
    if first:
        inside = jnp.minimum(du, j - ATT_BAND) >= 0
    else:
        inside = jnp.abs(2 * du - ATT_BAND) <= ATT_BAND
    return [jnp.where(inside, duf * float(-slopes[hh] * dil * LOG2E), ATT_MASKED)
            for hh in range(ATT_HEADS_PER_GROUP)]


ATT_LANE_CHUNKS = ATT_OUT // 128
ATT_HEADS_PER_CHUNK = 128 // ATT_HEAD_DIM
ATT_UNROLL = (5, 4, 4)
LOG2E = math.log2(math.e)
LN2 = math.log(2.0)


def _attn_block(q, keys, vals, bias):
    nk = keys.shape[0]
    low = lax.broadcasted_iota(jnp.int32, (ATT_BAND, 128), 1) < ATT_HEAD_DIM
    q2 = q * LOG2E
    kb = keys.astype(BF16)
    va = jnp.concatenate([vals.astype(BF16), jnp.ones((nk, 128), BF16)], axis=1)
    res = []
    for hh in range(ATT_HEADS_PER_CHUNK):
        qh = jnp.where(low if hh == 0 else jnp.logical_not(low), q2, 0.0).astype(BF16)
        s = lax.dot_general(qh, kb, (((1,), (1,)), ((), ())), preferred_element_type=F32) + bias[hh]
        m = jnp.max(s, axis=-1, keepdims=True)
        p = jnp.exp2(s - m)
        oa = jnp.dot(p.astype(BF16), va, preferred_element_type=F32)
        l = oa[:, 128:]
        res.append((oa[:, :128] * (1.0 / l), (m + jnp.log2(l)) * LN2))
    return jnp.where(low, res[0][0], res[1][0]), jnp.where(low, res[0][1], res[1][1])


def _attn_prompt_kernel(q_ref, kv_ref, o_ref, lse_ref):
    g = pl.program_id(0)
    L = q_ref.shape[2]
    nc = ATT_LANE_CHUNKS

    for gi, (win, dil) in enumerate(ATT_GROUPS):
        @pl.when(g == gi)
        def _(gi=gi, win=win, dil=dil):
            nblk = L // win

            def rows(start):
                return pl.ds(start, ATT_BAND, stride=dil) if dil > 1 else pl.ds(start, ATT_BAND)

            def unit(cur, prev, bias):
                for c in range(nc):
                    hs = slice(c * ATT_HEADS_PER_CHUNK, (c + 1) * ATT_HEADS_PER_CHUNK)
                    keys = jnp.concatenate([kv_ref[0, c, prev, :], kv_ref[0, c, cur, :]], axis=0)
                    vals = jnp.concatenate([kv_ref[0, nc + c, prev, :], kv_ref[0, nc + c, cur, :]], axis=0)
                    o, lse = _attn_block(q_ref[0, c, cur, :], keys, vals, bias[hs])
                    o_ref[0, c, cur, :] = o
                    lse_ref[0, c, cur, :] = lse

            bias0 = _attn_bias(gi, True)

            def body0(r, carry):
                unit(rows(r), rows(r), bias0)
                return carry

            lax.fori_loop(0, dil, body0, 0, unroll=min(dil, ATT_UNROLL[gi]))
            if nblk > 1:
                bias1 = _attn_bias(gi, False)

                def body(u, carry):
                    start = (u // dil + 1) * win + u % dil
                    unit(rows(start), rows(start - win), bias1)
                    return carry

                lax.fori_loop(0, (nblk - 1) * dil, body, 0, unroll=ATT_UNROLL[gi])


def attn_prompt(q5, kv5, bsz):
    ng, nc, _, lanes = q5.shape
    L = SEQ
    assert all(L % w == 0 for w, _ in ATT_GROUPS)
    blk = lambda n: pl.BlockSpec((1, n, L, lanes), lambda g, b: (g, 0, b, 0))
    out = jax.ShapeDtypeStruct((ng, nc, bsz * L, lanes), F32)
    return pl.pallas_call(
        _attn_prompt_kernel,
        out_shape=(out, out),
        grid=(ng, bsz),
        in_specs=[blk(nc), blk(2 * nc)],
        out_specs=(blk(nc), blk(nc)),
        compiler_params=pltpu.CompilerParams(
            dimension_semantics=("arbitrary", "arbitrary"), vmem_limit_bytes=VMEM_LIMIT_BYTES),
        name="attn_prompt",
    )(q5, kv5)


SMALL_LANES = 128
_OFF = np.cumsum((0,) + IN_SPLITS).tolist()
PROJ_AB_WIDTHS = (GDN_CONV_DIM, GDN_VW, SSD_INNER, SSD_CONV_DIM, SMALL_LANES)
PROJ_AB_N = sum(PROJ_AB_WIDTHS)
PROJ_C_N = 3 * ATT_HEADS * ATT_HEAD_DIM
ATT_Q_N = ATT_HEADS * ATT_HEAD_DIM


def prep_weights(w_in):
    o = _OFF
    wt = jnp.transpose(w_in, (0, 2, 1))
    pad = jnp.zeros((w_in.shape[0], SMALL_LANES - 2 * GDN_HEADS - SSD_HEADS, w_in.shape[1]), w_in.dtype)
    w_ab = jnp.concatenate([wt[:, o[0]:o[1]], wt[:, o[1]:o[2]], wt[:, o[4]:o[5]], wt[:, o[5]:o[6]],
                            wt[:, o[2]:o[4]], wt[:, o[6]:o[7]], pad], axis=1)
    back = lambda a: jnp.transpose(a.astype(BF16), (0, 2, 1))
    return back(w_ab), back(wt[:, o[7]:o[8]]), back(wt[:, o[8]:o[9]])


def _rmsnorm_rows(h, w_row):
    return h * lax.rsqrt(jnp.mean(h * h, axis=-1, keepdims=True) + RMS_EPS) * w_row


def _proj_ab_kernel(h_ref, nw_ref, w_ref, qkv_ref, gate_ref, z_ref, xbc_ref, small_ref):
    u = _rmsnorm_rows(h_ref[...], nw_ref[0]).astype(BF16)
    y = jnp.dot(u, w_ref[0], preferred_element_type=F32)
    lo = 0
    for ref, wd in zip((qkv_ref, gate_ref, z_ref, xbc_ref, small_ref), PROJ_AB_WIDTHS):
        ref[...] = y[:, lo:lo + wd]
        lo += wd


def proj_ab(h, norm_w, w_ab, layer, tm):
    t = h.shape[0]
    assert t % tm == 0
    return pl.pallas_call(
        _proj_ab_kernel,
        out_shape=tuple(jax.ShapeDtypeStruct((t, wd), F32) for wd in PROJ_AB_WIDTHS),
        grid=(t // tm,),
        in_specs=[pl.BlockSpec((tm, D_MODEL), lambda i: (i, 0)), _layer_spec(norm_w, layer), _layer_spec(w_ab, layer)],
        out_specs=tuple(pl.BlockSpec((tm, wd), lambda i: (i, 0)) for wd in PROJ_AB_WIDTHS),
        compiler_params=pltpu.CompilerParams(
            dimension_semantics=("arbitrary",), vmem_limit_bytes=VMEM_LIMIT_BYTES),
        name="proj_ab",
    )(h, norm_w, w_ab)


def _head_rmsnorm(x, w_row, seg):
    outs = []
    for c in range(x.shape[-1] // 128):
        xc = x[:, c * 128:(c + 1) * 128]
        hi, lo = _split2(xc * xc)
        ss = jnp.dot(hi, seg, preferred_element_type=F32) + jnp.dot(lo, seg, preferred_element_type=F32)
        outs.append(xc * lax.rsqrt(ss * (1.0 / ATT_HEAD_DIM) + RMS_EPS) * w_row)
    return outs


def _proj_c_kernel(h_ref, nw_ref, w_ref, qw_ref, kw_ref, *out_refs, chunked):
    u = _rmsnorm_rows(h_ref[...], nw_ref[0]).astype(BF16)
    y = jnp.dot(u, w_ref[0], preferred_element_type=F32)
    r = lax.broadcasted_iota(jnp.int32, (128, 128), 0) // ATT_HEAD_DIM
    c = lax.broadcasted_iota(jnp.int32, (128, 128), 1) // ATT_HEAD_DIM
    seg = (r == c).astype(BF16)
    qn = [t * ATT_HEAD_DIM ** -0.5 for t in _head_rmsnorm(y[:, :ATT_Q_N], qw_ref[...], seg)]
    kn = _head_rmsnorm(y[:, ATT_Q_N:2 * ATT_Q_N], kw_ref[...], seg)
    v = y[:, 2 * ATT_Q_N:]
    nc = ATT_LANE_CHUNKS
    if chunked:
        q5_ref, kv5_ref, *kvrows_refs = out_refs
    else:
        q_ref, *kvrows_refs = out_refs
    for g in range(N_ATT_GROUPS):
        for cc in range(nc):
            i = g * nc + cc
            vc = v[:, i * 128:(i + 1) * 128]
            kvrows_refs[g][:, cc * 128:(cc + 1) * 128] = kn[i]
            kvrows_refs[g][:, ATT_OUT + cc * 128:ATT_OUT + (cc + 1) * 128] = vc
            if chunked:
                q5_ref[g, cc] = qn[i]
                kv5_ref[g, cc] = kn[i]
                kv5_ref[g, nc + cc] = vc
            else:
                q_ref[:, i * 128:(i + 1) * 128] = qn[i]


def proj_c(h, norm_w, w_c, layer, q_norm, k_norm, tm, chunked):
    t = h.shape[0]
    assert t % tm == 0
    const = dict(pipeline_mode=pl.Buffered(1))
    ng, nc = N_ATT_GROUPS, ATT_LANE_CHUNKS
    if chunked:
        out_shape = [jax.ShapeDtypeStruct((ng, nc, t, 128), F32), jax.ShapeDtypeStruct((ng, 2 * nc, t, 128), F32)]
        out_specs = [pl.BlockSpec((ng, nc, tm, 128), lambda i: (0, 0, i, 0)),
                     pl.BlockSpec((ng, 2 * nc, tm, 128), lambda i: (0, 0, i, 0))]
    else:
        out_shape = [jax.ShapeDtypeStruct((t, ATT_Q_N), F32)]
        out_specs = [pl.BlockSpec((tm, ATT_Q_N), lambda i: (i, 0))]
    out_shape += [jax.ShapeDtypeStruct((t, 2 * ATT_OUT), F32)] * ng
    out_specs += [pl.BlockSpec((tm, 2 * ATT_OUT), lambda i: (i, 0))] * ng
    tile2 = lambda w: jnp.tile(w, 128 // ATT_HEAD_DIM).reshape(1, 128)
    return pl.pallas_call(
        functools.partial(_proj_c_kernel, chunked=chunked),
        out_shape=tuple(out_shape),
        grid=(t // tm,),
        in_specs=[
            pl.BlockSpec((tm, D_MODEL), lambda i: (i, 0)),
            _layer_spec(norm_w, layer),
            _layer_spec(w_c, layer),
            pl.BlockSpec((1, 128), lambda i: (0, 0), **const),
            pl.BlockSpec((1, 128), lambda i: (0, 0), **const),
        ],
        out_specs=tuple(out_specs),
        compiler_params=pltpu.CompilerParams(
            dimension_semantics=("arbitrary",), vmem_limit_bytes=VMEM_LIMIT_BYTES),
        name="proj_c",
    )(h, norm_w, w_c, tile2(q_norm), tile2(k_norm))


ROW_SUBTILE = 256


def _merge_kernel(h_ref, oa_ref, ob_ref, *refs, chunked):
    att_refs, refs = (refs[:2], refs[2:]) if chunked else (refs[:1], refs[1:])
    nw_ref, wg_ref, gb_ref, wa_ref, wb_ref, wc_ref, wo_ref, out_ref = refs
    tm = h_ref.shape[0]
    sub = min(tm, ROW_SUBTILE)
    for r0 in range(0, tm, sub):
        rs = slice(r0, r0 + sub)
        if chunked:
            o5_ref, lse5_ref = att_refs
            ocs = []
            for cc in range(ATT_LANE_CHUNKS):
                ls = [lse5_ref[g, cc, rs, :] for g in range(N_ATT_GROUPS)]
                m = functools.reduce(jnp.maximum, ls)
                es = [jnp.exp(l - m) for l in ls]
                num = sum(e * o5_ref[g, cc, rs, :] for g, e in enumerate(es))
                ocs.append(num * (1.0 / sum(es)))
            o_c = jnp.concatenate(ocs, axis=-1)
        else:
            o_c = att_refs[0][rs, :]
        h = h_ref[rs, :]
        u = _rmsnorm_rows(h, nw_ref[0]).astype(BF16)
        merged = None
        for i, (o, w_ref) in enumerate(((oa_ref[rs, :], wa_ref), (ob_ref[rs, :], wb_ref), (o_c, wc_ref))):
            sl = slice(i * D_MODEL, (i + 1) * D_MODEL)
            gate = jax.nn.sigmoid(jnp.dot(u, wg_ref[0, :, sl], preferred_element_type=F32) + gb_ref[0, :, sl])
            term = gate * jnp.dot(o.astype(BF16), w_ref[0], preferred_element_type=F32)
            merged = term if merged is None else merged + term
        out_ref[rs, :] = h + jnp.dot(merged.astype(BF16), wo_ref[0], preferred_element_type=F32)


def merge_block(h, o_a, o_b, o_att, params, layer, tm):
    t = h.shape[0]
    assert t % tm == 0
    chunked = isinstance(o_att, tuple)
    rows = lambda wd: pl.BlockSpec((tm, wd), lambda i: (i, 0))
    if chunked:
        att_specs = [pl.BlockSpec((N_ATT_GROUPS, ATT_LANE_CHUNKS, tm, 128), lambda i: (0, 0, i, 0))] * 2
        att_args = list(o_att)
    else:
        att_specs = [rows(ATT_OUT)]
        att_args = [o_att]
    return pl.pallas_call(
        functools.partial(_merge_kernel, chunked=chunked),
        out_shape=jax.ShapeDtypeStruct(h.shape, F32),
        grid=(t // tm,),
        in_specs=[rows(D_MODEL), rows(GDN_VW), rows(SSD_INNER)] + att_specs + [_layer_spec(a, layer) for a in params],
        out_specs=rows(D_MODEL),
        compiler_params=pltpu.CompilerParams(
            dimension_semantics=("arbitrary",), vmem_limit_bytes=VMEM_LIMIT_BYTES),
        name="merge",
    )(h, o_a, o_b, *att_args, *params)


DEC_TB = 8


def _conv_step(cst_ref, x, w_ref):
    prev = [cst_ref[0, i] for i in range(CONV_WIDTH - 1)]
    y = x * w_ref[CONV_WIDTH - 1:CONV_WIDTH, :]
    for i, pr in enumerate(prev):
        y = y + pr * w_ref[i:i + 1, :]
    return y, prev[1:] + [x]


def _gdn_decode_kernel(qkv_ref, gate_ref, small_ref, cst_ref, s_ref, carried_ref, convw_ref, alog_ref, dtb_ref,
                       normw_ref, o_ref, so_ref, csto_ref):
    del carried_ref
    tb = qkv_ref.shape[0]
    x = qkv_ref[...]
    y, new_cst = _conv_step(cst_ref, x, convw_ref)
    for i, r in enumerate(new_cst):
        csto_ref[i] = r
    y = _silu(y)
    small = small_ref[...]
    beta_all = jax.nn.sigmoid(small)
    eg_all = jnp.exp(-jnp.exp(alog_ref[...]) * _softplus(small + dtb_ref[...]))
    lanes = small.shape[-1]
    r = lax.broadcasted_iota(jnp.int32, (lanes, GDN_HEADS * GDN_DV), 0)
    c = lax.broadcasted_iota(jnp.int32, (lanes, GDN_HEADS * GDN_DV), 1)
    sel = (r == GDN_HEADS + c // GDN_DV).astype(BF16)
    eg_wide = sum(jnp.dot(t, sel, preferred_element_type=F32) for t in _split3(eg_all))
    tok = lax.broadcasted_iota(jnp.int32, (tb, 1), 0)
    tok2 = lax.broadcasted_iota(jnp.int32, (2 * tb, 1), 0)
    for h in range(GDN_HEADS):
        lo = h * GDN_DK
        q = y[:, lo:lo + GDN_DK]
        k = y[:, GDN_QK + lo:GDN_QK + lo + GDN_DK]
        v = y[:, 2 * GDN_QK + h * GDN_DV:2 * GDN_QK + (h + 1) * GDN_DV]
        q = q * (lax.rsqrt(jnp.sum(q * q, axis=-1, keepdims=True) + RMS_EPS) * GDN_DK ** -0.5)
        k = k * lax.rsqrt(jnp.sum(k * k, axis=-1, keepdims=True) + RMS_EPS)
        beta = beta_all[:, h:h + 1]
        eg = eg_all[:, GDN_HEADS + h:GDN_HEADS + h + 1]
        qk = jnp.sum(q * k, axis=-1, keepdims=True)
        kq = jnp.concatenate([k, q], axis=0).astype(BF16)
        kq_s = jnp.zeros((2 * tb, GDN_DV), F32)
        for t in range(tb):
            r = jnp.dot(kq, s_ref[0, t, h].astype(BF16), preferred_element_type=F32)
            kq_s = jnp.where(tok2 % tb == t, r, kq_s)
        u = beta * (v - eg * kq_s[:tb])
        o = eg * kq_s[tb:] + qk * u
        kb = k.astype(BF16)
        for t in range(tb):
            outer = lax.dot_general(kb, jnp.where(tok == t, u, 0.0).astype(BF16), (((0,), (0,)), ((), ())),
                                    preferred_element_type=F32)
            so_ref[0, t, h] = s_ref[0, t, h] * eg_wide[t:t + 1, h * GDN_DV:(h + 1) * GDN_DV] + outer
        o = o * lax.rsqrt(jnp.mean(o * o, axis=-1, keepdims=True) + RMS_EPS) * normw_ref[...]
        o_ref[:, h * GDN_DV:(h + 1) * GDN_DV] = o * _silu(gate_ref[:, h * GDN_DV:(h + 1) * GDN_DV])


def _carried(new_states, shape, operand_index):
    if new_states is None:
        return jnp.zeros((8, 128), F32), {}
    assert new_states.shape == shape
    return new_states, {operand_index: 1}


def gdn_decode(a_qkv, a_gate, small, conv_state_t, state, new_states, layer, conv_w, a_log, dt_bias, norm_w):
    bsz = a_qkv.shape[0]
    prev, aliases = _carried(new_states, state.shape, 5)
    tb = DEC_TB
    lanes = small.shape[-1]
    alog_row = jnp.zeros((1, lanes), F32).at[0, GDN_HEADS:2 * GDN_HEADS].set(a_log)
    dtb_row = jnp.zeros((1, lanes), F32).at[0, GDN_HEADS:2 * GDN_HEADS].set(dt_bias)
    nprev = CONV_WIDTH - 1
    rows = lambda wd: pl.BlockSpec((tb, wd), lambda i: (i, 0))
    const2 = lambda i: (0, 0)
    return pl.pallas_call(
        _gdn_decode_kernel,
        out_shape=(jax.ShapeDtypeStruct((bsz, GDN_VW), F32),
                   jax.ShapeDtypeStruct(state.shape, F32),
                   jax.ShapeDtypeStruct((nprev, bsz, GDN_CONV_DIM), F32)),
        grid=(bsz // tb,),
        in_specs=[rows(GDN_CONV_DIM), rows(GDN_VW), rows(lanes),
                  pl.BlockSpec((1, nprev, tb, GDN_CONV_DIM), lambda i: (layer, 0, i, 0)),
                  pl.BlockSpec((1, tb, GDN_HEADS, GDN_DK, GDN_DV), lambda i: (layer, i, 0, 0, 0)),
                  pl.BlockSpec(memory_space=pl.ANY),
                  pl.BlockSpec((CONV_WIDTH, GDN_CONV_DIM), const2),
                  pl.BlockSpec((1, lanes), const2), pl.BlockSpec((1, lanes), const2),
                  pl.BlockSpec((1, GDN_DV), const2)],
        out_specs=(rows(GDN_VW),
                   pl.BlockSpec((1, tb, GDN_HEADS, GDN_DK, GDN_DV), lambda i: (layer, i, 0, 0, 0)),
                   pl.BlockSpec((nprev, tb, GDN_CONV_DIM), lambda i: (0, i, 0))),
        input_output_aliases=aliases,
        compiler_params=pltpu.CompilerParams(
            dimension_semantics=("arbitrary",), vmem_limit_bytes=VMEM_LIMIT_BYTES),
        name="gdn_decode",
    )(a_qkv, a_gate, small, conv_state_t, state, prev, conv_w, alog_row, dtb_row, norm_w.reshape(1, GDN_DV))


def _ssd_decode_kernel(xbc_ref, z_ref, small_ref, cst_ref, h_ref, carried_ref, convw_ref, convb_ref, alog_ref,
                       dtb_ref, d_ref, normw_ref, o_ref, ho_ref, csto_ref):
    del carried_ref
    tb = xbc_ref.shape[0]
    x = xbc_ref[...]
    y, new_cst = _conv_step(cst_ref, x, convw_ref)
    for i, r in enumerate(new_cst):
        csto_ref[i] = r
    y = _silu(y + convb_ref[...])
    xs = y[:, :SSD_INNER]
    bm = y[:, SSD_INNER:SSD_INNER + SSD_BC]
    cm = y[:, SSD_INNER + SSD_BC:]
    dt_all = _softplus(small_ref[...] + dtb_ref[...])
    eda_all = jnp.exp(dt_all * (-jnp.exp(alog_ref[...])))
    lanes = dt_all.shape[-1]

    def spread(v, width):
        r = lax.broadcasted_iota(jnp.int32, (lanes, SSD_HEADS * width), 0)
        c = lax.broadcasted_iota(jnp.int32, (lanes, SSD_HEADS * width), 1)
        sel = (r == SSD_DT_LANE + c // width).astype(BF16)
        return sum(jnp.dot(t, sel, preferred_element_type=F32) for t in _split3(v))

    eda_wide = spread(eda_all, SSD_STATE)
    xd = xs * spread(dt_all, SSD_HEADDIM)
    tok = lax.broadcasted_iota(jnp.int32, (tb, 1), 0)
    gp = SSD_HPG * SSD_HEADDIM
    ys = []
    for g in range(SSD_GROUPS):
        bg = bm[:, g * SSD_STATE:(g + 1) * SSD_STATE]
        cg = cm[:, g * SSD_STATE:(g + 1) * SSD_STATE].astype(BF16)
        xd_g = xd[:, g * gp:(g + 1) * gp].astype(BF16)
        y_g = jnp.zeros((tb, gp), F32)
        for t in range(tb):
            outer = lax.dot_general(xd_g, jnp.where(tok == t, bg, 0.0).astype(BF16), (((0,), (0,)), ((), ())),
                                    preferred_element_type=F32)
            hns = []
            for hh in range(SSD_HPG):
                h = g * SSD_HPG + hh
                hn = (h_ref[0, t, h] * eda_wide[t:t + 1, h * SSD_STATE:(h + 1) * SSD_STATE]
                      + outer[hh * SSD_HEADDIM:(hh + 1) * SSD_HEADDIM])
                ho_ref[0, t, h] = hn
                hns.append(hn.astype(BF16))
            y_t = lax.dot_general(cg, jnp.concatenate(hns, axis=0), (((1,), (1,)), ((), ())),
                                  preferred_element_type=F32)
            y_g = jnp.where(tok == t, y_t, y_g)
        ys.append(y_g)
    yf = jnp.concatenate(ys, axis=-1) + d_ref[...] * xs
    yf = yf * _silu(z_ref[...])
    gw = SSD_INNER // SSD_GROUPS
    for g in range(SSD_GROUPS):
        seg = yf[:, g * gw:(g + 1) * gw]
        o_ref[:, g * gw:(g + 1) * gw] = (seg * lax.rsqrt(jnp.mean(seg * seg, axis=-1, keepdims=True) + RMS_EPS)
                                         * normw_ref[:, g * gw:(g + 1) * gw])


def ssd_decode(b_xbc, b_z, small, conv_state_t, state, new_states, layer, conv_w, conv_b, a_log, dt_bias, d_skip,
               norm_w):
    bsz = b_xbc.shape[0]
    prev, aliases = _carried(new_states, state.shape, 5)
    tb = DEC_TB
    lanes = small.shape[-1]
    sl = slice(SSD_DT_LANE, SSD_DT_LANE + SSD_HEADS)
    alog_row = jnp.zeros((1, lanes), F32).at[0, sl].set(a_log)
    dtb_row = jnp.zeros((1, lanes), F32).at[0, sl].set(dt_bias)
    d_row = jnp.repeat(d_skip, SSD_HEADDIM).reshape(1, SSD_INNER)
    nprev = CONV_WIDTH - 1
    rows = lambda wd: pl.BlockSpec((tb, wd), lambda i: (i, 0))
    const2 = lambda i: (0, 0)
    return pl.pallas_call(
        _ssd_decode_kernel,
        out_shape=(jax.ShapeDtypeStruct((bsz, SSD_INNER), F32),
                   jax.ShapeDtypeStruct(state.shape, F32),
                   jax.ShapeDtypeStruct((nprev, bsz, SSD_CONV_DIM), F32)),
        grid=(bsz // tb,),
        in_specs=[rows(SSD_CONV_DIM), rows(SSD_INNER), rows(lanes),
                  pl.BlockSpec((1, nprev, tb, SSD_CONV_DIM), lambda i: (layer, 0, i, 0)),
                  pl.BlockSpec((1, tb, SSD_HEADS, SSD_HEADDIM, SSD_STATE), lambda i: (layer, i, 0, 0, 0)),
                  pl.BlockSpec(memory_space=pl.ANY),
                  pl.BlockSpec((CONV_WIDTH, SSD_CONV_DIM), const2), pl.BlockSpec((1, SSD_CONV_DIM), const2),
                  pl.BlockSpec((1, lanes), const2), pl.BlockSpec((1, lanes), const2),
                  pl.BlockSpec((1, SSD_INNER), const2), pl.BlockSpec((1, SSD_INNER), const2)],
        out_specs=(rows(SSD_INNER),
                   pl.BlockSpec((1, tb, SSD_HEADS, SSD_HEADDIM, SSD_STATE), lambda i: (layer, i, 0, 0, 0)),
                   pl.BlockSpec((nprev, tb, SSD_CONV_DIM), lambda i: (0, i, 0))),
        input_output_aliases=aliases,
        compiler_params=pltpu.CompilerParams(
            dimension_semantics=("arbitrary",), vmem_limit_bytes=VMEM_LIMIT_BYTES),
        name="ssd_decode",
    )(b_xbc, b_z, small, conv_state_t, state, prev, conv_w, conv_b.reshape(1, SSD_CONV_DIM),
      alog_row, dtb_row, d_row, norm_w.reshape(1, SSD_INNER))


ATT_DEC_TB = 8
ATT_DEC_TOK = 2


def _attn_decode_tokens(q_ref, kvn_refs, c_refs, k, acc, lane0):
    tb = q_ref.shape[0]
    ntok = c_refs[0].shape[1]
    hd = ATT_HEAD_DIM
    slopes = _alibi_slopes_np()
    q_t = q_ref[...].T
    kvn_t = [ref[...].T for ref in kvn_refs]
    lane_t = lax.broadcasted_iota(jnp.int32, (1, tb), 1)

    def column(x_t, tok):
        return jnp.sum(jnp.where(lane_t == tok, x_t, 0.0), axis=1, keepdims=True)

    biases = []
    for gi, (win, dil) in enumerate(ATT_GROUPS):
        dist = win - lax.broadcasted_iota(jnp.int32, (ATT_HEADS_PER_GROUP, win), 1)
        head = lax.broadcasted_iota(jnp.int32, (ATT_HEADS_PER_GROUP, win), 0)
        slope = functools.reduce(lambda acc_, hh: jnp.where(head == hh, float(slopes[gi, hh]), acc_),
                                 range(ATT_HEADS_PER_GROUP), jnp.zeros((ATT_HEADS_PER_GROUP, win), F32))
        biases.append(jnp.where(dist % dil == 0, -slope * dist.astype(F32), ATT_MASKED))
    lane = lax.broadcasted_iota(jnp.int32, acc.shape, 1)
    for tt in range(ntok):
        tok = k * ntok + tt
        q_col = column(q_t, tok)
        og, lg = [], []
        for gi, c_ref in enumerate(c_refs):
            kvn_col = column(kvn_t[gi], tok)
            heads = range(ATT_HEADS_PER_GROUP)
            qcs = [q_col[gi * ATT_OUT + hh * hd:gi * ATT_OUT + (hh + 1) * hd] for hh in heads]
            kns = [kvn_col[hh * hd:(hh + 1) * hd] for hh in heads]
            vns = [kvn_col[ATT_OUT + hh * hd:ATT_OUT + (hh + 1) * hd] for hh in heads]
            s = jnp.concatenate([jnp.sum(c_ref[0, tt, 0, hh] * qcs[hh], axis=0, keepdims=True) for hh in heads],
                                axis=0) + biases[gi]
            s_new = jnp.concatenate([jnp.sum(qcs[hh] * kns[hh], axis=0, keepdims=True) for hh in heads], axis=0)
            m = jnp.maximum(jnp.max(s, axis=-1, keepdims=True), s_new)
            p = jnp.exp(s - m)
            p_new = jnp.exp(s_new - m)
            inv_l = 1.0 / (jnp.sum(p, axis=-1, keepdims=True) + p_new)
            og.append([(jnp.sum(c_ref[0, tt, 1, hh] * p[hh:hh + 1], axis=-1, keepdims=True)
                        + p_new[hh:hh + 1] * vns[hh]) * inv_l[hh:hh + 1] for hh in heads])
            lg.append(m - jnp.log(inv_l))
        m = functools.reduce(jnp.maximum, lg)
        es = [jnp.exp(l - m) for l in lg]
        inv = 1.0 / sum(es)
        ws = [e * inv for e in es]
        o_col = jnp.concatenate([sum(ws[gi][hh:hh + 1] * og[gi][hh] for gi in range(N_ATT_GROUPS))
                                 for hh in range(ATT_HEADS_PER_GROUP)], axis=0)
        acc = jnp.where(lane == lane0 + tt, o_col, acc)
    return acc


def _ffn_attn_decode_kernel(h_ref, nw_ref, wi_ref, wo_ref, q_ref, kvn0_ref, kvn1_ref, kvn2_ref,
                            c0_ref, c1_ref, c2_ref, o_ref, oct_ref):
    step = pl.program_id(0)
    ntok = c0_ref.shape[1]
    steps_per_block = q_ref.shape[0] // ntok

    @pl.when(step == 0)
    def _():
        oct_ref[...] = jnp.zeros_like(oct_ref)

    oct_ref[...] = _attn_decode_tokens(q_ref, (kvn0_ref, kvn1_ref, kvn2_ref), (c0_ref, c1_ref, c2_ref),
                                       step % steps_per_block, oct_ref[...], step * ntok)
    _ffn_kernel(h_ref, nw_ref, wi_ref, wo_ref, o_ref)


def ffn_attn_decode(h, norm_w, w_in_bf16, w_out_bf16, ffn_layer, q_rows, kv_new, caches, layer):
    t = h.shape[0]
    bsz = q_rows.shape[0]
    tb, ntok = ATT_DEC_TB, ATT_DEC_TOK
    steps = bsz // ntok
    assert t % steps == 0 and bsz % tb == 0 and tb % ntok == 0 and bsz % 128 == 0
    tm = t // steps
    views, specs = [], []
    for cache, (win, dil) in zip(caches, ATT_GROUPS):
        assert cache.shape[2] == win and win % dil == 0
        views.append(jnp.transpose(cache, (0, 1, 3, 4, 5, 2)))
        specs.append(pl.BlockSpec((1, ntok) + KV_TAIL + (win,), lambda s: (layer, s, 0, 0, 0, 0)))
    blk = lambda s: (s // (tb // ntok), 0)
    y, oc_t = pl.pallas_call(
        _ffn_attn_decode_kernel,
        out_shape=(jax.ShapeDtypeStruct(h.shape, h.dtype), jax.ShapeDtypeStruct((ATT_OUT, bsz), F32)),
        grid=(steps,),
        in_specs=([pl.BlockSpec((tm, D_MODEL), lambda s: (s, 0)),
                   _layer_spec(norm_w, ffn_layer), _layer_spec(w_in_bf16, ffn_layer),
                   _layer_spec(w_out_bf16, ffn_layer),
                   pl.BlockSpec((tb, ATT_Q_N), blk)]
                  + [pl.BlockSpec((tb, 2 * ATT_OUT), blk)] * N_ATT_GROUPS + specs),
        out_specs=(pl.BlockSpec((tm, D_MODEL), lambda s: (s, 0)),
                   pl.BlockSpec((ATT_OUT, bsz), lambda s: (0, 0))),
        compiler_params=pltpu.CompilerParams(
            dimension_semantics=("arbitrary",), vmem_limit_bytes=VMEM_LIMIT_BYTES),
        name="ffn_attn_decode",
    )(h, norm_w, w_in_bf16, w_out_bf16, q_rows, *kv_new, *views)
    return y, oc_t.T


PROMPT_TM = 512
PROMPT_FFN_TM = 512
KV_TAIL = (2, ATT_HEADS_PER_GROUP, ATT_HEAD_DIM)


def trunk(x_prompt, x_sample, p, state_gdn, state_gdn_conv, state_ssd, state_ssd_conv, caches):
    bsz, L, _ = x_prompt.shape
    t = bsz * L
    h = x_prompt.reshape(t, D_MODEL)
    new = [[] for _ in range(4 + N_ATT_GROUPS)]
    seq = lambda a: a.reshape(bsz, L, a.shape[-1])
    bs = x_sample.shape[0]
    assert x_sample.shape[1] == 1
    hs = x_sample.reshape(bs, D_MODEL)
    new_s = [[] for _ in range(2 + N_ATT_GROUPS)]
    gdn_conv_t = jnp.transpose(state_gdn_conv, (0, 2, 1, 3))
    ssd_conv_t = jnp.transpose(state_ssd_conv, (0, 2, 1, 3))
    s_gdn_s = s_ssd_s = None
    for l in range(DEPTH):
        hs = ffn_block(hs, p['ffn1_norm'], p['ffn1_w_in'], p['ffn1_w_out'], l, bs)
        qkv, gate, z, xbc, small = proj_ab(hs, p['mix_norm'], p['w_ab'], l, bs)
        q_rows, *kvrows_s = proj_c(hs, p['mix_norm'], p['w_c'], l, p['q_norm'][l], p['k_norm'][l], bs, False)
        o_a_s, s_gdn_s, c_gdn_s = gdn_decode(qkv, gate, small, gdn_conv_t, state_gdn, s_gdn_s, l, p['gdn_conv_w'][l],
                                             p['gdn_a_log'][l], p['gdn_dt_bias'][l], p['gdn_norm_w'][l])
        o_b_s, s_ssd_s, c_ssd_s = ssd_decode(xbc, z, small, ssd_conv_t, state_ssd, s_ssd_s, l, p['ssd_conv_w'][l],
                                             p['ssd_conv_b'][l], p['ssd_a_log'][l], p['ssd_dt_bias'][l],
                                             p['ssd_d'][l], p['ssd_norm_w'][l])
        h, o_c_s = ffn_attn_decode(h, p['ffn1_norm'], p['ffn1_w_in'], p['ffn1_w_out'], l, q_rows, kvrows_s, caches, l)
        hs = merge_block(hs, o_a_s, o_b_s, o_c_s, p['merge'], l, bs)
        hs = ffn_block(hs, p['ffn2_norm'], p['ffn2_w_in'], p['ffn2_w_out'], l, bs)
        for lst, s in zip(new_s, (c_gdn_s, c_ssd_s, *[r.reshape((bs, 1) + KV_TAIL) for r in kvrows_s])):
            lst.append(s)

        qkv, gate, z, xbc, small = proj_ab(h, p['mix_norm'], p['w_ab'], l, PROMPT_TM)
        q5, kv5, *kvrows = proj_c(h, p['mix_norm'], p['w_c'], l, p['q_norm'][l], p['k_norm'][l], PROMPT_TM, True)
        o_a, s_gdn, c_gdn = gdn_prompt(seq(qkv), seq(gate), seq(small), p['gdn_conv_w'][l], p['gdn_a_log'][l],
                                       p['gdn_dt_bias'][l], p['gdn_norm_w'][l])
        o_b, s_ssd, c_ssd = ssd_prompt(seq(xbc), seq(z), seq(small), p['ssd_conv_w'][l], p['ssd_conv_b'][l],
                                       p['ssd_a_log'][l], p['ssd_dt_bias'][l], p['ssd_d'][l], p['ssd_norm_w'][l])
        o5, lse5 = attn_prompt(q5, kv5, bsz)
        h = merge_block(h, o_a.reshape(t, GDN_VW), o_b.reshape(t, SSD_INNER), (o5, lse5), p['merge'], l, PROMPT_TM)
        h = ffn_block(h, p['ffn2_norm'], p['ffn2_w_in'], p['ffn2_w_out'], l, PROMPT_FFN_TM)
        kv_new = [kvrows[gi].reshape(bsz, L, 2 * ATT_OUT)[:, L - min(win, L):].reshape((bsz, min(win, L)) + KV_TAIL)
                  for gi, (win, _) in enumerate(ATT_GROUPS)]
        for lst, s in zip(new, (s_gdn, c_gdn, s_ssd, c_ssd, *kv_new)):
            lst.append(s)
    c_gdn_s, c_ssd_s, *kv_new_s = [jnp.stack(lst, axis=0) for lst in new_s]
    st_s = [s_gdn_s, jnp.transpose(c_gdn_s, (0, 2, 1, 3)), s_ssd_s, jnp.transpose(c_ssd_s, (0, 2, 1, 3)), *kv_new_s]
    return (h.reshape(bsz, L, D_MODEL), [jnp.stack(lst, axis=0) for lst in new],
            hs.reshape(bs, 1, D_MODEL), st_s)


def kernel(x_prompt, x_sample, state_gdn, state_gdn_conv, state_ssd, state_ssd_conv,
           cache_kv_w128, cache_kv_w512, cache_kv_w2048,
           ffn1_norm, ffn1_w_in, ffn1_w_out, mix_norm, w_in, gate_b,
           gdn_conv_w, gdn_a_log, gdn_dt_bias, gdn_norm_w,
           ssd_conv_w, ssd_conv_b, ssd_a_log, ssd_dt_bias, ssd_d, ssd_norm_w,
           q_norm, k_norm, w_br_a, w_br_b, w_br_c, w_out,
           ffn2_norm, ffn2_w_in, ffn2_w_out):
    w_ab, w_c, w_g = prep_weights(w_in)
    row = lambda a: a.reshape(DEPTH, 1, a.shape[-1])
    mix_norm_r = row(mix_norm)
    p = {'ffn1_norm': row(ffn1_norm), 'ffn1_w_in': ffn1_w_in.astype(BF16), 'ffn1_w_out': ffn1_w_out.astype(BF16),
         'mix_norm': mix_norm_r, 'w_ab': w_ab, 'w_c': w_c,
         'merge': (mix_norm_r, w_g, row(gate_b), w_br_a.astype(BF16), w_br_b.astype(BF16), w_br_c.astype(BF16),
                   w_out.astype(BF16)),
         'gdn_conv_w': gdn_conv_w, 'gdn_a_log': gdn_a_log, 'gdn_dt_bias': gdn_dt_bias, 'gdn_norm_w': gdn_norm_w,
         'ssd_conv_w': ssd_conv_w, 'ssd_conv_b': ssd_conv_b, 'ssd_a_log': ssd_a_log,
         'ssd_dt_bias': ssd_dt_bias, 'ssd_d': ssd_d, 'ssd_norm_w': ssd_norm_w,
         'q_norm': q_norm, 'k_norm': k_norm,
         'ffn2_norm': row(ffn2_norm), 'ffn2_w_in': ffn2_w_in.astype(BF16), 'ffn2_w_out': ffn2_w_out.astype(BF16)}
    y_prompt, st_p, y_sample, st_s = trunk(x_prompt, x_sample, p, state_gdn, state_gdn_conv, state_ssd,
                                           state_ssd_conv, [cache_kv_w128, cache_kv_w512, cache_kv_w2048])
    gdn_p, gdn_conv_p, ssd_p, ssd_conv_p, kv128_p, kv512_p, kv2048_p = st_p
    gdn_s, gdn_conv_s, ssd_s, ssd_conv_s, kv128_s, kv512_s, kv2048_s = st_s
    return (y_prompt, y_sample, gdn_p, gdn_s, gdn_conv_p, gdn_conv_s, ssd_p, ssd_s, ssd_conv_p, ssd_conv_s,
            kv128_p, kv128_s, kv512_p, kv512_s, kv2048_p, kv2048_s)
```

```python
import functools
import math
import jax, jax.numpy as jnp
from jax import lax
import numpy as np
from jax.experimental import pallas as pl
from jax.experimental.pallas import tpu as pltpu

D_MODEL = 1024
BATCH = 8
SEQ = 2048
DEPTH = 4
DEC_BATCH = 128
DEC_SEQ = 1
PAST_LEN = 2048

CONV_WIDTH = 4
RMS_EPS = 1e-6
GDN_HEADS = 4
GDN_DK = 128
GDN_DV = 128
GDN_CHUNK = 64
GDN_QK = GDN_HEADS * GDN_DK
GDN_VW = GDN_HEADS * GDN_DV
GDN_CONV_DIM = 2 * GDN_QK + GDN_VW
SSD_HEADS = 8
SSD_HEADDIM = 64
SSD_GROUPS = 2
SSD_STATE = 128
SSD_CHUNK = 128
SSD_INNER = SSD_HEADS * SSD_HEADDIM
SSD_CONV_DIM = SSD_INNER + 2 * SSD_GROUPS * SSD_STATE
ATT_GROUPS = ((128, 1), (512, 4), (2048, 16))
N_ATT_GROUPS = len(ATT_GROUPS)
ATT_HEADS_PER_GROUP = 4
ATT_HEAD_DIM = 64
ATT_HEADS = N_ATT_GROUPS * ATT_HEADS_PER_GROUP
ATT_OUT = ATT_HEADS_PER_GROUP * ATT_HEAD_DIM
D_FF = ((8 * D_MODEL // 3 + 127) // 128) * 128
N_BRANCH = 3
IN_SPLITS = (GDN_CONV_DIM, GDN_VW, GDN_HEADS, GDN_HEADS, SSD_INNER, SSD_CONV_DIM, SSD_HEADS,
             3 * ATT_HEADS * ATT_HEAD_DIM, N_BRANCH * D_MODEL)
D_IN = sum(IN_SPLITS)

F32 = jnp.float32
BF16 = jnp.bfloat16
VMEM_LIMIT_BYTES = 56 * 1024 * 1024


def _layer_spec(stacked, layer):
    zeros = (0,) * (stacked.ndim - 1)
    return pl.BlockSpec((1,) + stacked.shape[1:], lambda *_: (layer,) + zeros, pipeline_mode=pl.Buffered(1))


def _ffn_kernel(h_ref, nw_ref, wi_ref, wo_ref, o_ref):
    h = h_ref[...]
    xn = h * lax.rsqrt(jnp.mean(h * h, axis=-1, keepdims=True) + RMS_EPS) * nw_ref[0]
    gu = jnp.dot(xn.astype(BF16), wi_ref[0], preferred_element_type=F32)
    g, up = gu[:, :D_FF], gu[:, D_FF:]
    act = (g * jax.nn.sigmoid(g) * up).astype(BF16)
    y = jnp.dot(act, wo_ref[0], preferred_element_type=F32)
    o_ref[...] = h + 0.5 * y


def ffn_block(h, norm_w, w_in_bf16, w_out_bf16, layer, tm):
    t = h.shape[0]
    assert t % tm == 0
    return pl.pallas_call(
        _ffn_kernel,
        out_shape=jax.ShapeDtypeStruct(h.shape, h.dtype),
        grid=(t // tm,),
        in_specs=[pl.BlockSpec((tm, D_MODEL), lambda i: (i, 0)),
                  _layer_spec(norm_w, layer), _layer_spec(w_in_bf16, layer), _layer_spec(w_out_bf16, layer)],
        out_specs=pl.BlockSpec((tm, D_MODEL), lambda i: (i, 0)),
        compiler_params=pltpu.CompilerParams(
            dimension_semantics=("arbitrary",), vmem_limit_bytes=VMEM_LIMIT_BYTES),
        name="ffn",
    )(h, norm_w, w_in_bf16, w_out_bf16)


_DN_BMM = (((2,), (1,)), ((0,), (0,)))
_DN_BMM_NT = (((2,), (2,)), ((0,), (0,)))


def _bmm(a, b):
    return lax.dot_general(a.astype(BF16), b.astype(BF16), _DN_BMM, preferred_element_type=F32)


def _bmm_nt(a, b):
    return lax.dot_general(a.astype(BF16), b.astype(BF16), _DN_BMM_NT, preferred_element_type=F32)


def _split2(x):
    hi = x.astype(BF16)
    return hi, (x - hi.astype(F32)).astype(BF16)


def _split3(x):
    x1 = x.astype(BF16)
    r1 = x - x1.astype(F32)
    x2 = r1.astype(BF16)
    x3 = (r1 - x2.astype(F32)).astype(BF16)
    return x1, x2, x3


def _bmm_hi(a, b):
    a1, a2 = _split2(a)
    b1, b2 = _split2(b)
    dot = functools.partial(lax.dot_general, dimension_numbers=_DN_BMM, preferred_element_type=F32)
    return dot(a1, b1) + (dot(a1, b2) + dot(a2, b1))


def _cumsum_rows(tri, x):
    dot = functools.partial(lax.dot_general, dimension_numbers=_DN_BMM, preferred_element_type=F32)
    x1, x2, x3 = _split3(x)
    return dot(tri, x1) + (dot(tri, x2) + dot(tri, x3))


def _softplus(x):
    return jnp.maximum(x, 0.0) + jnp.log(1.0 + jnp.exp(-jnp.abs(x)))


def _silu(x):
    return x * jax.nn.sigmoid(x)


GDN_INV_BLOCK = 16


def _unit_lower_solve(a, rhs, row, col, mm):
    c = a.shape[-1]
    eye = (row == col).astype(F32)
    same_blk = (row // GDN_INV_BLOCK) == (col // GDN_INV_BLOCK)
    d = jnp.where(same_blk, a, 0.0)
    n = a - d
    x = eye - d
    pw = d
    k = 2
    while k < GDN_INV_BLOCK:
        pw = mm(pw, pw)
        x = x + mm(x, pw)
        k *= 2
    m = mm(x, n)
    y = mm(x, rhs)
    nblk = c // GDN_INV_BLOCK
    q = eye - m
    pw = m
    k = 2
    while k < nblk:
        pw = mm(pw, pw)
        q = q + mm(q, pw)
        k *= 2
    return mm(q, y)


def _gdn_prompt_kernel(qkv_ref, gate_ref, small_ref, convw_ref, alog_ref, dtb_ref, normw_ref,
                       o_ref, s_ref, cst_ref, xp_ref):
    c = GDN_CHUNK
    nb = qkv_ref.shape[0]

    @pl.when(pl.program_id(0) == 0)
    def _():
        s_ref[...] = jnp.zeros_like(s_ref)
        xp_ref[:, 0:8, :] = jnp.zeros((nb, 8, GDN_CONV_DIM), F32)

    x = qkv_ref[...]
    xp_ref[:, 8:8 + c, :] = x
    w = convw_ref[...]
    y = (xp_ref[:, 5:5 + c, :] * w[0:1, :] + xp_ref[:, 6:6 + c, :] * w[1:2, :]
         + xp_ref[:, 7:7 + c, :] * w[2:3, :] + x * w[3:4, :])
    xp_ref[:, 0:8, :] = xp_ref[:, c:c + 8, :]
    cst_ref[...] = x[:, c - (CONV_WIDTH - 1):, :]
    y = _silu(y)

    small = small_ref[...]
    beta_all = jax.nn.sigmoid(small)
    g_all = -jnp.exp(alog_ref[...]) * _softplus(small + dtb_ref[...])
    row = lax.broadcasted_iota(jnp.int32, (c, c), 0)
    col = lax.broadcasted_iota(jnp.int32, (c, c), 1)
    incl = row >= col
    strict = row > col
    tri = jnp.broadcast_to(incl.astype(BF16), (nb, c, c))
    gam_all = _cumsum_rows(tri, g_all)
    gam_all_t = jnp.swapaxes(gam_all, 1, 2)
    egam_all = jnp.exp(gam_all)

    for h in range(GDN_HEADS):
        lo = h * GDN_DK
        q = y[:, :, lo:lo + GDN_DK]
        k = y[:, :, GDN_QK + lo:GDN_QK + lo + GDN_DK]
        v = y[:, :, 2 * GDN_QK + h * GDN_DV:2 * GDN_QK + (h + 1) * GDN_DV]
        q = q * (lax.rsqrt(jnp.sum(q * q, axis=-1, keepdims=True) + RMS_EPS) * GDN_DK ** -0.5)
        k = k * lax.rsqrt(jnp.sum(k * k, axis=-1, keepdims=True) + RMS_EPS)
        beta = beta_all[:, :, h:h + 1]
        gam = gam_all[:, :, GDN_HEADS + h:GDN_HEADS + h + 1]
        gam_row = gam_all_t[:, GDN_HEADS + h:GDN_HEADS + h + 1, :]
        egam = egam_all[:, :, GDN_HEADS + h:GDN_HEADS + h + 1]
        gam_last = gam_row[:, :, c - 1:c]
        dec = jnp.where(incl, jnp.exp(jnp.minimum(gam - gam_row, 0.0)), 0.0)
        kk = _bmm_nt(k, k)
        a_mat = jnp.where(strict, beta * kk * dec, 0.0)
        rhs = jnp.concatenate([v * beta, k * (beta * egam)], axis=-1)
        sol = _unit_lower_solve(a_mat, rhs, row, col, _bmm)
        u0, wk = sol[:, :, :GDN_DV], sol[:, :, GDN_DV:]
        qk = _bmm_nt(q, k) * dec
        q_dec = q * egam
        k_dec = k * jnp.exp(gam_last - gam)
        s = s_ref[:, h]
        u = u0 - _bmm(wk, s)
        o = _bmm(q_dec, s) + _bmm(qk, u)
        s_ref[:, h] = s * jnp.exp(gam_last) + _bmm(jnp.swapaxes(k_dec, 1, 2), u)
        o = o * lax.rsqrt(jnp.mean(o * o, axis=-1, keepdims=True) + RMS_EPS) * normw_ref[...]
        o_ref[:, :, h * GDN_DV:(h + 1) * GDN_DV] = o * _silu(gate_ref[:, :, h * GDN_DV:(h + 1) * GDN_DV])


def _gdn_prompt_job(a_qkv, a_gate, small, conv_w, a_log, dt_bias, norm_w):
    bsz, L, _ = a_qkv.shape
    c = GDN_CHUNK
    assert L % c == 0
    lanes = small.shape[-1]
    alog_row = jnp.zeros((1, lanes), F32).at[0, GDN_HEADS:2 * GDN_HEADS].set(a_log)
    dtb_row = jnp.zeros((1, lanes), F32).at[0, GDN_HEADS:2 * GDN_HEADS].set(dt_bias)
    const2 = lambda n: (0, 0)
    return dict(
        kernel=_gdn_prompt_kernel,
        steps=L // c,
        args=[a_qkv, a_gate, small, conv_w, alog_row, dtb_row, norm_w.reshape(1, GDN_DV)],
        in_specs=[
            pl.BlockSpec((bsz, c, GDN_CONV_DIM), lambda n: (0, n, 0)),
            pl.BlockSpec((bsz, c, GDN_VW), lambda n: (0, n, 0)),
            pl.BlockSpec((bsz, c, lanes), lambda n: (0, n, 0)),
            pl.BlockSpec((CONV_WIDTH, GDN_CONV_DIM), const2),
            pl.BlockSpec((1, lanes), const2),
            pl.BlockSpec((1, lanes), const2),
            pl.BlockSpec((1, GDN_DV), const2),
        ],
        out_shape=[jax.ShapeDtypeStruct((bsz, L, GDN_VW), F32),
                   jax.ShapeDtypeStruct((bsz, GDN_HEADS, GDN_DK, GDN_DV), F32),
                   jax.ShapeDtypeStruct((bsz, CONV_WIDTH - 1, GDN_CONV_DIM), F32)],
        out_specs=[
            pl.BlockSpec((bsz, c, GDN_VW), lambda n: (0, n, 0)),
            pl.BlockSpec((bsz, GDN_HEADS, GDN_DK, GDN_DV), lambda n: (0, 0, 0, 0)),
            pl.BlockSpec((bsz, CONV_WIDTH - 1, GDN_CONV_DIM), lambda n: (0, 0, 0)),
        ],
        scratch=[pltpu.VMEM((bsz, c + 8, GDN_CONV_DIM), F32)])


def run_jobs(name, *jobs):
    steps = jobs[0]['steps']
    assert all(j['steps'] == steps for j in jobs)
    n_in = [len(j['args']) for j in jobs]
    n_out = [len(j['out_shape']) for j in jobs]
    n_scr = [len(j['scratch']) for j in jobs]

    def body(*refs):
        ins, outs, scr = refs[:sum(n_in)], refs[sum(n_in):sum(n_in) + sum(n_out)], refs[sum(n_in) + sum(n_out):]
        for idx, j in enumerate(jobs):
            a, b, c = sum(n_in[:idx]), sum(n_out[:idx]), sum(n_scr[:idx])
            j['kernel'](*ins[a:a + n_in[idx]], *outs[b:b + n_out[idx]], *scr[c:c + n_scr[idx]])

    aliases = {sum(n_in[:idx]) + a: sum(n_out[:idx]) + o
               for idx, j in enumerate(jobs) for a, o in j.get('aliases', {}).items()}
    res = pl.pallas_call(
        body,
        out_shape=tuple(s for j in jobs for s in j['out_shape']),
        grid=(steps,),
        in_specs=[s for j in jobs for s in j['in_specs']],
        out_specs=tuple(s for j in jobs for s in j['out_specs']),
        scratch_shapes=[s for j in jobs for s in j['scratch']],
        input_output_aliases=aliases,
        compiler_params=pltpu.CompilerParams(
            dimension_semantics=("arbitrary",), vmem_limit_bytes=VMEM_LIMIT_BYTES),
        name=name,
    )(*[a for j in jobs for a in j['args']])
    return [res[sum(n_out[:idx]):sum(n_out[:idx]) + n] for idx, n in enumerate(n_out)]


SSD_DT_LANE = 2 * GDN_HEADS
SSD_BC = SSD_GROUPS * SSD_STATE
SSD_HPG = SSD_HEADS // SSD_GROUPS


def _ssd_prompt_kernel(xbc_ref, z_ref, small_ref, convw_ref, convb_ref, alog_ref, dtb_ref, d_ref, normw_ref,
                       o_ref, h_ref, cst_ref, xp_ref):
    c = SSD_CHUNK
    nb = xbc_ref.shape[0]

    @pl.when(pl.program_id(0) == 0)
    def _():
        h_ref[...] = jnp.zeros_like(h_ref)
        xp_ref[:, 0:8, :] = jnp.zeros((nb, 8, SSD_CONV_DIM), F32)

    x = xbc_ref[...]
    xp_ref[:, 8:8 + c, :] = x
    w = convw_ref[...]
    y = (xp_ref[:, 5:5 + c, :] * w[0:1, :] + xp_ref[:, 6:6 + c, :] * w[1:2, :]
         + xp_ref[:, 7:7 + c, :] * w[2:3, :] + x * w[3:4, :]) + convb_ref[...]
    xp_ref[:, 0:8, :] = xp_ref[:, c:c + 8, :]
    cst_ref[...] = x[:, c - (CONV_WIDTH - 1):, :]
    y = _silu(y)
    xs = y[:, :, :SSD_INNER]
    bm = y[:, :, SSD_INNER:SSD_INNER + SSD_BC]
    cm = y[:, :, SSD_INNER + SSD_BC:]

    dt_all = _softplus(small_ref[...] + dtb_ref[...])
    da_all = dt_all * (-jnp.exp(alog_ref[...]))
    row = lax.broadcasted_iota(jnp.int32, (c, c), 0)
    col = lax.broadcasted_iota(jnp.int32, (c, c), 1)
    incl = row >= col
    tri = jnp.broadcast_to(incl.astype(BF16), (nb, c, c))
    acum_all = _cumsum_rows(tri, da_all)
    acum_all_t = jnp.swapaxes(acum_all, 1, 2)
    eacum_all = jnp.exp(acum_all)

    ys = []
    for g in range(SSD_GROUPS):
        bg = bm[:, :, g * SSD_STATE:(g + 1) * SSD_STATE]
        cg = cm[:, :, g * SSD_STATE:(g + 1) * SSD_STATE]
        scores = _bmm_nt(cg, bg)
        for hh in range(SSD_HPG):
            h = g * SSD_HPG + hh
            ln = SSD_DT_LANE + h
            dtc = dt_all[:, :, ln:ln + 1]
            ac = acum_all[:, :, ln:ln + 1]
            ar = acum_all_t[:, ln:ln + 1, :]
            alast = ar[:, :, c - 1:c]
            lmat = jnp.where(incl, jnp.exp(jnp.minimum(ac - ar, 0.0)), 0.0)
            xd = xs[:, :, h * SSD_HEADDIM:(h + 1) * SSD_HEADDIM] * dtc
            y_diag = _bmm(scores * lmat, xd)
            hin = h_ref[:, h]
            y_off = _bmm_nt(cg * eacum_all[:, :, ln:ln + 1], hin)
            states = _bmm(jnp.swapaxes(xd, 1, 2), bg * jnp.exp(alast - ac))
            h_ref[:, h] = hin * jnp.exp(alast) + states
            ys.append(y_diag + y_off)
    yf = jnp.concatenate(ys, axis=-1) + d_ref[...] * xs
    yf = yf * _silu(z_ref[...])
    gw = SSD_INNER // SSD_GROUPS
    for g in range(SSD_GROUPS):
        seg = yf[:, :, g * gw:(g + 1) * gw]
        o_ref[:, :, g * gw:(g + 1) * gw] = (seg * lax.rsqrt(jnp.mean(seg * seg, axis=-1, keepdims=True) + RMS_EPS)
                                            * normw_ref[:, g * gw:(g + 1) * gw])


def ssd_prompt(b_xbc, b_z, small, conv_w, conv_b, a_log, dt_bias, d_skip, norm_w):
    bsz, L, _ = b_xbc.shape
    c = SSD_CHUNK
    assert L % c == 0
    lanes = small.shape[-1]
    sl = slice(SSD_DT_LANE, SSD_DT_LANE + SSD_HEADS)
    alog_row = jnp.zeros((1, lanes), F32).at[0, sl].set(a_log)
    dtb_row = jnp.zeros((1, lanes), F32).at[0, sl].set(dt_bias)
    d_row = jnp.repeat(d_skip, SSD_HEADDIM).reshape(1, SSD_INNER)
    const2 = lambda n: (0, 0)
    return pl.pallas_call(
        _ssd_prompt_kernel,
        out_shape=(jax.ShapeDtypeStruct((bsz, L, SSD_INNER), F32),
                   jax.ShapeDtypeStruct((bsz, SSD_HEADS, SSD_HEADDIM, SSD_STATE), F32),
                   jax.ShapeDtypeStruct((bsz, CONV_WIDTH - 1, SSD_CONV_DIM), F32)),
        grid=(L // c,),
        in_specs=[
            pl.BlockSpec((bsz, c, SSD_CONV_DIM), lambda n: (0, n, 0)),
            pl.BlockSpec((bsz, c, SSD_INNER), lambda n: (0, n, 0)),
            pl.BlockSpec((bsz, c, lanes), lambda n: (0, n, 0)),
            pl.BlockSpec((CONV_WIDTH, SSD_CONV_DIM), const2),
            pl.BlockSpec((1, SSD_CONV_DIM), const2),
            pl.BlockSpec((1, lanes), const2),
            pl.BlockSpec((1, lanes), const2),
            pl.BlockSpec((1, SSD_INNER), const2),
            pl.BlockSpec((1, SSD_INNER), const2),
        ],
        out_specs=(
            pl.BlockSpec((bsz, c, SSD_INNER), lambda n: (0, n, 0)),
            pl.BlockSpec((bsz, SSD_HEADS, SSD_HEADDIM, SSD_STATE), lambda n: (0, 0, 0, 0)),
            pl.BlockSpec((bsz, CONV_WIDTH - 1, SSD_CONV_DIM), lambda n: (0, 0, 0)),
        ),
        scratch_shapes=[pltpu.VMEM((bsz, c + 8, SSD_CONV_DIM), F32)],
        compiler_params=pltpu.CompilerParams(
            dimension_semantics=("arbitrary",), vmem_limit_bytes=VMEM_LIMIT_BYTES),
        name="ssd_prompt",
    )(b_xbc, b_z, small, conv_w, conv_b.reshape(1, SSD_CONV_DIM), alog_row, dtb_row, d_row,
      norm_w.reshape(1, SSD_INNER))


ATT_BAND = 128
ATT_MASKED = -1e30
assert all(w // d == ATT_BAND for w, d in ATT_GROUPS)


def _alibi_slopes_np():
    idx = np.arange(1, ATT_HEADS + 1, dtype=np.float32)
    return np.exp2(-8.0 * idx / ATT_HEADS).astype(np.float32).reshape(N_ATT_GROUPS, ATT_HEADS_PER_GROUP)


def _attn_bias(gi, first):
    nk = 2 * ATT_BAND
    i = lax.broadcasted_iota(jnp.int32, (ATT_BAND, nk), 0)
    j = lax.broadcasted_iota(jnp.int32, (ATT_BAND, nk), 1)
    du = i + ATT_BAND - j
    duf = du.astype(F32)
    dil = ATT_GROUPS[gi][1]
    slopes = _alibi_slopes_np()[gi]
    if first:
        inside = jnp.minimum(du, j - ATT_BAND) >= 0
    else:
        inside = jnp.abs(2 * du - ATT_BAND) <= ATT_BAND
    return [jnp.where(inside, duf * float(-slopes[hh] * dil * LOG2E), ATT_MASKED)
            for hh in range(ATT_HEADS_PER_GROUP)]


ATT_LANE_CHUNKS = ATT_OUT // 128
ATT_HEADS_PER_CHUNK = 128 // ATT_HEAD_DIM
ATT_UNROLL = (5, 4, 4)
LOG2E = math.log2(math.e)
LN2 = math.log(2.0)


def _attn_block(q, keys, vals, bias):
    nk = keys.shape[0]
    low = lax.broadcasted_iota(jnp.int32, (ATT_BAND, 128), 1) < ATT_HEAD_DIM
    q2 = q * LOG2E
    kb = keys.astype(BF16)
    va = jnp.concatenate([vals.astype(BF16), jnp.ones((nk, 128), BF16)], axis=1)
    res = []
    for hh in range(ATT_HEADS_PER_CHUNK):
        qh = jnp.where(low if hh == 0 else jnp.logical_not(low), q2, 0.0).astype(BF16)
        s = lax.dot_general(qh, kb, (((1,), (1,)), ((), ())), preferred_element_type=F32) + bias[hh]
        m = jnp.max(s, axis=-1, keepdims=True)
        p = jnp.exp2(s - m)
        oa = jnp.dot(p.astype(BF16), va, preferred_element_type=F32)
        l = oa[:, 128:]
        res.append((oa[:, :128] * (1.0 / l), (m + jnp.log2(l)) * LN2))
    return jnp.where(low, res[0][0], res[1][0]), jnp.where(low, res[0][1], res[1][1])


def _attn_prompt_kernel(q_ref, kv_ref, o_ref, lse_ref):
    g = pl.program_id(0)
    L = q_ref.shape[2]
    nc = ATT_LANE_CHUNKS

    for gi, (win, dil) in enumerate(ATT_GROUPS):
        @pl.when(g == gi)
        def _(gi=gi, win=win, dil=dil):
            nblk = L // win

            def rows(start):
                return pl.ds(start, ATT_BAND, stride=dil) if dil > 1 else pl.ds(start, ATT_BAND)

            def unit(cur, prev, bias):
                for c in range(nc):
                    hs = slice(c * ATT_HEADS_PER_CHUNK, (c + 1) * ATT_HEADS_PER_CHUNK)
                    keys = jnp.concatenate([kv_ref[0, c, prev, :], kv_ref[0, c, cur, :]], axis=0)
                    vals = jnp.concatenate([kv_ref[0, nc + c, prev, :], kv_ref[0, nc + c, cur, :]], axis=0)
                    o, lse = _attn_block(q_ref[0, c, cur, :], keys, vals, bias[hs])
                    o_ref[0, c, cur, :] = o
                    lse_ref[0, c, cur, :] = lse

            bias0 = _attn_bias(gi, True)

            def body0(r, carry):
                unit(rows(r), rows(r), bias0)
                return carry

            lax.fori_loop(0, dil, body0, 0, unroll=min(dil, ATT_UNROLL[gi]))
            if nblk > 1:
                bias1 = _attn_bias(gi, False)

                def body(u, carry):
                    start = (u // dil + 1) * win + u % dil
                    unit(rows(start), rows(start - win), bias1)
                    return carry

                lax.fori_loop(0, (nblk - 1) * dil, body, 0, unroll=ATT_UNROLL[gi])


def attn_prompt(q5, kv5, bsz):
    ng, nc, _, lanes = q5.shape
    L = SEQ
    assert all(L % w == 0 for w, _ in ATT_GROUPS)
    blk = lambda n: pl.BlockSpec((1, n, L, lanes), lambda g, b: (g, 0, b, 0))
    out = jax.ShapeDtypeStruct((ng, nc, bsz * L, lanes), F32)
    return pl.pallas_call(
        _attn_prompt_kernel,
        out_shape=(out, out),
        grid=(ng, bsz),
        in_specs=[blk(nc), blk(2 * nc)],
        out_specs=(blk(nc), blk(nc)),
        compiler_params=pltpu.CompilerParams(
            dimension_semantics=("arbitrary", "arbitrary"), vmem_limit_bytes=VMEM_LIMIT_BYTES),
        name="attn_prompt",
    )(q5, kv5)


SMALL_LANES = 128
_OFF = np.cumsum((0,) + IN_SPLITS).tolist()
PROJ_AB_WIDTHS = (GDN_CONV_DIM, GDN_VW, SSD_INNER, SSD_CONV_DIM, SMALL_LANES)
PROJ_AB_N = sum(PROJ_AB_WIDTHS)
PROJ_C_N = 3 * ATT_HEADS * ATT_HEAD_DIM
ATT_Q_N = ATT_HEADS * ATT_HEAD_DIM


def prep_weights(w_in):
    o = _OFF
    wt = jnp.transpose(w_in, (0, 2, 1))
    pad = jnp.zeros((w_in.shape[0], SMALL_LANES - 2 * GDN_HEADS - SSD_HEADS, w_in.shape[1]), w_in.dtype)
    w_ab = jnp.concatenate([wt[:, o[0]:o[1]], wt[:, o[1]:o[2]], wt[:, o[4]:o[5]], wt[:, o[5]:o[6]],
                            wt[:, o[2]:o[4]], wt[:, o[6]:o[7]], pad], axis=1)
    back = lambda a: jnp.transpose(a.astype(BF16), (0, 2, 1))
    return back(w_ab), back(wt[:, o[7]:o[8]]), back(wt[:, o[8]:o[9]])


def _rmsnorm_rows(h, w_row):
    return h * lax.rsqrt(jnp.mean(h * h, axis=-1, keepdims=True) + RMS_EPS) * w_row


def _proj_ab_kernel(h_ref, nw_ref, w_ref, qkv_ref, gate_ref, z_ref, xbc_ref, small_ref):
    u = _rmsnorm_rows(h_ref[...], nw_ref[0]).astype(BF16)
    y = jnp.dot(u, w_ref[0], preferred_element_type=F32)
    lo = 0
    for ref, wd in zip((qkv_ref, gate_ref, z_ref, xbc_ref, small_ref), PROJ_AB_WIDTHS):
        ref[...] = y[:, lo:lo + wd]
        lo += wd


def proj_ab(h, norm_w, w_ab, layer, tm):
    t = h.shape[0]
    assert t % tm == 0
    return pl.pallas_call(
        _proj_ab_kernel,
        out_shape=tuple(jax.ShapeDtypeStruct((t, wd), F32) for wd in PROJ_AB_WIDTHS),
        grid=(t // tm,),
        in_specs=[pl.BlockSpec((tm, D_MODEL), lambda i: (i, 0)), _layer_spec(norm_w, layer), _layer_spec(w_ab, layer)],
        out_specs=tuple(pl.BlockSpec((tm, wd), lambda i: (i, 0)) for wd in PROJ_AB_WIDTHS),
        compiler_params=pltpu.CompilerParams(
            dimension_semantics=("arbitrary",), vmem_limit_bytes=VMEM_LIMIT_BYTES),
        name="proj_ab",
    )(h, norm_w, w_ab)


def _head_rmsnorm(x, w_row, seg):
    outs = []
    for c in range(x.shape[-1] // 128):
        xc = x[:, c * 128:(c + 1) * 128]
        hi, lo = _split2(xc * xc)
        ss = jnp.dot(hi, seg, preferred_element_type=F32) + jnp.dot(lo, seg, preferred_element_type=F32)
        outs.append(xc * lax.rsqrt(ss * (1.0 / ATT_HEAD_DIM) + RMS_EPS) * w_row)
    return outs


def _proj_c_kernel(h_ref, nw_ref, w_ref, qw_ref, kw_ref, *refs, chunked):
    out_refs = refs[N_ATT_GROUPS:] if chunked else refs
    u = _rmsnorm_rows(h_ref[...], nw_ref[0]).astype(BF16)
    y = jnp.dot(u, w_ref[0], preferred_element_type=F32)
    r = lax.broadcasted_iota(jnp.int32, (128, 128), 0) // ATT_HEAD_DIM
    c = lax.broadcasted_iota(jnp.int32, (128, 128), 1) // ATT_HEAD_DIM
    seg = (r == c).astype(BF16)
    qn = [t * ATT_HEAD_DIM ** -0.5 for t in _head_rmsnorm(y[:, :ATT_Q_N], qw_ref[...], seg)]
    kn = _head_rmsnorm(y[:, ATT_Q_N:2 * ATT_Q_N], kw_ref[...], seg)
    v = y[:, 2 * ATT_Q_N:]
    nc = ATT_LANE_CHUNKS
    if chunked:
        q5_ref, kv5_ref, *kvt_refs = out_refs
    else:
        q_ref, *kvrows_refs = out_refs
    for g in range(N_ATT_GROUPS):
        for cc in range(nc):
            i = g * nc + cc
            vc = v[:, i * 128:(i + 1) * 128]
            if chunked:
                q5_ref[g, cc] = qn[i]
                kv5_ref[g, cc] = kn[i]
                kv5_ref[g, nc + cc] = vc
                wcols = kvt_refs[g].shape[-1]
                kvt_refs[g][0, 0, cc * 128:(cc + 1) * 128, :] = kn[i].T[:, kn[i].shape[0] - wcols:]
                kvt_refs[g][0, 0, ATT_OUT + cc * 128:ATT_OUT + (cc + 1) * 128, :] = vc.T[:, vc.shape[0] - wcols:]
            else:
                q_ref[:, i * 128:(i + 1) * 128] = qn[i]
                kvrows_refs[g][:, cc * 128:(cc + 1) * 128] = kn[i]
                kvrows_refs[g][:, ATT_OUT + cc * 128:ATT_OUT + (cc + 1) * 128] = vc


def _proj_c_job(h, norm_w, w_c, layer, q_norm, k_norm, tm, chunked, kv_carried=None, seq_len=None):
    t = h.shape[0]
    assert t % tm == 0
    const = dict(pipeline_mode=pl.Buffered(1))
    ng, nc = N_ATT_GROUPS, ATT_LANE_CHUNKS
    if chunked:
        out_shape = [jax.ShapeDtypeStruct((ng, nc, t, 128), F32), jax.ShapeDtypeStruct((ng, 2 * nc, t, 128), F32)]
        out_specs = [pl.BlockSpec((ng, nc, tm, 128), lambda i: (0, 0, i, 0)),
                     pl.BlockSpec((ng, 2 * nc, tm, 128), lambda i: (0, 0, i, 0))]
    else:
        out_shape = [jax.ShapeDtypeStruct((t, ATT_Q_N), F32)]
        out_specs = [pl.BlockSpec((tm, ATT_Q_N), lambda i: (i, 0))]
        out_shape += [jax.ShapeDtypeStruct((t, 2 * ATT_OUT), F32)] * ng
        out_specs += [pl.BlockSpec((tm, 2 * ATT_OUT), lambda i: (i, 0))] * ng
    tile2 = lambda w: jnp.tile(w, 128 // ATT_HEAD_DIM).reshape(1, 128)
    args = [h, norm_w, w_c, tile2(q_norm), tile2(k_norm)]
    in_specs = [pl.BlockSpec((tm, D_MODEL), lambda i: (i, 0)),
                _layer_spec(norm_w, layer),
                _layer_spec(w_c, layer),
                pl.BlockSpec((1, 128), lambda i: (0, 0), **const),
                pl.BlockSpec((1, 128), lambda i: (0, 0), **const)]
    aliases = {}
    if chunked:
        assert seq_len % tm == 0
        tiles = seq_len // tm
        for gi, (win, _) in enumerate(ATT_GROUPS):
            w = min(win, seq_len)
            wblk = min(w, tm)
            first = (seq_len - w) // tm
            shape = (DEPTH, t // seq_len, 2 * ATT_OUT, w)
            out_shape.append(jax.ShapeDtypeStruct(shape, F32))
            out_specs.append(pl.BlockSpec(
                (1, 1, 2 * ATT_OUT, wblk),
                lambda i, first=first: (layer, i // tiles, 0, jnp.maximum(i % tiles - first, 0))))
            carried, al = _carried(None if kv_carried is None else kv_carried[gi], shape, len(args))
            aliases.update({k: len(out_shape) - 1 for k in al})
            args.append(carried)
            in_specs.append(pl.BlockSpec(memory_space=pl.ANY))
    return dict(
        kernel=functools.partial(_proj_c_kernel, chunked=chunked),
        steps=t // tm,
        args=args,
        in_specs=in_specs,
        out_shape=out_shape,
        out_specs=out_specs,
        scratch=[],
        aliases=aliases)


ROW_SUBTILE = 256


def _merge_kernel(h_ref, oa_ref, ob_ref, *refs, chunked):
    att_refs, refs = (refs[:2], refs[2:]) if chunked else (refs[:1], refs[1:])
    nw_ref, wg_ref, gb_ref, wa_ref, wb_ref, wc_ref, wo_ref, out_ref = refs
    tm = h_ref.shape[0]
    sub = min(tm, ROW_SUBTILE)
    for r0 in range(0, tm, sub):
        rs = slice(r0, r0 + sub)
        if chunked:
            o5_ref, lse5_ref = att_refs
            ocs = []
            for cc in range(ATT_LANE_CHUNKS):
                ls = [lse5_ref[g, cc, rs, :] for g in range(N_ATT_GROUPS)]
                m = functools.reduce(jnp.maximum, ls)
                es = [jnp.exp(l - m) for l in ls]
                num = sum(e * o5_ref[g, cc, rs, :] for g, e in enumerate(es))
                ocs.append(num * (1.0 / sum(es)))
            o_c = jnp.concatenate(ocs, axis=-1)
        else:
            o_c = att_refs[0][rs, :]
        h = h_ref[rs, :]
        u = _rmsnorm_rows(h, nw_ref[0]).astype(BF16)
        merged = None
        for i, (o, w_ref) in enumerate(((oa_ref[rs, :], wa_ref), (ob_ref[rs, :], wb_ref), (o_c, wc_ref))):
            sl = slice(i * D_MODEL, (i + 1) * D_MODEL)
            gate = jax.nn.sigmoid(jnp.dot(u, wg_ref[0, :, sl], preferred_element_type=F32) + gb_ref[0, :, sl])
            term = gate * jnp.dot(o.astype(BF16), w_ref[0], preferred_element_type=F32)
            merged = term if merged is None else merged + term
        out_ref[rs, :] = h + jnp.dot(merged.astype(BF16), wo_ref[0], preferred_element_type=F32)


def merge_block(h, o_a, o_b, o_att, params, layer, tm):
    t = h.shape[0]
    assert t % tm == 0
    chunked = isinstance(o_att, tuple)
    rows = lambda wd: pl.BlockSpec((tm, wd), lambda i: (i, 0))
    if chunked:
        att_specs = [pl.BlockSpec((N_ATT_GROUPS, ATT_LANE_CHUNKS, tm, 128), lambda i: (0, 0, i, 0))] * 2
        att_args = list(o_att)
    else:
        att_specs = [rows(ATT_OUT)]
        att_args = [o_att]
    return pl.pallas_call(
        functools.partial(_merge_kernel, chunked=chunked),
        out_shape=jax.ShapeDtypeStruct(h.shape, F32),
        grid=(t // tm,),
        in_specs=[rows(D_MODEL), rows(GDN_VW), rows(SSD_INNER)] + att_specs + [_layer_spec(a, layer) for a in params],
        out_specs=rows(D_MODEL),
        compiler_params=pltpu.CompilerParams(
            dimension_semantics=("arbitrary",), vmem_limit_bytes=VMEM_LIMIT_BYTES),
        name="merge",
    )(h, o_a, o_b, *att_args, *params)


DEC_TB = 8


def _conv_step(cst_ref, x, w_ref):
    prev = [cst_ref[0, i] for i in range(CONV_WIDTH - 1)]
    y = x * w_ref[CONV_WIDTH - 1:CONV_WIDTH, :]
    for i, pr in enumerate(prev):
        y = y + pr * w_ref[i:i + 1, :]
    return y, prev[1:] + [x]


def _gdn_decode_kernel(qkv_ref, gate_ref, small_ref, cst_ref, s_ref, carried_ref, convw_ref, alog_ref, dtb_ref,
                       normw_ref, o_ref, so_ref, csto_ref):
    del carried_ref
    tb = qkv_ref.shape[0]
    x = qkv_ref[...]
    y, new_cst = _conv_step(cst_ref, x, convw_ref)
    for i, r in enumerate(new_cst):
        csto_ref[i] = r
    y = _silu(y)
    small = small_ref[...]
    beta_all = jax.nn.sigmoid(small)
    eg_all = jnp.exp(-jnp.exp(alog_ref[...]) * _softplus(small + dtb_ref[...]))
    lanes = small.shape[-1]
    r = lax.broadcasted_iota(jnp.int32, (lanes, GDN_HEADS * GDN_DV), 0)
    c = lax.broadcasted_iota(jnp.int32, (lanes, GDN_HEADS * GDN_DV), 1)
    sel = (r == GDN_HEADS + c // GDN_DV).astype(BF16)
    eg_wide = sum(jnp.dot(t, sel, preferred_element_type=F32) for t in _split3(eg_all))
    tok = lax.broadcasted_iota(jnp.int32, (tb, 1), 0)
    tok2 = lax.broadcasted_iota(jnp.int32, (2 * tb, 1), 0)
    for h in range(GDN_HEADS):
        lo = h * GDN_DK
        q = y[:, lo:lo + GDN_DK]
        k = y[:, GDN_QK + lo:GDN_QK + lo + GDN_DK]
        v = y[:, 2 * GDN_QK + h * GDN_DV:2 * GDN_QK + (h + 1) * GDN_DV]
        q = q * (lax.rsqrt(jnp.sum(q * q, axis=-1, keepdims=True) + RMS_EPS) * GDN_DK ** -0.5)
        k = k * lax.rsqrt(jnp.sum(k * k, axis=-1, keepdims=True) + RMS_EPS)
        beta = beta_all[:, h:h + 1]
        eg = eg_all[:, GDN_HEADS + h:GDN_HEADS + h + 1]
        qk = jnp.sum(q * k, axis=-1, keepdims=True)
        kq = jnp.concatenate([k, q], axis=0).astype(BF16)
        kq_s = jnp.zeros((2 * tb, GDN_DV), F32)
        for t in range(tb):
            r = jnp.dot(kq, s_ref[0, t, h].astype(BF16), preferred_element_type=F32)
            kq_s = jnp.where(tok2 % tb == t, r, kq_s)
        u = beta * (v - eg * kq_s[:tb])
        o = eg * kq_s[tb:] + qk * u
        kb = k.astype(BF16)
        for t in range(tb):
            outer = lax.dot_general(kb, jnp.where(tok == t, u, 0.0).astype(BF16), (((0,), (0,)), ((), ())),
                                    preferred_element_type=F32)
            so_ref[0, t, h] = s_ref[0, t, h] * eg_wide[t:t + 1, h * GDN_DV:(h + 1) * GDN_DV] + outer
        o = o * lax.rsqrt(jnp.mean(o * o, axis=-1, keepdims=True) + RMS_EPS) * normw_ref[...]
        o_ref[:, h * GDN_DV:(h + 1) * GDN_DV] = o * _silu(gate_ref[:, h * GDN_DV:(h + 1) * GDN_DV])


def _carried(new_states, shape, operand_index):
    if new_states is None:
        return jnp.zeros((8, 128), F32), {}
    assert new_states.shape == shape
    return new_states, {operand_index: 1}


def gdn_decode(a_qkv, a_gate, small, conv_state_t, state, new_states, layer, conv_w, a_log, dt_bias, norm_w):
    bsz = a_qkv.shape[0]
    prev, aliases = _carried(new_states, state.shape, 5)
    tb = DEC_TB
    lanes = small.shape[-1]
    alog_row = jnp.zeros((1, lanes), F32).at[0, GDN_HEADS:2 * GDN_HEADS].set(a_log)
    dtb_row = jnp.zeros((1, lanes), F32).at[0, GDN_HEADS:2 * GDN_HEADS].set(dt_bias)
    nprev = CONV_WIDTH - 1
    rows = lambda wd: pl.BlockSpec((tb, wd), lambda i: (i, 0))
    const2 = lambda i: (0, 0)
    return pl.pallas_call(
        _gdn_decode_kernel,
        out_shape=(jax.ShapeDtypeStruct((bsz, GDN_VW), F32),
                   jax.ShapeDtypeStruct(state.shape, F32),
                   jax.ShapeDtypeStruct((nprev, bsz, GDN_CONV_DIM), F32)),
        grid=(bsz // tb,),
        in_specs=[rows(GDN_CONV_DIM), rows(GDN_VW), rows(lanes),
                  pl.BlockSpec((1, nprev, tb, GDN_CONV_DIM), lambda i: (layer, 0, i, 0)),
                  pl.BlockSpec((1, tb, GDN_HEADS, GDN_DK, GDN_DV), lambda i: (layer, i, 0, 0, 0)),
                  pl.BlockSpec(memory_space=pl.ANY),
                  pl.BlockSpec((CONV_WIDTH, GDN_CONV_DIM), const2),
                  pl.BlockSpec((1, lanes), const2), pl.BlockSpec((1, lanes), const2),
                  pl.BlockSpec((1, GDN_DV), const2)],
        out_specs=(rows(GDN_VW),
                   pl.BlockSpec((1, tb, GDN_HEADS, GDN_DK, GDN_DV), lambda i: (layer, i, 0, 0, 0)),
                   pl.BlockSpec((nprev, tb, GDN_CONV_DIM), lambda i: (0, i, 0))),
        input_output_aliases=aliases,
        compiler_params=pltpu.CompilerParams(
            dimension_semantics=("arbitrary",), vmem_limit_bytes=VMEM_LIMIT_BYTES),
        name="gdn_decode",
    )(a_qkv, a_gate, small, conv_state_t, state, prev, conv_w, alog_row, dtb_row, norm_w.reshape(1, GDN_DV))


def _ssd_decode_kernel(xbc_ref, z_ref, small_ref, cst_ref, h_ref, carried_ref, convw_ref, convb_ref, alog_ref,
                       dtb_ref, d_ref, normw_ref, o_ref, ho_ref, csto_ref):
    del carried_ref
    tb = xbc_ref.shape[0]
    x = xbc_ref[...]
    y, new_cst = _conv_step(cst_ref, x, convw_ref)
    for i, r in enumerate(new_cst):
        csto_ref[i] = r
    y = _silu(y + convb_ref[...])
    xs = y[:, :SSD_INNER]
    bm = y[:, SSD_INNER:SSD_INNER + SSD_BC]
    cm = y[:, SSD_INNER + SSD_BC:]
    dt_all = _softplus(small_ref[...] + dtb_ref[...])
    eda_all = jnp.exp(dt_all * (-jnp.exp(alog_ref[...])))
    lanes = dt_all.shape[-1]

    def spread(v, width):
        r = lax.broadcasted_iota(jnp.int32, (lanes, SSD_HEADS * width), 0)
        c = lax.broadcasted_iota(jnp.int32, (lanes, SSD_HEADS * width), 1)
        sel = (r == SSD_DT_LANE + c // width).astype(BF16)
        return sum(jnp.dot(t, sel, preferred_element_type=F32) for t in _split3(v))

    eda_wide = spread(eda_all, SSD_STATE)
    xd = xs * spread(dt_all, SSD_HEADDIM)
    tok = lax.broadcasted_iota(jnp.int32, (tb, 1), 0)
    gp = SSD_HPG * SSD_HEADDIM
    ys = []
    for g in range(SSD_GROUPS):
        bg = bm[:, g * SSD_STATE:(g + 1) * SSD_STATE]
        cg = cm[:, g * SSD_STATE:(g + 1) * SSD_STATE].astype(BF16)
        xd_g = xd[:, g * gp:(g + 1) * gp].astype(BF16)
        y_g = jnp.zeros((tb, gp), F32)
        for t in range(tb):
            outer = lax.dot_general(xd_g, jnp.where(tok == t, bg, 0.0).astype(BF16), (((0,), (0,)), ((), ())),
                                    preferred_element_type=F32)
            hns = []
            for hh in range(SSD_HPG):
                h = g * SSD_HPG + hh
                hn = (h_ref[0, t, h] * eda_wide[t:t + 1, h * SSD_STATE:(h + 1) * SSD_STATE]
                      + outer[hh * SSD_HEADDIM:(hh + 1) * SSD_HEADDIM])
                ho_ref[0, t, h] = hn
                hns.append(hn.astype(BF16))
            y_t = lax.dot_general(cg, jnp.concatenate(hns, axis=0), (((1,), (1,)), ((), ())),
                                  preferred_element_type=F32)
            y_g = jnp.where(tok == t, y_t, y_g)
        ys.append(y_g)
    yf = jnp.concatenate(ys, axis=-1) + d_ref[...] * xs
    yf = yf * _silu(z_ref[...])
    gw = SSD_INNER // SSD_GROUPS
    for g in range(SSD_GROUPS):
        seg = yf[:, g * gw:(g + 1) * gw]
        o_ref[:, g * gw:(g + 1) * gw] = (seg * lax.rsqrt(jnp.mean(seg * seg, axis=-1, keepdims=True) + RMS_EPS)
                                         * normw_ref[:, g * gw:(g + 1) * gw])


def ssd_decode(b_xbc, b_z, small, conv_state_t, state, new_states, layer, conv_w, conv_b, a_log, dt_bias, d_skip,
               norm_w):
    bsz = b_xbc.shape[0]
    prev, aliases = _carried(new_states, state.shape, 5)
    tb = DEC_TB
    lanes = small.shape[-1]
    sl = slice(SSD_DT_LANE, SSD_DT_LANE + SSD_HEADS)
    alog_row = jnp.zeros((1, lanes), F32).at[0, sl].set(a_log)
    dtb_row = jnp.zeros((1, lanes), F32).at[0, sl].set(dt_bias)
    d_row = jnp.repeat(d_skip, SSD_HEADDIM).reshape(1, SSD_INNER)
    nprev = CONV_WIDTH - 1
    rows = lambda wd: pl.BlockSpec((tb, wd), lambda i: (i, 0))
    const2 = lambda i: (0, 0)
    return pl.pallas_call(
        _ssd_decode_kernel,
        out_shape=(jax.ShapeDtypeStruct((bsz, SSD_INNER), F32),
                   jax.ShapeDtypeStruct(state.shape, F32),
                   jax.ShapeDtypeStruct((nprev, bsz, SSD_CONV_DIM), F32)),
        grid=(bsz // tb,),
        in_specs=[rows(SSD_CONV_DIM), rows(SSD_INNER), rows(lanes),
                  pl.BlockSpec((1, nprev, tb, SSD_CONV_DIM), lambda i: (layer, 0, i, 0)),
                  pl.BlockSpec((1, tb, SSD_HEADS, SSD_HEADDIM, SSD_STATE), lambda i: (layer, i, 0, 0, 0)),
                  pl.BlockSpec(memory_space=pl.ANY),
                  pl.BlockSpec((CONV_WIDTH, SSD_CONV_DIM), const2), pl.BlockSpec((1, SSD_CONV_DIM), const2),
                  pl.BlockSpec((1, lanes), const2), pl.BlockSpec((1, lanes), const2),
                  pl.BlockSpec((1, SSD_INNER), const2), pl.BlockSpec((1, SSD_INNER), const2)],
        out_specs=(rows(SSD_INNER),
                   pl.BlockSpec((1, tb, SSD_HEADS, SSD_HEADDIM, SSD_STATE), lambda i: (layer, i, 0, 0, 0)),
                   pl.BlockSpec((nprev, tb, SSD_CONV_DIM), lambda i: (0, i, 0))),
        input_output_aliases=aliases,
        compiler_params=pltpu.CompilerParams(
            dimension_semantics=("arbitrary",), vmem_limit_bytes=VMEM_LIMIT_BYTES),
        name="ssd_decode",
    )(b_xbc, b_z, small, conv_state_t, state, prev, conv_w, conv_b.reshape(1, SSD_CONV_DIM),
      alog_row, dtb_row, d_row, norm_w.reshape(1, SSD_INNER))


ATT_DEC_TB = 8
ATT_DEC_TOK = 2


def _attn_decode_tokens(q_ref, kvn_refs, c_refs, k, acc, lane0):
    tb = q_ref.shape[0]
    ntok = c_refs[0].shape[1]
    hd = ATT_HEAD_DIM
    slopes = _alibi_slopes_np()
    q_t = q_ref[...].T
    kvn_t = [ref[...].T for ref in kvn_refs]
    lane_t = lax.broadcasted_iota(jnp.int32, (1, tb), 1)

    def column(x_t, tok):
        return jnp.sum(jnp.where(lane_t == tok, x_t, 0.0), axis=1, keepdims=True)

    biases = []
    for gi, (win, dil) in enumerate(ATT_GROUPS):
        dist = win - lax.broadcasted_iota(jnp.int32, (ATT_HEADS_PER_GROUP, win), 1)
        head = lax.broadcasted_iota(jnp.int32, (ATT_HEADS_PER_GROUP, win), 0)
        slope = functools.reduce(lambda acc_, hh: jnp.where(head == hh, float(slopes[gi, hh]), acc_),
                                 range(ATT_HEADS_PER_GROUP), jnp.zeros((ATT_HEADS_PER_GROUP, win), F32))
        biases.append(jnp.where(dist % dil == 0, -slope * dist.astype(F32), ATT_MASKED))
    lane = lax.broadcasted_iota(jnp.int32, acc.shape, 1)
    for tt in range(ntok):
        tok = k * ntok + tt
        q_col = column(q_t, tok)
        og, lg = [], []
        for gi, c_ref in enumerate(c_refs):
            kvn_col = column(kvn_t[gi], tok)
            heads = range(ATT_HEADS_PER_GROUP)
            qcs = [q_col[gi * ATT_OUT + hh * hd:gi * ATT_OUT + (hh + 1) * hd] for hh in heads]
            kns = [kvn_col[hh * hd:(hh + 1) * hd] for hh in heads]
            vns = [kvn_col[ATT_OUT + hh * hd:ATT_OUT + (hh + 1) * hd] for hh in heads]
            s = jnp.concatenate([jnp.sum(c_ref[0, tt, 0, hh] * qcs[hh], axis=0, keepdims=True) for hh in heads],
                                axis=0) + biases[gi]
            s_new = jnp.concatenate([jnp.sum(qcs[hh] * kns[hh], axis=0, keepdims=True) for hh in heads], axis=0)
            m = jnp.maximum(jnp.max(s, axis=-1, keepdims=True), s_new)
            p = jnp.exp(s - m)
            p_new = jnp.exp(s_new - m)
            inv_l = 1.0 / (jnp.sum(p, axis=-1, keepdims=True) + p_new)
            og.append([(jnp.sum(c_ref[0, tt, 1, hh] * p[hh:hh + 1], axis=-1, keepdims=True)
                        + p_new[hh:hh + 1] * vns[hh]) * inv_l[hh:hh + 1] for hh in heads])
            lg.append(m - jnp.log(inv_l))
        m = functools.reduce(jnp.maximum, lg)
        es = [jnp.exp(l - m) for l in lg]
        inv = 1.0 / sum(es)
        ws = [e * inv for e in es]
        o_col = jnp.concatenate([sum(ws[gi][hh:hh + 1] * og[gi][hh] for gi in range(N_ATT_GROUPS))
                                 for hh in range(ATT_HEADS_PER_GROUP)], axis=0)
        acc = jnp.where(lane == lane0 + tt, o_col, acc)
    return acc


def _ffn_attn_decode_kernel(h_ref, nw_ref, wi_ref, wo_ref, q_ref, kvn0_ref, kvn1_ref, kvn2_ref,
                            c0_ref, c1_ref, c2_ref, o_ref, oct_ref):
    step = pl.program_id(0)
    ntok = c0_ref.shape[1]
    steps_per_block = q_ref.shape[0] // ntok

    @pl.when(step == 0)
    def _():
        oct_ref[...] = jnp.zeros_like(oct_ref)

    oct_ref[...] = _attn_decode_tokens(q_ref, (kvn0_ref, kvn1_ref, kvn2_ref), (c0_ref, c1_ref, c2_ref),
                                       step % steps_per_block, oct_ref[...], step * ntok)
    _ffn_kernel(h_ref, nw_ref, wi_ref, wo_ref, o_ref)


def ffn_attn_decode(h, norm_w, w_in_bf16, w_out_bf16, ffn_layer, q_rows, kv_new, caches, layer):
    t = h.shape[0]
    bsz = q_rows.shape[0]
    tb, ntok = ATT_DEC_TB, ATT_DEC_TOK
    steps = bsz // ntok
    assert t % steps == 0 and bsz % tb == 0 and tb % ntok == 0 and bsz % 128 == 0
    tm = t // steps
    views, specs = [], []
    for cache, (win, dil) in zip(caches, ATT_GROUPS):
        assert cache.shape[2] == win and win % dil == 0
        views.append(jnp.transpose(cache, (0, 1, 3, 4, 5, 2)))
        specs.append(pl.BlockSpec((1, ntok) + KV_TAIL + (win,), lambda s: (layer, s, 0, 0, 0, 0)))
    blk = lambda s: (s // (tb // ntok), 0)
    y, oc_t = pl.pallas_call(
        _ffn_attn_decode_kernel,
        out_shape=(jax.ShapeDtypeStruct(h.shape, h.dtype), jax.ShapeDtypeStruct((ATT_OUT, bsz), F32)),
        grid=(steps,),
        in_specs=([pl.BlockSpec((tm, D_MODEL), lambda s: (s, 0)),
                   _layer_spec(norm_w, ffn_layer), _layer_spec(w_in_bf16, ffn_layer),
                   _layer_spec(w_out_bf16, ffn_layer),
                   pl.BlockSpec((tb, ATT_Q_N), blk)]
                  + [pl.BlockSpec((tb, 2 * ATT_OUT), blk)] * N_ATT_GROUPS + specs),
        out_specs=(pl.BlockSpec((tm, D_MODEL), lambda s: (s, 0)),
                   pl.BlockSpec((ATT_OUT, bsz), lambda s: (0, 0))),
        compiler_params=pltpu.CompilerParams(
            dimension_semantics=("arbitrary",), vmem_limit_bytes=VMEM_LIMIT_BYTES),
        name="ffn_attn_decode",
    )(h, norm_w, w_in_bf16, w_out_bf16, q_rows, *kv_new, *views)
    return y, oc_t.T


PROMPT_TM = 512
PROMPT_FFN_TM = 512
KV_TAIL = (2, ATT_HEADS_PER_GROUP, ATT_HEAD_DIM)


def trunk(x_prompt, x_sample, p, state_gdn, state_gdn_conv, state_ssd, state_ssd_conv, caches):
    bsz, L, _ = x_prompt.shape
    t = bsz * L
    h = x_prompt.reshape(t, D_MODEL)
    new = [[] for _ in range(4)]
    kv_p = None
    seq = lambda a: a.reshape(bsz, L, a.shape[-1])
    bs = x_sample.shape[0]
    assert x_sample.shape[1] == 1
    hs = x_sample.reshape(bs, D_MODEL)
    new_s = [[] for _ in range(2 + N_ATT_GROUPS)]
    gdn_conv_t = jnp.transpose(state_gdn_conv, (0, 2, 1, 3))
    ssd_conv_t = jnp.transpose(state_ssd_conv, (0, 2, 1, 3))
    s_gdn_s = s_ssd_s = None
    for l in range(DEPTH):
        hs = ffn_block(hs, p['ffn1_norm'], p['ffn1_w_in'], p['ffn1_w_out'], l, bs)
        qkv, gate, z, xbc, small = proj_ab(hs, p['mix_norm'], p['w_ab'], l, bs)
        (q_rows, *kvrows_s), = run_jobs("proj_c", _proj_c_job(hs, p['mix_norm'], p['w_c'], l, p['q_norm'][l],
                                                              p['k_norm'][l], bs, False))
        o_a_s, s_gdn_s, c_gdn_s = gdn_decode(qkv, gate, small, gdn_conv_t, state_gdn, s_gdn_s, l, p['gdn_conv_w'][l],
                                             p['gdn_a_log'][l], p['gdn_dt_bias'][l], p['gdn_norm_w'][l])
        o_b_s, s_ssd_s, c_ssd_s = ssd_decode(xbc, z, small, ssd_conv_t, state_ssd, s_ssd_s, l, p['ssd_conv_w'][l],
                                             p['ssd_conv_b'][l], p['ssd_a_log'][l], p['ssd_dt_bias'][l],
                                             p['ssd_d'][l], p['ssd_norm_w'][l])
        h, o_c_s = ffn_attn_decode(h, p['ffn1_norm'], p['ffn1_w_in'], p['ffn1_w_out'], l, q_rows, kvrows_s, caches, l)
        hs = merge_block(hs, o_a_s, o_b_s, o_c_s, p['merge'], l, bs)
        hs = ffn_block(hs, p['ffn2_norm'], p['ffn2_w_in'], p['ffn2_w_out'], l, bs)
        for lst, s in zip(new_s, (c_gdn_s, c_ssd_s, *[r.reshape((bs, 1) + KV_TAIL) for r in kvrows_s])):
            lst.append(s)

        qkv, gate, z, xbc, small = proj_ab(h, p['mix_norm'], p['w_ab'], l, PROMPT_TM)
        (q5, kv5, *kv_p), = run_jobs(
            "proj_c", _proj_c_job(h, p['mix_norm'], p['w_c'], l, p['q_norm'][l], p['k_norm'][l], PROMPT_TM, True,
                                  kv_p, L))
        (o_a, s_gdn, c_gdn), = run_jobs(
            "gdn_prompt", _gdn_prompt_job(seq(qkv), seq(gate), seq(small), p['gdn_conv_w'][l], p['gdn_a_log'][l],
                                          p['gdn_dt_bias'][l], p['gdn_norm_w'][l]))
        o_b, s_ssd, c_ssd = ssd_prompt(seq(xbc), seq(z), seq(small), p['ssd_conv_w'][l], p['ssd_conv_b'][l],
                                       p['ssd_a_log'][l], p['ssd_dt_bias'][l], p['ssd_d'][l], p['ssd_norm_w'][l])
        o5, lse5 = attn_prompt(q5, kv5, bsz)
        h = merge_block(h, o_a.reshape(t, GDN_VW), o_b.reshape(t, SSD_INNER), (o5, lse5), p['merge'], l, PROMPT_TM)
        h = ffn_block(h, p['ffn2_norm'], p['ffn2_w_in'], p['ffn2_w_out'], l, PROMPT_FFN_TM)
        for lst, s in zip(new, (s_gdn, c_gdn, s_ssd, c_ssd)):
            lst.append(s)
    c_gdn_s, c_ssd_s, *kv_new_s = [jnp.stack(lst, axis=0) for lst in new_s]
    st_s = [s_gdn_s, jnp.transpose(c_gdn_s, (0, 2, 1, 3)), s_ssd_s, jnp.transpose(c_ssd_s, (0, 2, 1, 3)), *kv_new_s]
    kv_out = [jnp.transpose(b.reshape((DEPTH, bsz) + KV_TAIL + (b.shape[-1],)), (0, 1, 5, 2, 3, 4)) for b in kv_p]
    return (h.reshape(bsz, L, D_MODEL), [jnp.stack(lst, axis=0) for lst in new] + kv_out,
            hs.reshape(bs, 1, D_MODEL), st_s)


def kernel(x_prompt, x_sample, state_gdn, state_gdn_conv, state_ssd, state_ssd_conv,
           cache_kv_w128, cache_kv_w512, cache_kv_w2048,
           ffn1_norm, ffn1_w_in, ffn1_w_out, mix_norm, w_in, gate_b,
           gdn_conv_w, gdn_a_log, gdn_dt_bias, gdn_norm_w,
           ssd_conv_w, ssd_conv_b, ssd_a_log, ssd_dt_bias, ssd_d, ssd_norm_w,
           q_norm, k_norm, w_br_a, w_br_b, w_br_c, w_out,
           ffn2_norm, ffn2_w_in, ffn2_w_out):
    w_ab, w_c, w_g = prep_weights(w_in)
    row = lambda a: a.reshape(DEPTH, 1, a.shape[-1])
    mix_norm_r = row(mix_norm)
    p = {'ffn1_norm': row(ffn1_norm), 'ffn1_w_in': ffn1_w_in.astype(BF16), 'ffn1_w_out': ffn1_w_out.astype(BF16),
         'mix_norm': mix_norm_r, 'w_ab': w_ab, 'w_c': w_c,
         'merge': (mix_norm_r, w_g, row(gate_b), w_br_a.astype(BF16), w_br_b.astype(BF16), w_br_c.astype(BF16),
                   w_out.astype(BF16)),
         'gdn_conv_w': gdn_conv_w, 'gdn_a_log': gdn_a_log, 'gdn_dt_bias': gdn_dt_bias, 'gdn_norm_w': gdn_norm_w,
         'ssd_conv_w': ssd_conv_w, 'ssd_conv_b': ssd_conv_b, 'ssd_a_log': ssd_a_log,
         'ssd_dt_bias': ssd_dt_bias, 'ssd_d': ssd_d, 'ssd_norm_w': ssd_norm_w,
         'q_norm': q_norm, 'k_norm': k_norm,
         'ffn2_norm': row(ffn2_norm), 'ffn2_w_in': ffn2_w_in.astype(BF16), 'ffn2_w_out': ffn2_w_out.astype(BF16)}
    y_prompt, st_p, y_sample, st_s = trunk(x_prompt, x_sample, p, state_gdn, state_gdn_conv, state_ssd,
                                           state_ssd_conv, [cache_kv_w128, cache_kv_w512, cache_kv_w2048])
    gdn_p, gdn_conv_p, ssd_p, ssd_conv_p, kv128_p, kv512_p, kv2048_p = st_p
    gdn_s, gdn_conv_s, ssd_s, ssd_conv_s, kv128_s, kv512_s, kv2048_s = st_s
    return (y_prompt, y_sample, gdn_p, gdn_s, gdn_conv_p, gdn_conv_s, ssd_p, ssd_s, ssd_conv_p, ssd_conv_s,
            kv128_p, kv128_s, kv512_p, kv512_s, kv2048_p, kv2048_s)
```

```python
import functools
import math
import jax, jax.numpy as jnp
from jax import lax
import numpy as np
from jax.experimental import pallas as pl
from jax.experimental.pallas import tpu as pltpu

D_MODEL = 1024
BATCH = 8
SEQ = 2048
DEPTH = 4
DEC_BATCH = 128
DEC_SEQ = 1
PAST_LEN = 2048

CONV_WIDTH = 4
RMS_EPS = 1e-6
GDN_HEADS = 4
GDN_DK = 128
GDN_DV = 128
GDN_CHUNK = 64
GDN_QK = GDN_HEADS * GDN_DK
GDN_VW = GDN_HEADS * GDN_DV
GDN_CONV_DIM = 2 * GDN_QK + GDN_VW
SSD_HEADS = 8
SSD_HEADDIM = 64
SSD_GROUPS = 2
SSD_STATE = 128
SSD_CHUNK = 128
SSD_INNER = SSD_HEADS * SSD_HEADDIM
SSD_CONV_DIM = SSD_INNER + 2 * SSD_GROUPS * SSD_STATE
ATT_GROUPS = ((128, 1), (512, 4), (2048, 16))
N_ATT_GROUPS = len(ATT_GROUPS)
ATT_HEADS_PER_GROUP = 4
ATT_HEAD_DIM = 64
ATT_HEADS = N_ATT_GROUPS * ATT_HEADS_PER_GROUP
ATT_OUT = ATT_HEADS_PER_GROUP * ATT_HEAD_DIM
D_FF = ((8 * D_MODEL // 3 + 127) // 128) * 128
N_BRANCH = 3
IN_SPLITS = (GDN_CONV_DIM, GDN_VW, GDN_HEADS, GDN_HEADS, SSD_INNER, SSD_CONV_DIM, SSD_HEADS,
             3 * ATT_HEADS * ATT_HEAD_DIM, N_BRANCH * D_MODEL)
D_IN = sum(IN_SPLITS)

F32 = jnp.float32
BF16 = jnp.bfloat16
VMEM_LIMIT_BYTES = 56 * 1024 * 1024


def _layer_spec(stacked, layer):
    zeros = (0,) * (stacked.ndim - 1)
    return pl.BlockSpec((1,) + stacked.shape[1:], lambda *_: (layer,) + zeros, pipeline_mode=pl.Buffered(1))


def _ffn_kernel(h_ref, nw_ref, wi_ref, wo_ref, o_ref):
    h = h_ref[...]
    xn = h * lax.rsqrt(jnp.mean(h * h, axis=-1, keepdims=True) + RMS_EPS) * nw_ref[0]
    gu = jnp.dot(xn.astype(BF16), wi_ref[0], preferred_element_type=F32)
    g, up = gu[:, :D_FF], gu[:, D_FF:]
    act = (g * jax.nn.sigmoid(g) * up).astype(BF16)
    y = jnp.dot(act, wo_ref[0], preferred_element_type=F32)
    o_ref[...] = h + 0.5 * y


def ffn_block(h, norm_w, w_in_bf16, w_out_bf16, layer, tm):
    t = h.shape[0]
    assert t % tm == 0
    return pl.pallas_call(
        _ffn_kernel,
        out_shape=jax.ShapeDtypeStruct(h.shape, h.dtype),
        grid=(t // tm,),
        in_specs=[pl.BlockSpec((tm, D_MODEL), lambda i: (i, 0)),
                  _layer_spec(norm_w, layer), _layer_spec(w_in_bf16, layer), _layer_spec(w_out_bf16, layer)],
        out_specs=pl.BlockSpec((tm, D_MODEL), lambda i: (i, 0)),
        compiler_params=pltpu.CompilerParams(
            dimension_semantics=("arbitrary",), vmem_limit_bytes=VMEM_LIMIT_BYTES),
        name="ffn",
    )(h, norm_w, w_in_bf16, w_out_bf16)


_DN_BMM = (((2,), (1,)), ((0,), (0,)))
_DN_BMM_NT = (((2,), (2,)), ((0,), (0,)))


def _bmm(a, b):
    return lax.dot_general(a.astype(BF16), b.astype(BF16), _DN_BMM, preferred_element_type=F32)


def _bmm_nt(a, b):
    return lax.dot_general(a.astype(BF16), b.astype(BF16), _DN_BMM_NT, preferred_element_type=F32)


def _split2(x):
    hi = x.astype(BF16)
    return hi, (x - hi.astype(F32)).astype(BF16)


def _split3(x):
    x1 = x.astype(BF16)
    r1 = x - x1.astype(F32)
    x2 = r1.astype(BF16)
    x3 = (r1 - x2.astype(F32)).astype(BF16)
    return x1, x2, x3


def _bmm_hi(a, b):
    a1, a2 = _split2(a)
    b1, b2 = _split2(b)
    dot = functools.partial(lax.dot_general, dimension_numbers=_DN_BMM, preferred_element_type=F32)
    return dot(a1, b1) + (dot(a1, b2) + dot(a2, b1))


def _cumsum_rows(tri, x):
    dot = functools.partial(lax.dot_general, dimension_numbers=_DN_BMM, preferred_element_type=F32)
    x1, x2, x3 = _split3(x)
    return dot(tri, x1) + (dot(tri, x2) + dot(tri, x3))


def _softplus(x):
    return jnp.maximum(x, 0.0) + jnp.log(1.0 + jnp.exp(-jnp.abs(x)))


def _silu(x):
    return x * jax.nn.sigmoid(x)


GDN_INV_BLOCK = 16


def _unit_lower_solve(a, rhs, row, col, mm):
    c = a.shape[-1]
    eye = (row == col).astype(F32)
    same_blk = (row // GDN_INV_BLOCK) == (col // GDN_INV_BLOCK)
    d = jnp.where(same_blk, a, 0.0)
    n = a - d
    x = eye - d
    pw = d
    k = 2
    while k < GDN_INV_BLOCK:
        pw = mm(pw, pw)
        x = x + mm(x, pw)
        k *= 2
    m = mm(x, n)
    y = mm(x, rhs)
    nblk = c // GDN_INV_BLOCK
    q = eye - m
    pw = m
    k = 2
    while k < nblk:
        pw = mm(pw, pw)
        q = q + mm(q, pw)
        k *= 2
    return mm(q, y)


def _gdn_prompt_kernel(qkv_ref, gate_ref, small_ref, convw_ref, alog_ref, dtb_ref, normw_ref,
                       o_ref, s_ref, cst_ref, xp_ref):
    c = GDN_CHUNK
    nb = qkv_ref.shape[0]

    @pl.when(pl.program_id(0) == 0)
    def _():
        s_ref[...] = jnp.zeros_like(s_ref)
        xp_ref[:, 0:8, :] = jnp.zeros((nb, 8, GDN_CONV_DIM), F32)

    x = qkv_ref[...]
    xp_ref[:, 8:8 + c, :] = x
    w = convw_ref[...]
    y = (xp_ref[:, 5:5 + c, :] * w[0:1, :] + xp_ref[:, 6:6 + c, :] * w[1:2, :]
         + xp_ref[:, 7:7 + c, :] * w[2:3, :] + x * w[3:4, :])
    xp_ref[:, 0:8, :] = xp_ref[:, c:c + 8, :]
    cst_ref[...] = x[:, c - (CONV_WIDTH - 1):, :]
    y = _silu(y)

    small = small_ref[...]
    beta_all = jax.nn.sigmoid(small)
    g_all = -jnp.exp(alog_ref[...]) * _softplus(small + dtb_ref[...])
    row = lax.broadcasted_iota(jnp.int32, (c, c), 0)
    col = lax.broadcasted_iota(jnp.int32, (c, c), 1)
    incl = row >= col
    strict = row > col
    tri = jnp.broadcast_to(incl.astype(BF16), (nb, c, c))
    gam_all = _cumsum_rows(tri, g_all)
    gam_all_t = jnp.swapaxes(gam_all, 1, 2)
    egam_all = jnp.exp(gam_all)

    for h in range(GDN_HEADS):
        lo = h * GDN_DK
        q = y[:, :, lo:lo + GDN_DK]
        k = y[:, :, GDN_QK + lo:GDN_QK + lo + GDN_DK]
        v = y[:, :, 2 * GDN_QK + h * GDN_DV:2 * GDN_QK + (h + 1) * GDN_DV]
        q = q * (lax.rsqrt(jnp.sum(q * q, axis=-1, keepdims=True) + RMS_EPS) * GDN_DK ** -0.5)
        k = k * lax.rsqrt(jnp.sum(k * k, axis=-1, keepdims=True) + RMS_EPS)
        beta = beta_all[:, :, h:h + 1]
        gam = gam_all[:, :, GDN_HEADS + h:GDN_HEADS + h + 1]
        gam_row = gam_all_t[:, GDN_HEADS + h:GDN_HEADS + h + 1, :]
        egam = egam_all[:, :, GDN_HEADS + h:GDN_HEADS + h + 1]
        gam_last = gam_row[:, :, c - 1:c]
        dec = jnp.where(incl, jnp.exp(gam - gam_row), 0.0)
        kk = _bmm_nt(k, k)
        a_mat = jnp.where(strict, beta * kk * dec, 0.0)
        rhs = jnp.concatenate([v * beta, k * (beta * egam)], axis=-1)
        sol = _unit_lower_solve(a_mat, rhs, row, col, _bmm)
        u0, wk = sol[:, :, :GDN_DV], sol[:, :, GDN_DV:]
        qk = _bmm_nt(q, k) * dec
        q_dec = q * egam
        k_dec = k * jnp.exp(gam_last - gam)
        s = s_ref[:, h]
        u = u0 - _bmm(wk, s)
        o = _bmm(q_dec, s) + _bmm(qk, u)
        s_ref[:, h] = s * jnp.exp(gam_last) + _bmm(jnp.swapaxes(k_dec, 1, 2), u)
        o = o * lax.rsqrt(jnp.mean(o * o, axis=-1, keepdims=True) + RMS_EPS) * normw_ref[...]
        o_ref[:, :, h * GDN_DV:(h + 1) * GDN_DV] = o * _silu(gate_ref[:, :, h * GDN_DV:(h + 1) * GDN_DV])


def _gdn_prompt_job(a_qkv, a_gate, small, conv_w, a_log, dt_bias, norm_w):
    bsz, L, _ = a_qkv.shape
    c = GDN_CHUNK
    assert L % c == 0
    lanes = small.shape[-1]
    alog_row = jnp.zeros((1, lanes), F32).at[0, GDN_HEADS:2 * GDN_HEADS].set(a_log)
    dtb_row = jnp.zeros((1, lanes), F32).at[0, GDN_HEADS:2 * GDN_HEADS].set(dt_bias)
    const2 = lambda n: (0, 0)
    return dict(
        kernel=_gdn_prompt_kernel,
        steps=L // c,
        args=[a_qkv, a_gate, small, conv_w, alog_row, dtb_row, norm_w.reshape(1, GDN_DV)],
        in_specs=[
            pl.BlockSpec((bsz, c, GDN_CONV_DIM), lambda n: (0, n, 0)),
            pl.BlockSpec((bsz, c, GDN_VW), lambda n: (0, n, 0)),
            pl.BlockSpec((bsz, c, lanes), lambda n: (0, n, 0)),
            pl.BlockSpec((CONV_WIDTH, GDN_CONV_DIM), const2),
            pl.BlockSpec((1, lanes), const2),
            pl.BlockSpec((1, lanes), const2),
            pl.BlockSpec((1, GDN_DV), const2),
        ],
        out_shape=[jax.ShapeDtypeStruct((bsz, L, GDN_VW), F32),
                   jax.ShapeDtypeStruct((bsz, GDN_HEADS, GDN_DK, GDN_DV), F32),
                   jax.ShapeDtypeStruct((bsz, CONV_WIDTH - 1, GDN_CONV_DIM), F32)],
        out_specs=[
            pl.BlockSpec((bsz, c, GDN_VW), lambda n: (0, n, 0)),
            pl.BlockSpec((bsz, GDN_HEADS, GDN_DK, GDN_DV), lambda n: (0, 0, 0, 0)),
            pl.BlockSpec((bsz, CONV_WIDTH - 1, GDN_CONV_DIM), lambda n: (0, 0, 0)),
        ],
        scratch=[pltpu.VMEM((bsz, c + 8, GDN_CONV_DIM), F32)])


def run_jobs(name, *jobs):
    steps = jobs[0]['steps']
    assert all(j['steps'] == steps for j in jobs)
    n_in = [len(j['args']) for j in jobs]
    n_out = [len(j['out_shape']) for j in jobs]
    n_scr = [len(j['scratch']) for j in jobs]

    def body(*refs):
        ins, outs, scr = refs[:sum(n_in)], refs[sum(n_in):sum(n_in) + sum(n_out)], refs[sum(n_in) + sum(n_out):]
        for idx, j in enumerate(jobs):
            a, b, c = sum(n_in[:idx]), sum(n_out[:idx]), sum(n_scr[:idx])
            j['kernel'](*ins[a:a + n_in[idx]], *outs[b:b + n_out[idx]], *scr[c:c + n_scr[idx]])

    aliases = {sum(n_in[:idx]) + a: sum(n_out[:idx]) + o
               for idx, j in enumerate(jobs) for a, o in j.get('aliases', {}).items()}
    res = pl.pallas_call(
        body,
        out_shape=tuple(s for j in jobs for s in j['out_shape']),
        grid=(steps,),
        in_specs=[s for j in jobs for s in j['in_specs']],
        out_specs=tuple(s for j in jobs for s in j['out_specs']),
        scratch_shapes=[s for j in jobs for s in j['scratch']],
        input_output_aliases=aliases,
        compiler_params=pltpu.CompilerParams(
            dimension_semantics=("arbitrary",), vmem_limit_bytes=VMEM_LIMIT_BYTES),
        name=name,
    )(*[a for j in jobs for a in j['args']])
    return [res[sum(n_out[:idx]):sum(n_out[:idx]) + n] for idx, n in enumerate(n_out)]


SSD_DT_LANE = 2 * GDN_HEADS
SSD_BC = SSD_GROUPS * SSD_STATE
SSD_HPG = SSD_HEADS // SSD_GROUPS


def _ssd_prompt_kernel(xbc_ref, z_ref, small_ref, convw_ref, convb_ref, alog_ref, dtb_ref, d_ref, normw_ref,
                       o_ref, h_ref, cst_ref, xp_ref):
    c = SSD_CHUNK
    nb = xbc_ref.shape[0]

    @pl.when(pl.program_id(0) == 0)
    def _():
        h_ref[...] = jnp.zeros_like(h_ref)
        xp_ref[:, 0:8, :] = jnp.zeros((nb, 8, SSD_CONV_DIM), F32)

    x = xbc_ref[...]
    xp_ref[:, 8:8 + c, :] = x
    w = convw_ref[...]
    y = (xp_ref[:, 5:5 + c, :] * w[0:1, :] + xp_ref[:, 6:6 + c, :] * w[1:2, :]
         + xp_ref[:, 7:7 + c, :] * w[2:3, :] + x * w[3:4, :]) + convb_ref[...]
    xp_ref[:, 0:8, :] = xp_ref[:, c:c + 8, :]
    cst_ref[...] = x[:, c - (CONV_WIDTH - 1):, :]
    y = _silu(y)
    xs = y[:, :, :SSD_INNER]
    bm = y[:, :, SSD_INNER:SSD_INNER + SSD_BC]
    cm = y[:, :, SSD_INNER + SSD_BC:]

    dt_all = _softplus(small_ref[...] + dtb_ref[...])
    da_all = dt_all * (-jnp.exp(alog_ref[...]))
    row = lax.broadcasted_iota(jnp.int32, (c, c), 0)
    col = lax.broadcasted_iota(jnp.int32, (c, c), 1)
    incl = row >= col
    tri = jnp.broadcast_to(incl.astype(BF16), (nb, c, c))
    acum_all = _cumsum_rows(tri, da_all)
    acum_all_t = jnp.swapaxes(acum_all, 1, 2)
    eacum_all = jnp.exp(acum_all)

    ys = []
    for g in range(SSD_GROUPS):
        bg = bm[:, :, g * SSD_STATE:(g + 1) * SSD_STATE]
        cg = cm[:, :, g * SSD_STATE:(g + 1) * SSD_STATE]
        scores = _bmm_nt(cg, bg)
        for hh in range(SSD_HPG):
            h = g * SSD_HPG + hh
            ln = SSD_DT_LANE + h
            dtc = dt_all[:, :, ln:ln + 1]
            ac = acum_all[:, :, ln:ln + 1]
            ar = acum_all_t[:, ln:ln + 1, :]
            alast = ar[:, :, c - 1:c]
            lmat = jnp.where(incl, jnp.exp(ac - ar), 0.0)
            xd = xs[:, :, h * SSD_HEADDIM:(h + 1) * SSD_HEADDIM] * dtc
            y_diag = _bmm(scores * lmat, xd)
            hin = h_ref[:, h]
            y_off = _bmm_nt(cg * eacum_all[:, :, ln:ln + 1], hin)
            states = _bmm(jnp.swapaxes(xd, 1, 2), bg * jnp.exp(alast - ac))
            h_ref[:, h] = hin * jnp.exp(alast) + states
            ys.append(y_diag + y_off)
    yf = jnp.concatenate(ys, axis=-1) + d_ref[...] * xs
    yf = yf * _silu(z_ref[...])
    gw = SSD_INNER // SSD_GROUPS
    for g in range(SSD_GROUPS):
        seg = yf[:, :, g * gw:(g + 1) * gw]
        o_ref[:, :, g * gw:(g + 1) * gw] = (seg * lax.rsqrt(jnp.mean(seg * seg, axis=-1, keepdims=True) + RMS_EPS)
                                            * normw_ref[:, g * gw:(g + 1) * gw])


def ssd_prompt(b_xbc, b_z, small, conv_w, conv_b, a_log, dt_bias, d_skip, norm_w):
    bsz, L, _ = b_xbc.shape
    c = SSD_CHUNK
    assert L % c == 0
    lanes = small.shape[-1]
    sl = slice(SSD_DT_LANE, SSD_DT_LANE + SSD_HEADS)
    alog_row = jnp.zeros((1, lanes), F32).at[0, sl].set(a_log)
    dtb_row = jnp.zeros((1, lanes), F32).at[0, sl].set(dt_bias)
    d_row = jnp.repeat(d_skip, SSD_HEADDIM).reshape(1, SSD_INNER)
    const2 = lambda n: (0, 0)
    return pl.pallas_call(
        _ssd_prompt_kernel,
        out_shape=(jax.ShapeDtypeStruct((bsz, L, SSD_INNER), F32),
                   jax.ShapeDtypeStruct((bsz, SSD_HEADS, SSD_HEADDIM, SSD_STATE), F32),
                   jax.ShapeDtypeStruct((bsz, CONV_WIDTH - 1, SSD_CONV_DIM), F32)),
        grid=(L // c,),
        in_specs=[
            pl.BlockSpec((bsz, c, SSD_CONV_DIM), lambda n: (0, n, 0)),
            pl.BlockSpec((bsz, c, SSD_INNER), lambda n: (0, n, 0)),
            pl.BlockSpec((bsz, c, lanes), lambda n: (0, n, 0)),
            pl.BlockSpec((CONV_WIDTH, SSD_CONV_DIM), const2),
            pl.BlockSpec((1, SSD_CONV_DIM), const2),
            pl.BlockSpec((1, lanes), const2),
            pl.BlockSpec((1, lanes), const2),
            pl.BlockSpec((1, SSD_INNER), const2),
            pl.BlockSpec((1, SSD_INNER), const2),
        ],
        out_specs=(
            pl.BlockSpec((bsz, c, SSD_INNER), lambda n: (0, n, 0)),
            pl.BlockSpec((bsz, SSD_HEADS, SSD_HEADDIM, SSD_STATE), lambda n: (0, 0, 0, 0)),
            pl.BlockSpec((bsz, CONV_WIDTH - 1, SSD_CONV_DIM), lambda n: (0, 0, 0)),
        ),
        scratch_shapes=[pltpu.VMEM((bsz, c + 8, SSD_CONV_DIM), F32)],
        compiler_params=pltpu.CompilerParams(
            dimension_semantics=("arbitrary",), vmem_limit_bytes=VMEM_LIMIT_BYTES),
        name="ssd_prompt",
    )(b_xbc, b_z, small, conv_w, conv_b.reshape(1, SSD_CONV_DIM), alog_row, dtb_row, d_row,
      norm_w.reshape(1, SSD_INNER))


ATT_BAND = 128
ATT_MASKED = -1e30
assert all(w // d == ATT_BAND for w, d in ATT_GROUPS)


def _alibi_slopes_np():
    idx = np.arange(1, ATT_HEADS + 1, dtype=np.float32)
    return np.exp2(-8.0 * idx / ATT_HEADS).astype(np.float32).reshape(N_ATT_GROUPS, ATT_HEADS_PER_GROUP)


def _attn_bias(gi, first):
    nk = 2 * ATT_BAND
    i = lax.broadcasted_iota(jnp.int32, (ATT_BAND, nk), 0)
    j = lax.broadcasted_iota(jnp.int32, (ATT_BAND, nk), 1)
    du = i + ATT_BAND - j
    duf = du.astype(F32)
    dil = ATT_GROUPS[gi][1]
    slopes = _alibi_slopes_np()[gi]
    if first:
        inside = jnp.minimum(du, j - ATT_BAND) >= 0
    else:
        inside = jnp.abs(2 * du - ATT_BAND) <= ATT_BAND
    return [jnp.where(inside, duf * float(-slopes[hh] * dil * LOG2E), ATT_MASKED)
            for hh in range(ATT_HEADS_PER_GROUP)]


ATT_LANE_CHUNKS = ATT_OUT // 128
ATT_HEADS_PER_CHUNK = 128 // ATT_HEAD_DIM
ATT_UNROLL = (15, 12, 16)
LOG2E = math.log2(math.e)
LN2 = math.log(2.0)


def _attn_block(q, keys, vals, bias):
    nk = keys.shape[0]
    low = lax.broadcasted_iota(jnp.int32, (ATT_BAND, 128), 1) < ATT_HEAD_DIM
    q2 = q * LOG2E
    kb = keys.astype(BF16)
    va = jnp.concatenate([vals.astype(BF16), jnp.ones((nk, 128), BF16)], axis=1)
    res = []
    for hh in range(ATT_HEADS_PER_CHUNK):
        qh = jnp.where(low if hh == 0 else jnp.logical_not(low), q2, 0.0).astype(BF16)
        s = lax.dot_general(qh, kb, (((1,), (1,)), ((), ())), preferred_element_type=F32) + bias[hh]
        m = jnp.max(s, axis=-1, keepdims=True)
        p = jnp.exp2(s - m)
        oa = jnp.dot(p.astype(BF16), va, preferred_element_type=F32)
        l = oa[:, 128:]
        res.append((oa[:, :128] * (1.0 / l), (m + jnp.log2(l)) * LN2))
    return jnp.where(low, res[0][0], res[1][0]), jnp.where(low, res[0][1], res[1][1])


def _attn_prompt_kernel(q_ref, kv_ref, o_ref, lse_ref):
    g = pl.program_id(0)
    L = q_ref.shape[2]
    nc = ATT_LANE_CHUNKS

    for gi, (win, dil) in enumerate(ATT_GROUPS):
        @pl.when(g == gi)
        def _(gi=gi, win=win, dil=dil):
            nblk = L // win

            def rows(start):
                return pl.ds(start, ATT_BAND, stride=dil) if dil > 1 else pl.ds(start, ATT_BAND)

            def unit(cur, prev, bias):
                for c in range(nc):
                    hs = slice(c * ATT_HEADS_PER_CHUNK, (c + 1) * ATT_HEADS_PER_CHUNK)
                    keys = jnp.concatenate([kv_ref[0, c, prev, :], kv_ref[0, c, cur, :]], axis=0)
                    vals = jnp.concatenate([kv_ref[0, nc + c, prev, :], kv_ref[0, nc + c, cur, :]], axis=0)
                    o, lse = _attn_block(q_ref[0, c, cur, :], keys, vals, bias[hs])
                    o_ref[0, c, cur, :] = o
                    lse_ref[0, c, cur, :] = lse

            bias0 = _attn_bias(gi, True)

            def body0(r, carry):
                unit(rows(r), rows(r), bias0)
                return carry

            lax.fori_loop(0, dil, body0, 0, unroll=min(dil, ATT_UNROLL[gi]))
            if nblk > 1:
                bias1 = _attn_bias(gi, False)

                def body(u, carry):
                    start = (u // dil + 1) * win + u % dil
                    unit(rows(start), rows(start - win), bias1)
                    return carry

                lax.fori_loop(0, (nblk - 1) * dil, body, 0, unroll=ATT_UNROLL[gi])


def attn_prompt(q5, kv5, bsz):
    ng, nc, _, lanes = q5.shape
    L = SEQ
    assert all(L % w == 0 for w, _ in ATT_GROUPS)
    blk = lambda n: pl.BlockSpec((1, n, L, lanes), lambda g, b: (g, 0, b, 0))
    out = jax.ShapeDtypeStruct((ng, nc, bsz * L, lanes), F32)
    return pl.pallas_call(
        _attn_prompt_kernel,
        out_shape=(out, out),
        grid=(ng, bsz),
        in_specs=[blk(nc), blk(2 * nc)],
        out_specs=(blk(nc), blk(nc)),
        compiler_params=pltpu.CompilerParams(
            dimension_semantics=("arbitrary", "arbitrary"), vmem_limit_bytes=VMEM_LIMIT_BYTES),
        name="attn_prompt",
    )(q5, kv5)


SMALL_LANES = 128
_OFF = np.cumsum((0,) + IN_SPLITS).tolist()
PROJ_AB_WIDTHS = (GDN_CONV_DIM, GDN_VW, SSD_INNER, SSD_CONV_DIM, SMALL_LANES)
PROJ_AB_N = sum(PROJ_AB_WIDTHS)
PROJ_C_N = 3 * ATT_HEADS * ATT_HEAD_DIM
ATT_Q_N = ATT_HEADS * ATT_HEAD_DIM


def prep_weights(w_in):
    o = _OFF
    wt = jnp.transpose(w_in, (0, 2, 1))
    pad = jnp.zeros((w_in.shape[0], SMALL_LANES - 2 * GDN_HEADS - SSD_HEADS, w_in.shape[1]), w_in.dtype)
    w_ab = jnp.concatenate([wt[:, o[0]:o[1]], wt[:, o[1]:o[2]], wt[:, o[4]:o[5]], wt[:, o[5]:o[6]],
                            wt[:, o[2]:o[4]], wt[:, o[6]:o[7]], pad], axis=1)
    back = lambda a: jnp.transpose(a.astype(BF16), (0, 2, 1))
    return back(w_ab), back(wt[:, o[7]:o[8]]), back(wt[:, o[8]:o[9]])


def _rmsnorm_rows(h, w_row):
    return h * lax.rsqrt(jnp.mean(h * h, axis=-1, keepdims=True) + RMS_EPS) * w_row


def _proj_ab_kernel(h_ref, nw_ref, w_ref, qkv_ref, gate_ref, z_ref, xbc_ref, small_ref):
    u = _rmsnorm_rows(h_ref[...], nw_ref[0]).astype(BF16)
    y = jnp.dot(u, w_ref[0], preferred_element_type=F32)
    lo = 0
    for ref, wd in zip((qkv_ref, gate_ref, z_ref, xbc_ref, small_ref), PROJ_AB_WIDTHS):
        ref[...] = y[:, lo:lo + wd]
        lo += wd


def proj_ab(h, norm_w, w_ab, layer, tm):
    t = h.shape[0]
    assert t % tm == 0
    return pl.pallas_call(
        _proj_ab_kernel,
        out_shape=tuple(jax.ShapeDtypeStruct((t, wd), F32) for wd in PROJ_AB_WIDTHS),
        grid=(t // tm,),
        in_specs=[pl.BlockSpec((tm, D_MODEL), lambda i: (i, 0)), _layer_spec(norm_w, layer), _layer_spec(w_ab, layer)],
        out_specs=tuple(pl.BlockSpec((tm, wd), lambda i: (i, 0)) for wd in PROJ_AB_WIDTHS),
        compiler_params=pltpu.CompilerParams(
            dimension_semantics=("arbitrary",), vmem_limit_bytes=VMEM_LIMIT_BYTES),
        name="proj_ab",
    )(h, norm_w, w_ab)


def _head_rmsnorm(x, w_row, seg):
    outs = []
    for c in range(x.shape[-1] // 128):
        xc = x[:, c * 128:(c + 1) * 128]
        hi, lo = _split2(xc * xc)
        ss = jnp.dot(hi, seg, preferred_element_type=F32) + jnp.dot(lo, seg, preferred_element_type=F32)
        outs.append(xc * lax.rsqrt(ss * (1.0 / ATT_HEAD_DIM) + RMS_EPS) * w_row)
    return outs


def _proj_c_kernel(h_ref, nw_ref, w_ref, qw_ref, kw_ref, *refs, chunked):
    out_refs = refs[N_ATT_GROUPS:] if chunked else refs
    u = _rmsnorm_rows(h_ref[...], nw_ref[0]).astype(BF16)
    y = jnp.dot(u, w_ref[0], preferred_element_type=F32)
    r = lax.broadcasted_iota(jnp.int32, (128, 128), 0) // ATT_HEAD_DIM
    c = lax.broadcasted_iota(jnp.int32, (128, 128), 1) // ATT_HEAD_DIM
    seg = (r == c).astype(BF16)
    qn = [t * ATT_HEAD_DIM ** -0.5 for t in _head_rmsnorm(y[:, :ATT_Q_N], qw_ref[...], seg)]
    kn = _head_rmsnorm(y[:, ATT_Q_N:2 * ATT_Q_N], kw_ref[...], seg)
    v = y[:, 2 * ATT_Q_N:]
    nc = ATT_LANE_CHUNKS
    if chunked:
        q5_ref, kv5_ref, *kvt_refs = out_refs
    else:
        q_ref, *kvrows_refs = out_refs
    for g in range(N_ATT_GROUPS):
        for cc in range(nc):
            i = g * nc + cc
            vc = v[:, i * 128:(i + 1) * 128]
            if chunked:
                q5_ref[g, cc] = qn[i]
                kv5_ref[g, cc] = kn[i]
                kv5_ref[g, nc + cc] = vc
                wcols = kvt_refs[g].shape[-1]
                kvt_refs[g][0, 0, cc * 128:(cc + 1) * 128, :] = kn[i].T[:, kn[i].shape[0] - wcols:]
                kvt_refs[g][0, 0, ATT_OUT + cc * 128:ATT_OUT + (cc + 1) * 128, :] = vc.T[:, vc.shape[0] - wcols:]
            else:
                q_ref[:, i * 128:(i + 1) * 128] = qn[i]
                kvrows_refs[g][:, cc * 128:(cc + 1) * 128] = kn[i]
                kvrows_refs[g][:, ATT_OUT + cc * 128:ATT_OUT + (cc + 1) * 128] = vc


def _proj_c_job(h, norm_w, w_c, layer, q_norm, k_norm, tm, chunked, kv_carried=None, seq_len=None):
    t = h.shape[0]
    assert t % tm == 0
    const = dict(pipeline_mode=pl.Buffered(1))
    ng, nc = N_ATT_GROUPS, ATT_LANE_CHUNKS
    if chunked:
        out_shape = [jax.ShapeDtypeStruct((ng, nc, t, 128), F32), jax.ShapeDtypeStruct((ng, 2 * nc, t, 128), F32)]
        out_specs = [pl.BlockSpec((ng, nc, tm, 128), lambda i: (0, 0, i, 0)),
                     pl.BlockSpec((ng, 2 * nc, tm, 128), lambda i: (0, 0, i, 0))]
    else:
        out_shape = [jax.ShapeDtypeStruct((t, ATT_Q_N), F32)]
        out_specs = [pl.BlockSpec((tm, ATT_Q_N), lambda i: (i, 0))]
        out_shape += [jax.ShapeDtypeStruct((t, 2 * ATT_OUT), F32)] * ng
        out_specs += [pl.BlockSpec((tm, 2 * ATT_OUT), lambda i: (i, 0))] * ng
    tile2 = lambda w: jnp.tile(w, 128 // ATT_HEAD_DIM).reshape(1, 128)
    args = [h, norm_w, w_c, tile2(q_norm), tile2(k_norm)]
    in_specs = [pl.BlockSpec((tm, D_MODEL), lambda i: (i, 0)),
                _layer_spec(norm_w, layer),
                _layer_spec(w_c, layer),
                pl.BlockSpec((1, 128), lambda i: (0, 0), **const),
                pl.BlockSpec((1, 128), lambda i: (0, 0), **const)]
    aliases = {}
    if chunked:
        assert seq_len % tm == 0
        tiles = seq_len // tm
        for gi, (win, _) in enumerate(ATT_GROUPS):
            w = min(win, seq_len)
            wblk = min(w, tm)
            first = (seq_len - w) // tm
            shape = (DEPTH, t // seq_len, 2 * ATT_OUT, w)
            out_shape.append(jax.ShapeDtypeStruct(shape, F32))
            out_specs.append(pl.BlockSpec(
                (1, 1, 2 * ATT_OUT, wblk),
                lambda i, first=first: (layer, i // tiles, 0, jnp.maximum(i % tiles - first, 0))))
            carried, al = _carried(None if kv_carried is None else kv_carried[gi], shape, len(args))
            aliases.update({k: len(out_shape) - 1 for k in al})
            args.append(carried)
            in_specs.append(pl.BlockSpec(memory_space=pl.ANY))
    return dict(
        kernel=functools.partial(_proj_c_kernel, chunked=chunked),
        steps=t // tm,
        args=args,
        in_specs=in_specs,
        out_shape=out_shape,
        out_specs=out_specs,
        scratch=[],
        aliases=aliases)


ROW_SUBTILE = 256


def _merge_kernel(h_ref, oa_ref, ob_ref, *refs, chunked):
    att_refs, refs = (refs[:2], refs[2:]) if chunked else (refs[:1], refs[1:])
    nw_ref, wg_ref, gb_ref, wa_ref, wb_ref, wc_ref, wo_ref, out_ref = refs
    tm = h_ref.shape[0]
    sub = min(tm, ROW_SUBTILE)
    for r0 in range(0, tm, sub):
        rs = slice(r0, r0 + sub)
        if chunked:
            o5_ref, lse5_ref = att_refs
            ocs = []
            for cc in range(ATT_LANE_CHUNKS):
                ls = [lse5_ref[g, cc, rs, :] for g in range(N_ATT_GROUPS)]
                m = functools.reduce(jnp.maximum, ls)
                es = [jnp.exp(l - m) for l in ls]
                num = sum(e * o5_ref[g, cc, rs, :] for g, e in enumerate(es))
                ocs.append(num * (1.0 / sum(es)))
            o_c = jnp.concatenate(ocs, axis=-1)
        else:
            o_c = att_refs[0][rs, :]
        h = h_ref[rs, :]
        u = _rmsnorm_rows(h, nw_ref[0]).astype(BF16)
        merged = None
        for i, (o, w_ref) in enumerate(((oa_ref[rs, :], wa_ref), (ob_ref[rs, :], wb_ref), (o_c, wc_ref))):
            sl = slice(i * D_MODEL, (i + 1) * D_MODEL)
            gate = jax.nn.sigmoid(jnp.dot(u, wg_ref[0, :, sl], preferred_element_type=F32) + gb_ref[0, :, sl])
            term = gate * jnp.dot(o.astype(BF16), w_ref[0], preferred_element_type=F32)
            merged = term if merged is None else merged + term
        out_ref[rs, :] = h + jnp.dot(merged.astype(BF16), wo_ref[0], preferred_element_type=F32)


def merge_block(h, o_a, o_b, o_att, params, layer, tm):
    t = h.shape[0]
    assert t % tm == 0
    chunked = isinstance(o_att, tuple)
    rows = lambda wd: pl.BlockSpec((tm, wd), lambda i: (i, 0))
    if chunked:
        att_specs = [pl.BlockSpec((N_ATT_GROUPS, ATT_LANE_CHUNKS, tm, 128), lambda i: (0, 0, i, 0))] * 2
        att_args = list(o_att)
    else:
        att_specs = [rows(ATT_OUT)]
        att_args = [o_att]
    return pl.pallas_call(
        functools.partial(_merge_kernel, chunked=chunked),
        out_shape=jax.ShapeDtypeStruct(h.shape, F32),
        grid=(t // tm,),
        in_specs=[rows(D_MODEL), rows(GDN_VW), rows(SSD_INNER)] + att_specs + [_layer_spec(a, layer) for a in params],
        out_specs=rows(D_MODEL),
        compiler_params=pltpu.CompilerParams(
            dimension_semantics=("arbitrary",), vmem_limit_bytes=VMEM_LIMIT_BYTES),
        name="merge",
    )(h, o_a, o_b, *att_args, *params)


DEC_TB = 8


def _conv_step(cst_ref, x, w_ref):
    prev = [cst_ref[0, i] for i in range(CONV_WIDTH - 1)]
    y = x * w_ref[CONV_WIDTH - 1:CONV_WIDTH, :]
    for i, pr in enumerate(prev):
        y = y + pr * w_ref[i:i + 1, :]
    return y, prev[1:] + [x]


def _gdn_decode_kernel(qkv_ref, gate_ref, small_ref, cst_ref, s_ref, carried_ref, convw_ref, alog_ref, dtb_ref,
                       normw_ref, o_ref, so_ref, csto_ref):
    del carried_ref
    tb = qkv_ref.shape[0]
    x = qkv_ref[...]
    y, new_cst = _conv_step(cst_ref, x, convw_ref)
    for i, r in enumerate(new_cst):
        csto_ref[i] = r
    y = _silu(y)
    small = small_ref[...]
    beta_all = jax.nn.sigmoid(small)
    eg_all = jnp.exp(-jnp.exp(alog_ref[...]) * _softplus(small + dtb_ref[...]))
    lanes = small.shape[-1]
    r = lax.broadcasted_iota(jnp.int32, (lanes, GDN_HEADS * GDN_DV), 0)
    c = lax.broadcasted_iota(jnp.int32, (lanes, GDN_HEADS * GDN_DV), 1)
    sel = (r == GDN_HEADS + c // GDN_DV).astype(BF16)
    eg_wide = sum(jnp.dot(t, sel, preferred_element_type=F32) for t in _split3(eg_all))
    tok = lax.broadcasted_iota(jnp.int32, (tb, 1), 0)
    tok2 = lax.broadcasted_iota(jnp.int32, (2 * tb, 1), 0)
    for h in range(GDN_HEADS):
        lo = h * GDN_DK
        q = y[:, lo:lo + GDN_DK]
        k = y[:, GDN_QK + lo:GDN_QK + lo + GDN_DK]
        v = y[:, 2 * GDN_QK + h * GDN_DV:2 * GDN_QK + (h + 1) * GDN_DV]
        q = q * (lax.rsqrt(jnp.sum(q * q, axis=-1, keepdims=True) + RMS_EPS) * GDN_DK ** -0.5)
        k = k * lax.rsqrt(jnp.sum(k * k, axis=-1, keepdims=True) + RMS_EPS)
        beta = beta_all[:, h:h + 1]
        eg = eg_all[:, GDN_HEADS + h:GDN_HEADS + h + 1]
        qk = jnp.sum(q * k, axis=-1, keepdims=True)
        kq = jnp.concatenate([k, q], axis=0).astype(BF16)
        kq_s = jnp.zeros((2 * tb, GDN_DV), F32)
        for t in range(tb):
            r = jnp.dot(kq, s_ref[0, t, h].astype(BF16), preferred_element_type=F32)
            kq_s = jnp.where(tok2 % tb == t, r, kq_s)
        u = beta * (v - eg * kq_s[:tb])
        o = eg * kq_s[tb:] + qk * u
        kb = k.astype(BF16)
        for t in range(tb):
            outer = lax.dot_general(kb, jnp.where(tok == t, u, 0.0).astype(BF16), (((0,), (0,)), ((), ())),
                                    preferred_element_type=F32)
            so_ref[0, t, h] = s_ref[0, t, h] * eg_wide[t:t + 1, h * GDN_DV:(h + 1) * GDN_DV] + outer
        o = o * lax.rsqrt(jnp.mean(o * o, axis=-1, keepdims=True) + RMS_EPS) * normw_ref[...]
        o_ref[:, h * GDN_DV:(h + 1) * GDN_DV] = o * _silu(gate_ref[:, h * GDN_DV:(h + 1) * GDN_DV])


def _carried(new_states, shape, operand_index):
    if new_states is None:
        return jnp.zeros((8, 128), F32), {}
    assert new_states.shape == shape
    return new_states, {operand_index: 1}


def gdn_decode(a_qkv, a_gate, small, conv_state_t, state, new_states, layer, conv_w, a_log, dt_bias, norm_w):
    bsz = a_qkv.shape[0]
    prev, aliases = _carried(new_states, state.shape, 5)
    tb = DEC_TB
    lanes = small.shape[-1]
    alog_row = jnp.zeros((1, lanes), F32).at[0, GDN_HEADS:2 * GDN_HEADS].set(a_log)
    dtb_row = jnp.zeros((1, lanes), F32).at[0, GDN_HEADS:2 * GDN_HEADS].set(dt_bias)
    nprev = CONV_WIDTH - 1
    rows = lambda wd: pl.BlockSpec((tb, wd), lambda i: (i, 0))
    const2 = lambda i: (0, 0)
    return pl.pallas_call(
        _gdn_decode_kernel,
        out_shape=(jax.ShapeDtypeStruct((bsz, GDN_VW), F32),
                   jax.ShapeDtypeStruct(state.shape, F32),
                   jax.ShapeDtypeStruct((nprev, bsz, GDN_CONV_DIM), F32)),
        grid=(bsz // tb,),
        in_specs=[rows(GDN_CONV_DIM), rows(GDN_VW), rows(lanes),
                  pl.BlockSpec((1, nprev, tb, GDN_CONV_DIM), lambda i: (layer, 0, i, 0)),
                  pl.BlockSpec((1, tb, GDN_HEADS, GDN_DK, GDN_DV), lambda i: (layer, i, 0, 0, 0)),
                  pl.BlockSpec(memory_space=pl.ANY),
                  pl.BlockSpec((CONV_WIDTH, GDN_CONV_DIM), const2),
                  pl.BlockSpec((1, lanes), const2), pl.BlockSpec((1, lanes), const2),
                  pl.BlockSpec((1, GDN_DV), const2)],
        out_specs=(rows(GDN_VW),
                   pl.BlockSpec((1, tb, GDN_HEADS, GDN_DK, GDN_DV), lambda i: (layer, i, 0, 0, 0)),
                   pl.BlockSpec((nprev, tb, GDN_CONV_DIM), lambda i: (0, i, 0))),
        input_output_aliases=aliases,
        compiler_params=pltpu.CompilerParams(
            dimension_semantics=("arbitrary",), vmem_limit_bytes=VMEM_LIMIT_BYTES),
        name="gdn_decode",
    )(a_qkv, a_gate, small, conv_state_t, state, prev, conv_w, alog_row, dtb_row, norm_w.reshape(1, GDN_DV))


def _ssd_decode_kernel(xbc_ref, z_ref, small_ref, cst_ref, h_ref, carried_ref, convw_ref, convb_ref, alog_ref,
                       dtb_ref, d_ref, normw_ref, o_ref, ho_ref, csto_ref):
    del carried_ref
    tb = xbc_ref.shape[0]
    x = xbc_ref[...]
    y, new_cst = _conv_step(cst_ref, x, convw_ref)
    for i, r in enumerate(new_cst):
        csto_ref[i] = r
    y = _silu(y + convb_ref[...])
    xs = y[:, :SSD_INNER]
    bm = y[:, SSD_INNER:SSD_INNER + SSD_BC]
    cm = y[:, SSD_INNER + SSD_BC:]
    dt_all = _softplus(small_ref[...] + dtb_ref[...])
    eda_all = jnp.exp(dt_all * (-jnp.exp(alog_ref[...])))
    lanes = dt_all.shape[-1]

    def spread(v, width):
        r = lax.broadcasted_iota(jnp.int32, (lanes, SSD_HEADS * width), 0)
        c = lax.broadcasted_iota(jnp.int32, (lanes, SSD_HEADS * width), 1)
        sel = (r == SSD_DT_LANE + c // width).astype(BF16)
        return sum(jnp.dot(t, sel, preferred_element_type=F32) for t in _split3(v))

    eda_wide = spread(eda_all, SSD_STATE)
    xd = xs * spread(dt_all, SSD_HEADDIM)
    tok = lax.broadcasted_iota(jnp.int32, (tb, 1), 0)
    gp = SSD_HPG * SSD_HEADDIM
    ys = []
    for g in range(SSD_GROUPS):
        bg = bm[:, g * SSD_STATE:(g + 1) * SSD_STATE]
        cg = cm[:, g * SSD_STATE:(g + 1) * SSD_STATE].astype(BF16)
        xd_g = xd[:, g * gp:(g + 1) * gp].astype(BF16)
        y_g = jnp.zeros((tb, gp), F32)
        for t in range(tb):
            outer = lax.dot_general(xd_g, jnp.where(tok == t, bg, 0.0).astype(BF16), (((0,), (0,)), ((), ())),
                                    preferred_element_type=F32)
            hns = []
            for hh in range(SSD_HPG):
                h = g * SSD_HPG + hh
                hn = (h_ref[0, t, h] * eda_wide[t:t + 1, h * SSD_STATE:(h + 1) * SSD_STATE]
                      + outer[hh * SSD_HEADDIM:(hh + 1) * SSD_HEADDIM])
                ho_ref[0, t, h] = hn
                hns.append(hn.astype(BF16))
            y_t = lax.dot_general(cg, jnp.concatenate(hns, axis=0), (((1,), (1,)), ((), ())),
                                  preferred_element_type=F32)
            y_g = jnp.where(tok == t, y_t, y_g)
        ys.append(y_g)
    yf = jnp.concatenate(ys, axis=-1) + d_ref[...] * xs
    yf = yf * _silu(z_ref[...])
    gw = SSD_INNER // SSD_GROUPS
    for g in range(SSD_GROUPS):
        seg = yf[:, g * gw:(g + 1) * gw]
        o_ref[:, g * gw:(g + 1) * gw] = (seg * lax.rsqrt(jnp.mean(seg * seg, axis=-1, keepdims=True) + RMS_EPS)
                                         * normw_ref[:, g * gw:(g + 1) * gw])


def ssd_decode(b_xbc, b_z, small, conv_state_t, state, new_states, layer, conv_w, conv_b, a_log, dt_bias, d_skip,
               norm_w):
    bsz = b_xbc.shape[0]
    prev, aliases = _carried(new_states, state.shape, 5)
    tb = DEC_TB
    lanes = small.shape[-1]
    sl = slice(SSD_DT_LANE, SSD_DT_LANE + SSD_HEADS)
    alog_row = jnp.zeros((1, lanes), F32).at[0, sl].set(a_log)
    dtb_row = jnp.zeros((1, lanes), F32).at[0, sl].set(dt_bias)
    d_row = jnp.repeat(d_skip, SSD_HEADDIM).reshape(1, SSD_INNER)
    nprev = CONV_WIDTH - 1
    rows = lambda wd: pl.BlockSpec((tb, wd), lambda i: (i, 0))
    const2 = lambda i: (0, 0)
    return pl.pallas_call(
        _ssd_decode_kernel,
        out_shape=(jax.ShapeDtypeStruct((bsz, SSD_INNER), F32),
                   jax.ShapeDtypeStruct(state.shape, F32),
                   jax.ShapeDtypeStruct((nprev, bsz, SSD_CONV_DIM), F32)),
        grid=(bsz // tb,),
        in_specs=[rows(SSD_CONV_DIM), rows(SSD_INNER), rows(lanes),
                  pl.BlockSpec((1, nprev, tb, SSD_CONV_DIM), lambda i: (layer, 0, i, 0)),
                  pl.BlockSpec((1, tb, SSD_HEADS, SSD_HEADDIM, SSD_STATE), lambda i: (layer, i, 0, 0, 0)),
                  pl.BlockSpec(memory_space=pl.ANY),
                  pl.BlockSpec((CONV_WIDTH, SSD_CONV_DIM), const2), pl.BlockSpec((1, SSD_CONV_DIM), const2),
                  pl.BlockSpec((1, lanes), const2), pl.BlockSpec((1, lanes), const2),
                  pl.BlockSpec((1, SSD_INNER), const2), pl.BlockSpec((1, SSD_INNER), const2)],
        out_specs=(rows(SSD_INNER),
                   pl.BlockSpec((1, tb, SSD_HEADS, SSD_HEADDIM, SSD_STATE), lambda i: (layer, i, 0, 0, 0)),
                   pl.BlockSpec((nprev, tb, SSD_CONV_DIM), lambda i: (0, i, 0))),
        input_output_aliases=aliases,
        compiler_params=pltpu.CompilerParams(
            dimension_semantics=("arbitrary",), vmem_limit_bytes=VMEM_LIMIT_BYTES),
        name="ssd_decode",
    )(b_xbc, b_z, small, conv_state_t, state, prev, conv_w, conv_b.reshape(1, SSD_CONV_DIM),
      alog_row, dtb_row, d_row, norm_w.reshape(1, SSD_INNER))


ATT_DEC_TB = 8
ATT_DEC_TOK = 2


def _attn_decode_tokens(q_ref, kvn_refs, c_refs, k, acc, lane0):
    tb = q_ref.shape[0]
    ntok = c_refs[0].shape[1]
    hd = ATT_HEAD_DIM
    slopes = _alibi_slopes_np()
    q_t = q_ref[...].T
    kvn_t = [ref[...].T for ref in kvn_refs]
    lane_t = lax.broadcasted_iota(jnp.int32, (1, tb), 1)

    def column(x_t, tok):
        return jnp.sum(jnp.where(lane_t == tok, x_t, 0.0), axis=1, keepdims=True)

    biases = []
    for gi, (win, dil) in enumerate(ATT_GROUPS):
        dist = win - lax.broadcasted_iota(jnp.int32, (ATT_HEADS_PER_GROUP, win), 1)
        head = lax.broadcasted_iota(jnp.int32, (ATT_HEADS_PER_GROUP, win), 0)
        slope = functools.reduce(lambda acc_, hh: jnp.where(head == hh, float(slopes[gi, hh]), acc_),
                                 range(ATT_HEADS_PER_GROUP), jnp.zeros((ATT_HEADS_PER_GROUP, win), F32))
        biases.append(jnp.where(dist % dil == 0, -slope * dist.astype(F32), ATT_MASKED))
    lane = lax.broadcasted_iota(jnp.int32, acc.shape, 1)
    for tt in range(ntok):
        tok = k * ntok + tt
        q_col = column(q_t, tok)
        og, lg = [], []
        for gi, c_ref in enumerate(c_refs):
            kvn_col = column(kvn_t[gi], tok)
            heads = range(ATT_HEADS_PER_GROUP)
            qcs = [q_col[gi * ATT_OUT + hh * hd:gi * ATT_OUT + (hh + 1) * hd] for hh in heads]
            kns = [kvn_col[hh * hd:(hh + 1) * hd] for hh in heads]
            vns = [kvn_col[ATT_OUT + hh * hd:ATT_OUT + (hh + 1) * hd] for hh in heads]
            s = jnp.concatenate([jnp.sum(c_ref[0, tt, 0, hh] * qcs[hh], axis=0, keepdims=True) for hh in heads],
                                axis=0) + biases[gi]
            s_new = jnp.concatenate([jnp.sum(qcs[hh] * kns[hh], axis=0, keepdims=True) for hh in heads], axis=0)
            m = jnp.maximum(jnp.max(s, axis=-1, keepdims=True), s_new)
            p = jnp.exp(s - m)
            p_new = jnp.exp(s_new - m)
            inv_l = 1.0 / (jnp.sum(p, axis=-1, keepdims=True) + p_new)
            og.append([(jnp.sum(c_ref[0, tt, 1, hh] * p[hh:hh + 1], axis=-1, keepdims=True)
                        + p_new[hh:hh + 1] * vns[hh]) * inv_l[hh:hh + 1] for hh in heads])
            lg.append(m - jnp.log(inv_l))
        m = functools.reduce(jnp.maximum, lg)
        es = [jnp.exp(l - m) for l in lg]
        inv = 1.0 / sum(es)
        ws = [e * inv for e in es]
        o_col = jnp.concatenate([sum(ws[gi][hh:hh + 1] * og[gi][hh] for gi in range(N_ATT_GROUPS))
                                 for hh in range(ATT_HEADS_PER_GROUP)], axis=0)
        acc = jnp.where(lane == lane0 + tt, o_col, acc)
    return acc


def _ffn_attn_decode_kernel(h_ref, nw_ref, wi_ref, wo_ref, q_ref, kvn0_ref, kvn1_ref, kvn2_ref,
                            c0_ref, c1_ref, c2_ref, o_ref, oct_ref):
    step = pl.program_id(0)
    ntok = c0_ref.shape[1]
    steps_per_block = q_ref.shape[0] // ntok

    @pl.when(step == 0)
    def _():
        oct_ref[...] = jnp.zeros_like(oct_ref)

    oct_ref[...] = _attn_decode_tokens(q_ref, (kvn0_ref, kvn1_ref, kvn2_ref), (c0_ref, c1_ref, c2_ref),
                                       step % steps_per_block, oct_ref[...], step * ntok)
    _ffn_kernel(h_ref, nw_ref, wi_ref, wo_ref, o_ref)


def ffn_attn_decode(h, norm_w, w_in_bf16, w_out_bf16, ffn_layer, q_rows, kv_new, caches, layer):
    t = h.shape[0]
    bsz = q_rows.shape[0]
    tb, ntok = ATT_DEC_TB, ATT_DEC_TOK
    steps = bsz // ntok
    assert t % steps == 0 and bsz % tb == 0 and tb % ntok == 0 and bsz % 128 == 0
    tm = t // steps
    views, specs = [], []
    for cache, (win, dil) in zip(caches, ATT_GROUPS):
        assert cache.shape[2] == win and win % dil == 0
        views.append(jnp.transpose(cache, (0, 1, 3, 4, 5, 2)))
        specs.append(pl.BlockSpec((1, ntok) + KV_TAIL + (win,), lambda s: (layer, s, 0, 0, 0, 0)))
    blk = lambda s: (s // (tb // ntok), 0)
    y, oc_t = pl.pallas_call(
        _ffn_attn_decode_kernel,
        out_shape=(jax.ShapeDtypeStruct(h.shape, h.dtype), jax.ShapeDtypeStruct((ATT_OUT, bsz), F32)),
        grid=(steps,),
        in_specs=([pl.BlockSpec((tm, D_MODEL), lambda s: (s, 0)),
                   _layer_spec(norm_w, ffn_layer), _layer_spec(w_in_bf16, ffn_layer),
                   _layer_spec(w_out_bf16, ffn_layer),
                   pl.BlockSpec((tb, ATT_Q_N), blk)]
                  + [pl.BlockSpec((tb, 2 * ATT_OUT), blk)] * N_ATT_GROUPS + specs),
        out_specs=(pl.BlockSpec((tm, D_MODEL), lambda s: (s, 0)),
                   pl.BlockSpec((ATT_OUT, bsz), lambda s: (0, 0))),
        compiler_params=pltpu.CompilerParams(
            dimension_semantics=("arbitrary",), vmem_limit_bytes=VMEM_LIMIT_BYTES),
        name="ffn_attn_decode",
    )(h, norm_w, w_in_bf16, w_out_bf16, q_rows, *kv_new, *views)
    return y, oc_t.T


PROMPT_TM = 512
PROMPT_FFN_TM = 512
KV_TAIL = (2, ATT_HEADS_PER_GROUP, ATT_HEAD_DIM)


def trunk(x_prompt, x_sample, p, state_gdn, state_gdn_conv, state_ssd, state_ssd_conv, caches):
    bsz, L, _ = x_prompt.shape
    t = bsz * L
    h = x_prompt.reshape(t, D_MODEL)
    new = [[] for _ in range(4)]
    kv_p = None
    seq = lambda a: a.reshape(bsz, L, a.shape[-1])
    bs = x_sample.shape[0]
    assert x_sample.shape[1] == 1
    hs = x_sample.reshape(bs, D_MODEL)
    new_s = [[] for _ in range(2 + N_ATT_GROUPS)]
    gdn_conv_t = jnp.transpose(state_gdn_conv, (0, 2, 1, 3))
    ssd_conv_t = jnp.transpose(state_ssd_conv, (0, 2, 1, 3))
    s_gdn_s = s_ssd_s = None
    for l in range(DEPTH):
        hs = ffn_block(hs, p['ffn1_norm'], p['ffn1_w_in'], p['ffn1_w_out'], l, bs)
        qkv, gate, z, xbc, small = proj_ab(hs, p['mix_norm'], p['w_ab'], l, bs)
        (q_rows, *kvrows_s), = run_jobs("proj_c", _proj_c_job(hs, p['mix_norm'], p['w_c'], l, p['q_norm'][l],
                                                              p['k_norm'][l], bs, False))
        o_a_s, s_gdn_s, c_gdn_s = gdn_decode(qkv, gate, small, gdn_conv_t, state_gdn, s_gdn_s, l, p['gdn_conv_w'][l],
                                             p['gdn_a_log'][l], p['gdn_dt_bias'][l], p['gdn_norm_w'][l])
        o_b_s, s_ssd_s, c_ssd_s = ssd_decode(xbc, z, small, ssd_conv_t, state_ssd, s_ssd_s, l, p['ssd_conv_w'][l],
                                             p['ssd_conv_b'][l], p['ssd_a_log'][l], p['ssd_dt_bias'][l],
                                             p['ssd_d'][l], p['ssd_norm_w'][l])
        h, o_c_s = ffn_attn_decode(h, p['ffn1_norm'], p['ffn1_w_in'], p['ffn1_w_out'], l, q_rows, kvrows_s, caches, l)
        hs = merge_block(hs, o_a_s, o_b_s, o_c_s, p['merge'], l, bs)
        hs = ffn_block(hs, p['ffn2_norm'], p['ffn2_w_in'], p['ffn2_w_out'], l, bs)
        for lst, s in zip(new_s, (c_gdn_s, c_ssd_s, *[r.reshape((bs, 1) + KV_TAIL) for r in kvrows_s])):
            lst.append(s)

        qkv, gate, z, xbc, small = proj_ab(h, p['mix_norm'], p['w_ab'], l, PROMPT_TM)
        (q5, kv5, *kv_p), = run_jobs(
            "proj_c", _proj_c_job(h, p['mix_norm'], p['w_c'], l, p['q_norm'][l], p['k_norm'][l], PROMPT_TM, True,
                                  kv_p, L))
        (o_a, s_gdn, c_gdn), = run_jobs(
            "gdn_prompt", _gdn_prompt_job(seq(qkv), seq(gate), seq(small), p['gdn_conv_w'][l], p['gdn_a_log'][l],
                                          p['gdn_dt_bias'][l], p['gdn_norm_w'][l]))
        o_b, s_ssd, c_ssd = ssd_prompt(seq(xbc), seq(z), seq(small), p['ssd_conv_w'][l], p['ssd_conv_b'][l],
                                       p['ssd_a_log'][l], p['ssd_dt_bias'][l], p['ssd_d'][l], p['ssd_norm_w'][l])
        o5, lse5 = attn_prompt(q5, kv5, bsz)
        h = merge_block(h, o_a.reshape(t, GDN_VW), o_b.reshape(t, SSD_INNER), (o5, lse5), p['merge'], l, PROMPT_TM)
        h = ffn_block(h, p['ffn2_norm'], p['ffn2_w_in'], p['ffn2_w_out'], l, PROMPT_FFN_TM)
        for lst, s in zip(new, (s_gdn, c_gdn, s_ssd, c_ssd)):
            lst.append(s)
    c_gdn_s, c_ssd_s, *kv_new_s = [jnp.stack(lst, axis=0) for lst in new_s]
    st_s = [s_gdn_s, jnp.transpose(c_gdn_s, (0, 2, 1, 3)), s_ssd_s, jnp.transpose(c_ssd_s, (0, 2, 1, 3)), *kv_new_s]
    kv_out = [jnp.transpose(b.reshape((DEPTH, bsz) + KV_TAIL + (b.shape[-1],)), (0, 1, 5, 2, 3, 4)) for b in kv_p]
    return (h.reshape(bsz, L, D_MODEL), [jnp.stack(lst, axis=0) for lst in new] + kv_out,
            hs.reshape(bs, 1, D_MODEL), st_s)


def kernel(x_prompt, x_sample, state_gdn, state_gdn_conv, state_ssd, state_ssd_conv,
           cache_kv_w128, cache_kv_w512, cache_kv_w2048,
           ffn1_norm, ffn1_w_in, ffn1_w_out, mix_norm, w_in, gate_b,
           gdn_conv_w, gdn_a_log, gdn_dt_bias, gdn_norm_w,
           ssd_conv_w, ssd_conv_b, ssd_a_log, ssd_dt_bias, ssd_d, ssd_norm_w,
           q_norm, k_norm, w_br_a, w_br_b, w_br_c, w_out,
           ffn2_norm, ffn2_w_in, ffn2_w_out):
    w_ab, w_c, w_g = prep_weights(w_in)
    row = lambda a: a.reshape(DEPTH, 1, a.shape[-1])
    mix_norm_r = row(mix_norm)
    p = {'ffn1_norm': row(ffn1_norm), 'ffn1_w_in': ffn1_w_in.astype(BF16), 'ffn1_w_out': ffn1_w_out.astype(BF16),
         'mix_norm': mix_norm_r, 'w_ab': w_ab, 'w_c': w_c,
         'merge': (mix_norm_r, w_g, row(gate_b), w_br_a.astype(BF16), w_br_b.astype(BF16), w_br_c.astype(BF16),
                   w_out.astype(BF16)),
         'gdn_conv_w': gdn_conv_w, 'gdn_a_log': gdn_a_log, 'gdn_dt_bias': gdn_dt_bias, 'gdn_norm_w': gdn_norm_w,
         'ssd_conv_w': ssd_conv_w, 'ssd_conv_b': ssd_conv_b, 'ssd_a_log': ssd_a_log,
         'ssd_dt_bias': ssd_dt_bias, 'ssd_d': ssd_d, 'ssd_norm_w': ssd_norm_w,
         'q_norm': q_norm, 'k_norm': k_norm,
         'ffn2_norm': row(ffn2_norm), 'ffn2_w_in': ffn2_w_in.astype(BF16), 'ffn2_w_out': ffn2_w_out.astype(BF16)}
    y_prompt, st_p, y_sample, st_s = trunk(x_prompt, x_sample, p, state_gdn, state_gdn_conv, state_ssd,
                                           state_ssd_conv, [cache_kv_w128, cache_kv_w512, cache_kv_w2048])
    gdn_p, gdn_conv_p, ssd_p, ssd_conv_p, kv128_p, kv512_p, kv2048_p = st_p
    gdn_s, gdn_conv_s, ssd_s, ssd_conv_s, kv128_s, kv512_s, kv2048_s = st_s
    return (y_prompt, y_sample, gdn_p, gdn_s, gdn_conv_p, gdn_conv_s, ssd_p, ssd_s, ssd_conv_p, ssd_conv_s,
            kv128_p, kv128_s, kv512_p, kv512_s, kv2048_p, kv2048_s)
```
